```python
import math
import jax, jax.numpy as jnp
from jax import lax
import numpy as np

D_MODEL = 1024
BATCH = 1
SEQ = 16384
DEPTH = 1

HEAD_DIM = 64
N_HEADS_A = 8
N_KV_A = 2
N_HEADS_B = 8
D_A = N_HEADS_A * HEAD_DIM
D_KV_A = N_KV_A * HEAD_DIM
D_B = N_HEADS_B * HEAD_DIM
D_MIX = D_A + D_B
D_IN = D_A + 2 * D_KV_A + 3 * D_B
D_FF = 4 * D_MODEL
D_PLE = 256
GRID_W = 64
ROPE_THETA = 10000.0
ROPE_HALF = HEAD_DIM // 2
Q_BLOCK = 128
DILATED_PATTERNS = ((128, 1), (512, 4), (2048, 16))
N_BUCKETS = 32
MAX_DISTANCE = 1024
EPS = 1e-6
NEG_BIG = -1e30

kernel_name = "hymba_axial_gqa_dilated_swa_encoder_layer"


def rms_norm(x, g):
    xf = x.astype(jnp.float32)
    y = xf * lax.rsqrt(jnp.mean(xf * xf, axis=-1, keepdims=True) + EPS)
    return (y * g.astype(jnp.float32)).astype(x.dtype)


def axial_rope_tables(n_tokens):
    rows = n_tokens // GRID_W
    row = jnp.broadcast_to(jnp.arange(rows)[:, None], (rows, GRID_W)).reshape(-1).astype(jnp.float32)
    col = jnp.broadcast_to(jnp.arange(GRID_W)[None, :], (rows, GRID_W)).reshape(-1).astype(jnp.float32)
    n_axis = ROPE_HALF // 2
    inv_freq = ROPE_THETA ** (-jnp.arange(n_axis, dtype=jnp.float32) / n_axis)
    ang = jnp.concatenate([row[:, None] * inv_freq, col[:, None] * inv_freq], axis=-1)
    return jnp.cos(ang), jnp.sin(ang)


def apply_rope(x, cos, sin):
    xf = x.astype(jnp.float32)
    x1, x2 = xf[..., :ROPE_HALF], xf[..., ROPE_HALF:]
    c, s = cos[None, :, None, :], sin[None, :, None, :]
    return jnp.concatenate([x1 * c - x2 * s, x2 * c + x1 * s], axis=-1).astype(x.dtype)


def mixer_a(q, k, v, g_q, g_k, cos, sin):
    q = apply_rope(rms_norm(q, g_q), cos, sin)
    k = apply_rope(rms_norm(k, g_k), cos, sin)
    b, s_len = q.shape[0], q.shape[1]
    grp = N_HEADS_A // N_KV_A
    nblk = s_len // Q_BLOCK
    qb = q.reshape(b, nblk, Q_BLOCK, N_KV_A, grp, HEAD_DIM).transpose(1, 0, 2, 3, 4, 5)
    scale = HEAD_DIM ** -0.5

    def block(q_blk):
        sc = jnp.einsum('bqkgd,bskd->bkgqs', q_blk, k, preferred_element_type=jnp.float32) * scale
        pr = jax.nn.softmax(sc, axis=-1)
        return jnp.einsum('bkgqs,bskd->bqkgd', pr.astype(v.dtype), v)

    o = lax.map(block, qb)
    return o.transpose(1, 0, 2, 3, 4, 5).reshape(b, s_len, D_A)


def t5_bucket(rel):
    nb = N_BUCKETS // 2
    max_exact = nb // 2
    side = jnp.where(rel > 0, nb, 0)
    n = jnp.abs(rel)
    large = max_exact + (jnp.log(jnp.maximum(n, max_exact).astype(jnp.float32) / max_exact)
                         / math.log(MAX_DISTANCE / max_exact) * (nb - max_exact)).astype(jnp.int32)
    large = jnp.minimum(large, nb - 1)
    return side + jnp.where(n < max_exact, n, large)


def dilated_pattern(q, k, v, rel_bias, window, dilation):
    b, s_len, h, d = q.shape
    length = s_len // dilation
    half = window // (2 * dilation)
    blk = half
    nblk = -(-length // blk)
    lp = nblk * blk

    def to_sub(x):
        x = x.reshape(b, length, dilation, h, d).transpose(0, 2, 1, 3, 4)
        return jnp.pad(x, ((0, 0), (0, 0), (0, lp - length), (0, 0), (0, 0)))

    def band(x):
        xp = jnp.pad(to_sub(x), ((0, 0), (0, 0), (blk, blk), (0, 0), (0, 0)))
        xb = xp.reshape(b, dilation, nblk + 2, blk, h, d)
        return jnp.concatenate([xb[:, :, :-2], xb[:, :, 1:-1], xb[:, :, 2:]], axis=3)

    qs = to_sub(q).reshape(b, dilation, nblk, blk, h, d)
    ks, vs = band(k), band(v)
    qi = jnp.arange(blk)
    kj = jnp.arange(3 * blk)
    rel = kj[None, :] - blk - qi[:, None]
    key_m = jnp.arange(nblk)[:, None] * blk - blk + kj[None, :]
    valid = ((jnp.abs(rel) <= half)[None, :, :]
             & (key_m >= 0)[:, None, :] & (key_m < length)[:, None, :])
    bias = rel_bias[t5_bucket(rel * dilation)].astype(jnp.float32)
    bias = bias.transpose(0, 2, 1)[None, None, None]
    sc = jnp.einsum('brnqhd,brnkhd->brnqhk', qs, ks, preferred_element_type=jnp.float32) * (HEAD_DIM ** -0.5)
    sc = jnp.where(valid[None, None, :, :, None, :], sc + bias, NEG_BIG)
    mx = jnp.max(sc, axis=-1)
    e = jnp.exp(sc - mx[..., None])
    den = jnp.sum(e, axis=-1)
    o = jnp.einsum('brnqhk,brnkhd->brnqhd', e.astype(vs.dtype), vs,
                   preferred_element_type=jnp.float32) / den[..., None]

    def from_sub(y):
        y = y.reshape((b, dilation, lp) + y.shape[4:])[:, :, :length]
        return jnp.moveaxis(y, 1, 2).reshape((b, s_len) + y.shape[3:])

    return from_sub(o), from_sub(mx), from_sub(den)


def mixer_b(q, k, v, rel_bias):
    outs, mxs, dens = [], [], []
    for window, dilation in DILATED_PATTERNS:
        o, mx, den = dilated_pattern(q, k, v, rel_bias, window, dilation)
        outs.append(o); mxs.append(mx); dens.append(den)
    m_all = jnp.maximum(jnp.maximum(mxs[0], mxs[1]), mxs[2])
    w = [den * jnp.exp(mx - m_all) for den, mx in zip(dens, mxs)]
    w_sum = w[0] + w[1] + w[2]
    y = (w[0][..., None] * outs[0] + w[1][..., None] * outs[1] + w[2][..., None] * outs[2]) / w_sum[..., None]
    b, s_len = q.shape[0], q.shape[1]
    return y.reshape(b, s_len, D_B).astype(q.dtype)


def setup_inputs(seed: int = 0) -> dict:
    key = jax.random.key(seed)
    ks = jax.random.split(key, 20)
    nrm = lambda k, shape, scale: jax.random.normal(k, shape, jnp.float32) * scale
    gain = lambda k, shape: 1.0 + 0.05 * jax.random.normal(k, shape, jnp.float32)
    return {
        "x": nrm(ks[0], (BATCH, SEQ, D_MODEL), 1.0),
        "p": nrm(ks[1], (DEPTH, BATCH, SEQ, D_PLE), 1.0),
        "w_in": nrm(ks[2], (DEPTH, D_MODEL, D_IN), D_MODEL ** -0.5),
        "g_attn_pre": gain(ks[3], (DEPTH, D_MODEL)),
        "g_q": gain(ks[4], (DEPTH, HEAD_DIM)),
        "g_k": gain(ks[5], (DEPTH, HEAD_DIM)),
        "g_out_a": gain(ks[6], (DEPTH, D_A)),
        "g_out_b": gain(ks[7], (DEPTH, D_B)),
        "w_out": nrm(ks[8], (DEPTH, D_MIX, D_MODEL), D_MIX ** -0.5),
        "g_attn_post": gain(ks[9], (DEPTH, D_MODEL)),
        "rel_bias": nrm(ks[10], (N_BUCKETS, N_HEADS_B), 0.5),
        "g_mlp_pre": gain(ks[11], (DEPTH, D_MODEL)),
        "w_ff1": nrm(ks[12], (DEPTH, D_MODEL, D_FF), D_MODEL ** -0.5),
        "w_ff2": nrm(ks[13], (DEPTH, D_FF, D_MODEL), D_FF ** -0.5),
        "g_mlp_post": gain(ks[14], (DEPTH, D_MODEL)),
        "g_ple": gain(ks[15], (DEPTH, D_MODEL)),
        "w_ple_gate": nrm(ks[16], (DEPTH, D_MODEL, D_MODEL), D_MODEL ** -0.5),
        "w_ple_proj": nrm(ks[17], (DEPTH, D_PLE, D_MODEL), D_PLE ** -0.5),
    }


def reference(x, p, w_in, g_attn_pre, g_q, g_k, g_out_a, g_out_b, w_out, g_attn_post, rel_bias,
              g_mlp_pre, w_ff1, w_ff2, g_mlp_post, g_ple, w_ple_gate, w_ple_proj):
    b, s_len = x.shape[0], x.shape[1]
    cos, sin = axial_rope_tables(s_len)
    split_at = [D_A, D_A + D_KV_A, D_A + 2 * D_KV_A, D_A + 2 * D_KV_A + D_B, D_A + 2 * D_KV_A + 2 * D_B]
    h = x
    for i in range(DEPTH):
        xn = rms_norm(h, g_attn_pre[i])
        proj = jnp.einsum('bsd,de->bse', xn, w_in[i])
        qa, ka, va, qb, kb, vb = jnp.split(proj, split_at, axis=-1)
        ya = mixer_a(qa.reshape(b, s_len, N_HEADS_A, HEAD_DIM),
                     ka.reshape(b, s_len, N_KV_A, HEAD_DIM),
                     va.reshape(b, s_len, N_KV_A, HEAD_DIM), g_q[i], g_k[i], cos, sin)
        yb = mixer_b(qb.reshape(b, s_len, N_HEADS_B, HEAD_DIM),
                     kb.reshape(b, s_len, N_HEADS_B, HEAD_DIM),
                     vb.reshape(b, s_len, N_HEADS_B, HEAD_DIM), rel_bias)
        y = jnp.concatenate([rms_norm(ya, g_out_a[i]), rms_norm(yb, g_out_b[i])], axis=-1)
        y = jnp.einsum('bse,ed->bsd', y, w_out[i])
        h = h + rms_norm(y, g_attn_post[i])
        xn = rms_norm(h, g_mlp_pre[i])
        f = jnp.square(jax.nn.relu(jnp.einsum('bsd,df->bsf', xn, w_ff1[i])))
        f = jnp.einsum('bsf,fd->bsd', f, w_ff2[i])
        h = h + rms_norm(f, g_mlp_post[i])
        gate = jax.nn.sigmoid(jnp.einsum('bsd,de->bse', rms_norm(h, g_ple[i]), w_ple_gate[i]))
        h = h + gate * jnp.einsum('bsp,pd->bsd', p[i], w_ple_proj[i])
    return h
```

```python
import functools
import math

import jax
import jax.numpy as jnp
from jax import lax
from jax.experimental import pallas as pl
from jax.experimental.pallas import tpu as pltpu

D_MODEL = 1024
HEAD_DIM = 64
N_HEADS_A = 8
N_KV_A = 2
N_HEADS_B = 8
D_A = N_HEADS_A * HEAD_DIM
D_KV_A = N_KV_A * HEAD_DIM
D_B = N_HEADS_B * HEAD_DIM
D_FF = 4 * D_MODEL
D_PLE = 256
GRID_W = 64
ROPE_THETA = 10000.0
ROPE_HALF = HEAD_DIM // 2
DILATED_PATTERNS = ((128, 1), (512, 4), (2048, 16))
N_BUCKETS = 32
MAX_DISTANCE = 1024
EPS = 1e-6
NEG_BIG = -1e30

LOG2E = math.log2(math.e)
Q_SCALE = HEAD_DIM ** -0.5 * LOG2E

LANES = 128
VMEM_LIMIT = 56 * 1024 * 1024

TM_PROJ = 512
BQ_A = 128
NK_A = 512
BQ_B = 256
NK_B = 256
B_REACH = 1024
TM_OUT = 512
TM_FFN = 512
FF_CHUNK = 1024

_T_QA, _T_KA, _T_VA, _T_QB, _T_VB = 0, 512, 640, 768, 1280
_T_ROWS = 1792


def _rms_rows(x, g_row):
    ms = jnp.mean(x * x, axis=-1, keepdims=True)
    return x * lax.rsqrt(ms + EPS) * g_row


def _proj_kernel(x_ref, g_ref, wt_ref, wkb_ref, gq_ref, gk_ref, cos_ref, sin_ref,
                 qa_ref, ka_ref, va_ref, qb_ref, kb_ref, vb_ref):
    xn = _rms_rows(x_ref[...], g_ref[...]).astype(jnp.bfloat16)
    pt = lax.dot_general(wt_ref[...], xn, (((1,), (1,)), ((), ())),
                         preferred_element_type=jnp.float32)
    kb = jnp.dot(xn, wkb_ref[...], preferred_element_type=jnp.float32)
    kb_ref[...] = kb.astype(jnp.bfloat16)

    cos = cos_ref[...]
    sin = sin_ref[...]

    def norm_rope(blk, g_col):
        ms = jnp.mean(blk * blk, axis=0, keepdims=True)
        y = blk * lax.rsqrt(ms + EPS) * g_col
        y1, y2 = y[:ROPE_HALF], y[ROPE_HALF:]
        return jnp.concatenate([y1 * cos - y2 * sin, y2 * cos + y1 * sin], axis=0)

    gq = gq_ref[...]
    gk = gk_ref[...]
    for h in range(N_HEADS_A):
        r0 = _T_QA + h * HEAD_DIM
        q = norm_rope(pt[r0:r0 + HEAD_DIM], gq) * Q_SCALE
        qa_ref[h * HEAD_DIM:(h + 1) * HEAD_DIM, :] = q.astype(jnp.bfloat16)
    k_heads = [norm_rope(pt[_T_KA + g * HEAD_DIM:_T_KA + (g + 1) * HEAD_DIM], gk)
               for g in range(N_KV_A)]
    ka_ref[...] = jnp.concatenate(k_heads, axis=0).T.astype(jnp.bfloat16)

    tm = x_ref.shape[0]
    va = pt[_T_VA:_T_VA + D_KV_A].astype(jnp.bfloat16)
    for c in range(tm // NK_A):
        va_ref[c] = va[:, c * NK_A:(c + 1) * NK_A]
    qb_ref[...] = (pt[_T_QB:_T_QB + D_B] * Q_SCALE).astype(jnp.bfloat16)
    vb = pt[_T_VB:_T_VB + D_B].astype(jnp.bfloat16)
    for c in range(tm // NK_B):
        vb_ref[c] = vb[:, c * NK_B:(c + 1) * NK_B]


def _proj_call(x2, g_pre, wt, wkb, gq, gk, cos_t, sin_t):
    s_len = x2.shape[0]
    tm = TM_PROJ
    const = lambda i: (0, 0)
    return pl.pallas_call(
        _proj_kernel,
        grid=(s_len // tm,),
        in_specs=[
            pl.BlockSpec((tm, D_MODEL), lambda i: (i, 0)),
            pl.BlockSpec((1, D_MODEL), const),
            pl.BlockSpec((_T_ROWS, D_MODEL), const),
            pl.BlockSpec((D_MODEL, D_B), const),
            pl.BlockSpec((HEAD_DIM, 1), const),
            pl.BlockSpec((HEAD_DIM, 1), const),
            pl.BlockSpec((ROPE_HALF, tm), lambda i: (0, i)),
            pl.BlockSpec((ROPE_HALF, tm), lambda i: (0, i)),
        ],
        out_specs=[
            pl.BlockSpec((D_A, tm), lambda i: (0, i)),
            pl.BlockSpec((tm, D_KV_A), lambda i: (i, 0)),
            pl.BlockSpec((tm // NK_A, D_KV_A, NK_A), lambda i: (i, 0, 0)),
            pl.BlockSpec((D_B, tm), lambda i: (0, i)),
            pl.BlockSpec((tm, D_B), lambda i: (i, 0)),
            pl.BlockSpec((tm // NK_B, D_B, NK_B), lambda i: (i, 0, 0)),
        ],
        out_shape=[
            jax.ShapeDtypeStruct((D_A, s_len), jnp.bfloat16),
            jax.ShapeDtypeStruct((s_len, D_KV_A), jnp.bfloat16),
            jax.ShapeDtypeStruct((s_len // NK_A, D_KV_A, NK_A), jnp.bfloat16),
            jax.ShapeDtypeStruct((D_B, s_len), jnp.bfloat16),
            jax.ShapeDtypeStruct((s_len, D_B), jnp.bfloat16),
            jax.ShapeDtypeStruct((s_len // NK_B, D_B, NK_B), jnp.bfloat16),
        ],
        compiler_params=pltpu.CompilerParams(
            dimension_semantics=("arbitrary",), vmem_limit_bytes=VMEM_LIMIT),
        name="proj",
    )(x2, g_pre, wt, wkb, gq, gk, cos_t, sin_t)


def _softmax_step(s, v_t, m, l, acc_ref):
    m_new = jnp.maximum(m, jnp.max(s, axis=0, keepdims=True))
    alpha = jnp.exp2(m - m_new)
    p = jnp.exp2(s - m_new)
    l_new = alpha * l + jnp.sum(p, axis=0, keepdims=True)
    pv = jnp.dot(v_t, p.astype(jnp.bfloat16), preferred_element_type=jnp.float32)
    acc_ref[...] = acc_ref[...] * alpha + pv
    return m_new, l_new


def _attn_a_kernel(q_ref, k_ref, v_ref, o_ref, qt_ref, acc_ref):
    bq = q_ref.shape[1]
    m_cols = N_HEADS_A * bq
    grp = N_HEADS_A // N_KV_A
    zeros = jnp.zeros((HEAD_DIM, bq), jnp.bfloat16)
    for h in range(N_HEADS_A):
        qh = q_ref[h * HEAD_DIM:(h + 1) * HEAD_DIM, :]
        parts = [qh if g == h // grp else zeros for g in range(N_KV_A)]
        qt_ref[:, h * bq:(h + 1) * bq] = jnp.concatenate(parts, axis=0)
    acc_ref[...] = jnp.zeros_like(acc_ref)

    def body(c, carry):
        m, l = carry
        start = pl.multiple_of(c * NK_A, NK_A)
        s = jnp.dot(k_ref[pl.ds(start, NK_A), :], qt_ref[...],
                    preferred_element_type=jnp.float32)
        return _softmax_step(s, v_ref[c], m, l, acc_ref)

    m0 = jnp.full((1, m_cols), NEG_BIG, jnp.float32)
    l0 = jnp.zeros((1, m_cols), jnp.float32)
    _, l = lax.fori_loop(0, k_ref.shape[0] // NK_A, body, (m0, l0))
    inv = 1.0 / l
    for h in range(N_HEADS_A):
        g = h // grp
        cols = slice(h * bq, (h + 1) * bq)
        o_ref[h * HEAD_DIM:(h + 1) * HEAD_DIM, :] = (
            acc_ref[g * HEAD_DIM:(g + 1) * HEAD_DIM, cols] * inv[:, cols])


def _attn_a_call(qa_t, ka, va_t):
    s_len = ka.shape[0]
    bq = BQ_A
    return pl.pallas_call(
        _attn_a_kernel,
        grid=(s_len // bq,),
        in_specs=[
            pl.BlockSpec((D_A, bq), lambda i: (0, i)),
            pl.BlockSpec((s_len, D_KV_A), lambda i: (0, 0)),
            pl.BlockSpec((s_len // NK_A, D_KV_A, NK_A), lambda i: (0, 0, 0)),
        ],
        out_specs=pl.BlockSpec((D_A, bq), lambda i: (0, i)),
        out_shape=jax.ShapeDtypeStruct((D_A, s_len), jnp.float32),
        scratch_shapes=[
            pltpu.VMEM((D_KV_A, N_HEADS_A * bq), jnp.bfloat16),
            pltpu.VMEM((D_KV_A, N_HEADS_A * bq), jnp.float32),
        ],
        compiler_params=pltpu.CompilerParams(
            dimension_semantics=("arbitrary",), vmem_limit_bytes=VMEM_LIMIT),
        name="attn_a",
    )(qa_t, ka, va_t)


def _attn_b_kernel(q_ref, k_ref, v_ref, bias_ref, o_ref, qt_ref, acc_ref):
    i = pl.program_id(1)
    bq = q_ref.shape[1]
    zeros = jnp.zeros((HEAD_DIM, bq), jnp.bfloat16)
    q0 = q_ref[:HEAD_DIM, :]
    q1 = q_ref[HEAD_DIM:, :]
    qt_ref[:, :bq] = jnp.concatenate([q0, zeros], axis=0)
    qt_ref[:, bq:] = jnp.concatenate([zeros, q1], axis=0)
    acc_ref[...] = jnp.zeros_like(acc_ref)

    n_chunks = k_ref.shape[0] // NK_B
    back = B_REACH // NK_B
    span = (bq + 2 * B_REACH) // NK_B
    first = i * (bq // NK_B) - back
    c_lo = jnp.maximum(0, -first)
    c_hi = jnp.minimum(span, n_chunks - first)

    def body(c, carry):
        m, l = carry
        kc = first + c
        k_start = pl.multiple_of(kc * NK_B, NK_B)
        b_start = pl.multiple_of(c * NK_B, NK_B)
        s = jnp.dot(k_ref[pl.ds(k_start, NK_B), :], qt_ref[...],
                    preferred_element_type=jnp.float32)
        s = s + bias_ref[pl.ds(b_start, NK_B), :]
        return _softmax_step(s, v_ref[kc], m, l, acc_ref)

    m0 = jnp.full((1, 2 * bq), NEG_BIG, jnp.float32)
    l0 = jnp.zeros((1, 2 * bq), jnp.float32)
    _, l = lax.fori_loop(c_lo, c_hi, body, (m0, l0))
    inv = 1.0 / l
    o_ref[:HEAD_DIM, :] = acc_ref[:HEAD_DIM, :bq] * inv[:, :bq]
    o_ref[HEAD_DIM:, :] = acc_ref[HEAD_DIM:, bq:] * inv[:, bq:]


def _attn_b_call(qb_t, kb, vb_t, bias_t):
    s_len = kb.shape[0]
    bq = BQ_B
    pairs = N_HEADS_B // 2
    span_rows = bq + 2 * B_REACH
    return pl.pallas_call(
        _attn_b_kernel,
        grid=(pairs, s_len // bq),
        in_specs=[
            pl.BlockSpec((2 * HEAD_DIM, bq), lambda j, i: (j, i)),
            pl.BlockSpec((s_len, 2 * HEAD_DIM), lambda j, i: (0, j)),
            pl.BlockSpec((s_len // NK_B, 2 * HEAD_DIM, NK_B), lambda j, i: (0, j, 0)),
            pl.BlockSpec((None, span_rows, 2 * bq), lambda j, i: (j, 0, 0)),
        ],
        out_specs=pl.BlockSpec((2 * HEAD_DIM, bq), lambda j, i: (j, i)),
        out_shape=jax.ShapeDtypeStruct((D_B, s_len), jnp.float32),
        scratch_shapes=[
            pltpu.VMEM((2 * HEAD_DIM, 2 * bq), jnp.bfloat16),
            pltpu.VMEM((2 * HEAD_DIM, 2 * bq), jnp.float32),
        ],
        compiler_params=pltpu.CompilerParams(
            dimension_semantics=("arbitrary", "arbitrary"), vmem_limit_bytes=VMEM_LIMIT),
        name="attn_b",
    )(qb_t, kb, vb_t, bias_t)


def _t5_bucket_index(rel):
    nb = N_BUCKETS // 2
    max_exact = nb // 2
    side = jnp.where(rel > 0, nb, 0)
    n = jnp.abs(rel)
    large = max_exact + (jnp.log(jnp.maximum(n, max_exact).astype(jnp.float32) / max_exact)
                         / math.log(MAX_DISTANCE / max_exact) * (nb - max_exact)).astype(jnp.int32)
    large = jnp.minimum(large, nb - 1)
    return side + jnp.where(n < max_exact, n, large)


def _dilated_bias_tiles(rel_bias):
    bq = BQ_B
    reach_all = B_REACH + bq - 1
    delta = jnp.arange(-reach_all, reach_all + 1)
    count = jnp.zeros(delta.shape, jnp.float32)
    for window, dilation in DILATED_PATTERNS:
        inside = (delta % dilation == 0) & (jnp.abs(delta) <= window // 2)
        count = count + inside.astype(jnp.float32)
    table = rel_bias[_t5_bucket_index(delta)].astype(jnp.float32)
    table = jnp.where((count > 0)[:, None],
                      (table + jnp.log(jnp.maximum(count, 1.0))[:, None]) * LOG2E, NEG_BIG)
    key_row = jnp.arange(bq + 2 * B_REACH)[:, None]
    qry_col = jnp.arange(bq)[None, :]
    idx = key_row - B_REACH - qry_col + reach_all
    tiles = table[idx]
    tiles = tiles.reshape(idx.shape[0], bq, N_HEADS_B // 2, 2)
    return tiles.transpose(2, 0, 3, 1).reshape(N_HEADS_B // 2, idx.shape[0], 2 * bq)


def _out_kernel(ya_ref, yb_ref, x_ref, ga_ref, gb_ref, w_ref, gpost_ref, h_ref):
    def norm_t(y_t, g_col):
        ms = jnp.mean(y_t * y_t, axis=0, keepdims=True)
        return (y_t * lax.rsqrt(ms + EPS) * g_col).astype(jnp.bfloat16)

    y_t = jnp.concatenate([norm_t(ya_ref[...], ga_ref[...]),
                           norm_t(yb_ref[...], gb_ref[...])], axis=0)
    y = lax.dot_general(y_t, w_ref[...], (((0,), (0,)), ((), ())),
                        preferred_element_type=jnp.float32)
    h_ref[...] = x_ref[...] + _rms_rows(y, gpost_ref[...])


def _out_call(ya_t, yb_t, x2, ga, gb, w_out, g_post):
    s_len = x2.shape[0]
    tm = TM_OUT
    const = lambda i: (0, 0)
    return pl.pallas_call(
        _out_kernel,
        grid=(s_len // tm,),
        in_specs=[
            pl.BlockSpec((D_A, tm), lambda i: (0, i)),
            pl.BlockSpec((D_B, tm), lambda i: (0, i)),
            pl.BlockSpec((tm, D_MODEL), lambda i: (i, 0)),
            pl.BlockSpec((D_A, 1), const),
            pl.BlockSpec((D_B, 1), const),
            pl.BlockSpec((D_A + D_B, D_MODEL), const),
            pl.BlockSpec((1, D_MODEL), const),
        ],
        out_specs=pl.BlockSpec((tm, D_MODEL), lambda i: (i, 0)),
        out_shape=jax.ShapeDtypeStruct((s_len, D_MODEL), jnp.float32),
        compiler_params=pltpu.CompilerParams(
            dimension_semantics=("arbitrary",), vmem_limit_bytes=VMEM_LIMIT),
        name="out_proj",
    )(ya_t, yb_t, x2, ga, gb, w_out, g_post)


def _ffn_kernel(h_ref, p_ref, g1_ref, w1_ref, w2_ref, g2_ref, g3_ref, wg_ref, wp_ref, o_ref):
    h = h_ref[...]
    xn = _rms_rows(h, g1_ref[...]).astype(jnp.bfloat16)
    f = None
    for c in range(D_FF // FF_CHUNK):
        cols = slice(c * FF_CHUNK, (c + 1) * FF_CHUNK)
        u = jnp.dot(xn, w1_ref[:, cols], preferred_element_type=jnp.float32)
        u = jnp.square(jnp.maximum(u, 0.0)).astype(jnp.bfloat16)
        part = jnp.dot(u, w2_ref[cols, :], preferred_element_type=jnp.float32)
        f = part if f is None else f + part
    h = h + _rms_rows(f, g2_ref[...])
    gate_in = _rms_rows(h, g3_ref[...]).astype(jnp.bfloat16)
    gate = jax.nn.sigmoid(jnp.dot(gate_in, wg_ref[...], preferred_element_type=jnp.float32))
    emb = jnp.dot(p_ref[...].astype(jnp.bfloat16), wp_ref[...], preferred_element_type=jnp.float32)
    o_ref[...] = h + gate * emb


def _ffn_call(h1, p2, g1, w1, w2, g2, g3, wg, wp):
    s_len = h1.shape[0]
    tm = TM_FFN
    const = lambda i: (0, 0)
    resident = functools.partial(pl.BlockSpec, index_map=const, pipeline_mode=pl.Buffered(1))
    return pl.pallas_call(
        _ffn_kernel,
        grid=(s_len // tm,),
        in_specs=[
            pl.BlockSpec((tm, D_MODEL), lambda i: (i, 0)),
            pl.BlockSpec((tm, D_PLE), lambda i: (i, 0)),
            pl.BlockSpec((1, D_MODEL), const),
            resident((D_MODEL, D_FF)),
            resident((D_FF, D_MODEL)),
            pl.BlockSpec((1, D_MODEL), const),
            pl.BlockSpec((1, D_MODEL), const),
            resident((D_MODEL, D_MODEL)),
            resident((D_PLE, D_MODEL)),
        ],
        out_specs=pl.BlockSpec((tm, D_MODEL), lambda i: (i, 0)),
        out_shape=jax.ShapeDtypeStruct((s_len, D_MODEL), jnp.float32),
        compiler_params=pltpu.CompilerParams(
            dimension_semantics=("arbitrary",), vmem_limit_bytes=VMEM_LIMIT),
        name="ffn_ple",
    )(h1, p2, g1, w1, w2, g2, g3, wg, wp)


def _rope_tables_t(n_tokens):
    tok = jnp.arange(n_tokens)
    row = (tok // GRID_W).astype(jnp.float32)
    col = (tok % GRID_W).astype(jnp.float32)
    n_axis = ROPE_HALF // 2
    inv_freq = ROPE_THETA ** (-jnp.arange(n_axis, dtype=jnp.float32) / n_axis)
    ang = jnp.concatenate([inv_freq[:, None] * row[None, :], inv_freq[:, None] * col[None, :]], axis=0)
    return jnp.cos(ang), jnp.sin(ang)


def _layer(h, p_i, w_in, g_attn_pre, g_q, g_k, g_out_a, g_out_b, w_out, g_attn_post, bias_t,
           g_mlp_pre, w_ff1, w_ff2, g_mlp_post, g_ple, w_ple_gate, w_ple_proj, cos_t, sin_t):
    bf = jnp.bfloat16
    o_ka, o_va = D_A, D_A + D_KV_A
    o_qb = D_A + 2 * D_KV_A
    o_kb, o_vb = o_qb + D_B, o_qb + 2 * D_B
    wt = jnp.concatenate([w_in[:, :o_qb + D_B], w_in[:, o_vb:]], axis=1).T.astype(bf)
    wkb = w_in[:, o_kb:o_vb].astype(bf)
    row = lambda g: g.reshape(1, -1)
    col = lambda g: g.reshape(-1, 1)

    qa_t, ka, va_t, qb_t, kb, vb_t = _proj_call(
        h, row(g_attn_pre), wt, wkb, col(g_q), col(g_k), cos_t, sin_t)
    ya_t = _attn_a_call(qa_t, ka, va_t)
    yb_t = _attn_b_call(qb_t, kb, vb_t, bias_t)
    h1 = _out_call(ya_t, yb_t, h, col(g_out_a), col(g_out_b), w_out.astype(bf), row(g_attn_post))
    return _ffn_call(h1, p_i, row(g_mlp_pre), w_ff1.astype(bf), w_ff2.astype(bf), row(g_mlp_post),
                     row(g_ple), w_ple_gate.astype(bf), w_ple_proj.astype(bf))


def kernel(x, p, w_in, g_attn_pre, g_q, g_k, g_out_a, g_out_b, w_out, g_attn_post, rel_bias,
           g_mlp_pre, w_ff1, w_ff2, g_mlp_post, g_ple, w_ple_gate, w_ple_proj):
    b, s_len, _ = x.shape
    cos_t, sin_t = _rope_tables_t(s_len)
    bias_t = _dilated_bias_tiles(rel_bias)
    outs = []
    for bi in range(b):
        h = x[bi]
        for i in range(w_in.shape[0]):
            h = _layer(h, p[i, bi], w_in[i], g_attn_pre[i], g_q[i], g_k[i], g_out_a[i], g_out_b[i],
                       w_out[i], g_attn_post[i], bias_t, g_mlp_pre[i], w_ff1[i], w_ff2[i],
                       g_mlp_post[i], g_ple[i], w_ple_gate[i], w_ple_proj[i], cos_t, sin_t)
        outs.append(h)
    return jnp.stack(outs, axis=0)
```

```python
import functools
import math

import jax
import jax.numpy as jnp
from jax import lax
from jax.experimental import pallas as pl
from jax.experimental.pallas import tpu as pltpu

D_MODEL = 1024
HEAD_DIM = 64
N_HEADS_A = 8
N_KV_A = 2
N_HEADS_B = 8
D_A = N_HEADS_A * HEAD_DIM
D_KV_A = N_KV_A * HEAD_DIM
D_B = N_HEADS_B * HEAD_DIM
D_FF = 4 * D_MODEL
D_PLE = 256
GRID_W = 64
ROPE_THETA = 10000.0
ROPE_HALF = HEAD_DIM // 2
DILATED_PATTERNS = ((128, 1), (512, 4), (2048, 16))
N_BUCKETS = 32
MAX_DISTANCE = 1024
EPS = 1e-6
NEG_BIG = -1e30

LOG2E = math.log2(math.e)
Q_SCALE = HEAD_DIM ** -0.5 * LOG2E

LANES = 128
VMEM_LIMIT = 56 * 1024 * 1024

TM_PROJ = 512
BQ_A = 128
NK_A = 512
UNROLL_A = 4
BQ_B = 256
NK_B = 256
B_REACH = 1024
TM_OUT = 512
TM_FFN = 512
FF_CHUNK = 1024

_T_QA, _T_KA, _T_VA, _T_QB, _T_VB = 0, 512, 640, 768, 1280
_T_ROWS = 1792


def _rms_rows(x, g_row):
    ms = jnp.mean(x * x, axis=-1, keepdims=True)
    return x * lax.rsqrt(ms + EPS) * g_row


def _proj_kernel(x_ref, g_ref, wt_ref, wkb_ref, gq_ref, gk_ref, cos_ref, sin_ref,
                 qa_ref, ka_ref, va_ref, qb_ref, kb_ref, vb_ref):
    xn = _rms_rows(x_ref[...], g_ref[...]).astype(jnp.bfloat16)
    pt = lax.dot_general(wt_ref[...], xn, (((1,), (1,)), ((), ())),
                         preferred_element_type=jnp.float32)
    kb = jnp.dot(xn, wkb_ref[...], preferred_element_type=jnp.float32)
    kb_ref[...] = kb.astype(jnp.bfloat16)

    cos = cos_ref[...]
    sin = sin_ref[...]

    def norm_rope(blk, g_col):
        ms = jnp.mean(blk * blk, axis=0, keepdims=True)
        y = blk * lax.rsqrt(ms + EPS) * g_col
        y1, y2 = y[:ROPE_HALF], y[ROPE_HALF:]
        return jnp.concatenate([y1 * cos - y2 * sin, y2 * cos + y1 * sin], axis=0)

    gq = gq_ref[...]
    gk = gk_ref[...]
    for h in range(N_HEADS_A):
        r0 = _T_QA + h * HEAD_DIM
        q = norm_rope(pt[r0:r0 + HEAD_DIM], gq) * Q_SCALE
        qa_ref[h * HEAD_DIM:(h + 1) * HEAD_DIM, :] = q.astype(jnp.bfloat16)
    k_heads = [norm_rope(pt[_T_KA + g * HEAD_DIM:_T_KA + (g + 1) * HEAD_DIM], gk)
               for g in range(N_KV_A)]
    ka_ref[...] = jnp.concatenate(k_heads, axis=0).T.astype(jnp.bfloat16)

    tm = x_ref.shape[0]
    va = pt[_T_VA:_T_VA + D_KV_A].astype(jnp.bfloat16)
    for c in range(tm // NK_A):
        va_ref[c] = va[:, c * NK_A:(c + 1) * NK_A]
    qb_ref[...] = (pt[_T_QB:_T_QB + D_B] * Q_SCALE).astype(jnp.bfloat16)
    vb = pt[_T_VB:_T_VB + D_B].astype(jnp.bfloat16)
    for c in range(tm // NK_B):
        vb_ref[c] = vb[:, c * NK_B:(c + 1) * NK_B]


def _proj_call(x2, g_pre, wt, wkb, gq, gk, cos_t, sin_t):
    s_len = x2.shape[0]
    tm = TM_PROJ
    const = lambda i: (0, 0)
    return pl.pallas_call(
        _proj_kernel,
        grid=(s_len // tm,),
        in_specs=[
            pl.BlockSpec((tm, D_MODEL), lambda i: (i, 0)),
            pl.BlockSpec((1, D_MODEL), const),
            pl.BlockSpec((_T_ROWS, D_MODEL), const),
            pl.BlockSpec((D_MODEL, D_B), const),
            pl.BlockSpec((HEAD_DIM, 1), const),
            pl.BlockSpec((HEAD_DIM, 1), const),
            pl.BlockSpec((ROPE_HALF, tm), lambda i: (0, i)),
            pl.BlockSpec((ROPE_HALF, tm), lambda i: (0, i)),
        ],
        out_specs=[
            pl.BlockSpec((D_A, tm), lambda i: (0, i)),
            pl.BlockSpec((tm, D_KV_A), lambda i: (i, 0)),
            pl.BlockSpec((tm // NK_A, D_KV_A, NK_A), lambda i: (i, 0, 0)),
            pl.BlockSpec((D_B, tm), lambda i: (0, i)),
            pl.BlockSpec((tm, D_B), lambda i: (i, 0)),
            pl.BlockSpec((tm // NK_B, D_B, NK_B), lambda i: (i, 0, 0)),
        ],
        out_shape=[
            jax.ShapeDtypeStruct((D_A, s_len), jnp.bfloat16),
            jax.ShapeDtypeStruct((s_len, D_KV_A), jnp.bfloat16),
            jax.ShapeDtypeStruct((s_len // NK_A, D_KV_A, NK_A), jnp.bfloat16),
            jax.ShapeDtypeStruct((D_B, s_len), jnp.bfloat16),
            jax.ShapeDtypeStruct((s_len, D_B), jnp.bfloat16),
            jax.ShapeDtypeStruct((s_len // NK_B, D_B, NK_B), jnp.bfloat16),
        ],
        compiler_params=pltpu.CompilerParams(
            dimension_semantics=("arbitrary",), vmem_limit_bytes=VMEM_LIMIT),
        name="proj",
    )(x2, g_pre, wt, wkb, gq, gk, cos_t, sin_t)


def _softmax_step(s, v_t, m, l, acc_ref):
    m_new = jnp.maximum(m, jnp.max(s, axis=0, keepdims=True))
    alpha = jnp.exp2(m - m_new)
    p = jnp.exp2(s - m_new)
    l_new = alpha * l + jnp.sum(p, axis=0, keepdims=True)
    pv = jnp.dot(v_t, p.astype(jnp.bfloat16), preferred_element_type=jnp.float32)
    acc_ref[...] = acc_ref[...] * alpha + pv
    return m_new, l_new


def _attn_a_kernel(q_ref, k_ref, v_ref, o_ref, qt_ref, acc_ref, s0_ref, s1_ref):
    bq = q_ref.shape[1]
    m_cols = N_HEADS_A * bq
    grp = N_HEADS_A // N_KV_A
    zeros = jnp.zeros((HEAD_DIM, bq), jnp.bfloat16)
    for h in range(N_HEADS_A):
        qh = q_ref[h * HEAD_DIM:(h + 1) * HEAD_DIM, :]
        parts = [qh if g == h // grp else zeros for g in range(N_KV_A)]
        qt_ref[:, h * bq:(h + 1) * bq] = jnp.concatenate(parts, axis=0)
    acc_ref[...] = jnp.zeros_like(acc_ref)
    n_chunks = k_ref.shape[0] // NK_A
    s_refs = (s0_ref, s1_ref)

    def logits(c):
        start = pl.multiple_of(c * NK_A, NK_A)
        return jnp.dot(k_ref[pl.ds(start, NK_A), :], qt_ref[...],
                       preferred_element_type=jnp.float32)

    s_refs[0][...] = logits(0)

    def body(t, carry):
        m, l = carry
        c0 = t * UNROLL_A
        for u in range(UNROLL_A):
            nxt = jnp.minimum(c0 + u + 1, n_chunks - 1)
            s_refs[(u + 1) % 2][...] = logits(nxt)
            m, l = _softmax_step(s_refs[u % 2][...], v_ref[c0 + u], m, l, acc_ref)
        return m, l

    m0 = jnp.full((1, m_cols), NEG_BIG, jnp.float32)
    l0 = jnp.zeros((1, m_cols), jnp.float32)
    _, l = lax.fori_loop(0, n_chunks // UNROLL_A, body, (m0, l0))
    inv = 1.0 / l
    for h in range(N_HEADS_A):
        g = h // grp
        cols = slice(h * bq, (h + 1) * bq)
        o_ref[h * HEAD_DIM:(h + 1) * HEAD_DIM, :] = (
            acc_ref[g * HEAD_DIM:(g + 1) * HEAD_DIM, cols] * inv[:, cols])


def _attn_a_call(qa_t, ka, va_t):
    s_len = ka.shape[0]
    bq = BQ_A
    return pl.pallas_call(
        _attn_a_kernel,
        grid=(s_len // bq,),
        in_specs=[
            pl.BlockSpec((D_A, bq), lambda i: (0, i)),
            pl.BlockSpec((s_len, D_KV_A), lambda i: (0, 0)),
            pl.BlockSpec((s_len // NK_A, D_KV_A, NK_A), lambda i: (0, 0, 0)),
        ],
        out_specs=pl.BlockSpec((D_A, bq), lambda i: (0, i)),
        out_shape=jax.ShapeDtypeStruct((D_A, s_len), jnp.float32),
        scratch_shapes=[
            pltpu.VMEM((D_KV_A, N_HEADS_A * bq), jnp.bfloat16),
            pltpu.VMEM((D_KV_A, N_HEADS_A * bq), jnp.float32),
            pltpu.VMEM((NK_A, N_HEADS_A * bq), jnp.float32),
            pltpu.VMEM((NK_A, N_HEADS_A * bq), jnp.float32),
        ],
        compiler_params=pltpu.CompilerParams(
            dimension_semantics=("arbitrary",), vmem_limit_bytes=VMEM_LIMIT),
        name="attn_a",
    )(qa_t, ka, va_t)


def _attn_b_kernel(q_ref, k_ref, v_ref, bias_ref, o_ref, qt_ref, acc_ref):
    i = pl.program_id(1)
    bq = q_ref.shape[1]
    zeros = jnp.zeros((HEAD_DIM, bq), jnp.bfloat16)
    q0 = q_ref[:HEAD_DIM, :]
    q1 = q_ref[HEAD_DIM:, :]
    qt_ref[:, :bq] = jnp.concatenate([q0, zeros], axis=0)
    qt_ref[:, bq:] = jnp.concatenate([zeros, q1], axis=0)

    n_chunks = k_ref.shape[0] // NK_B
    back = B_REACH // NK_B
    span = (bq + 2 * B_REACH) // NK_B
    first = i * (bq // NK_B) - back
    interior = jnp.logical_and(first >= 0, first + span <= n_chunks)

    def write(acc, l):
        inv = 1.0 / l
        o_ref[:HEAD_DIM, :] = acc[:HEAD_DIM, :bq] * inv[:, :bq]
        o_ref[HEAD_DIM:, :] = acc[HEAD_DIM:, bq:] * inv[:, bq:]

    @pl.when(interior)
    def _whole_window():
        k_start = pl.multiple_of(first * NK_B, NK_B)
        s = jnp.dot(k_ref[pl.ds(k_start, span * NK_B), :], qt_ref[...],
                    preferred_element_type=jnp.float32) + bias_ref[...]
        p = jnp.exp2(s - jnp.max(s, axis=0, keepdims=True))
        l = jnp.sum(p, axis=0, keepdims=True)
        pb = p.astype(jnp.bfloat16)
        acc = None
        for c in range(span):
            part = jnp.dot(v_ref[first + c], pb[c * NK_B:(c + 1) * NK_B],
                           preferred_element_type=jnp.float32)
            acc = part if acc is None else acc + part
        write(acc, l)

    @pl.when(jnp.logical_not(interior))
    def _clipped_window():
        acc_ref[...] = jnp.zeros_like(acc_ref)
        c_lo = jnp.maximum(0, -first)
        c_hi = jnp.minimum(span, n_chunks - first)

        def body(c, carry):
            m, l = carry
            kc = first + c
            k_start = pl.multiple_of(kc * NK_B, NK_B)
            b_start = pl.multiple_of(c * NK_B, NK_B)
            s = jnp.dot(k_ref[pl.ds(k_start, NK_B), :], qt_ref[...],
                        preferred_element_type=jnp.float32)
            s = s + bias_ref[pl.ds(b_start, NK_B), :]
            return _softmax_step(s, v_ref[kc], m, l, acc_ref)

        m0 = jnp.full((1, 2 * bq), NEG_BIG, jnp.float32)
        l0 = jnp.zeros((1, 2 * bq), jnp.float32)
        _, l = lax.fori_loop(c_lo, c_hi, body, (m0, l0))
        write(acc_ref[...], l)


def _attn_b_call(qb_t, kb, vb_t, bias_t):
    s_len = kb.shape[0]
    bq = BQ_B
    pairs = N_HEADS_B // 2
    span_rows = bq + 2 * B_REACH
    return pl.pallas_call(
        _attn_b_kernel,
        grid=(pairs, s_len // bq),
        in_specs=[
            pl.BlockSpec((2 * HEAD_DIM, bq), lambda j, i: (j, i)),
            pl.BlockSpec((s_len, 2 * HEAD_DIM), lambda j, i: (0, j)),
            pl.BlockSpec((s_len // NK_B, 2 * HEAD_DIM, NK_B), lambda j, i: (0, j, 0)),
            pl.BlockSpec((None, span_rows, 2 * bq), lambda j, i: (j, 0, 0)),
        ],
        out_specs=pl.BlockSpec((2 * HEAD_DIM, bq), lambda j, i: (j, i)),
        out_shape=jax.ShapeDtypeStruct((D_B, s_len), jnp.float32),
        scratch_shapes=[
            pltpu.VMEM((2 * HEAD_DIM, 2 * bq), jnp.bfloat16),
            pltpu.VMEM((2 * HEAD_DIM, 2 * bq), jnp.float32),
        ],
        compiler_params=pltpu.CompilerParams(
            dimension_semantics=("arbitrary", "arbitrary"), vmem_limit_bytes=VMEM_LIMIT),
        name="attn_b",
    )(qb_t, kb, vb_t, bias_t)


def _t5_bucket_index(rel):
    nb = N_BUCKETS // 2
    max_exact = nb // 2
    side = jnp.where(rel > 0, nb, 0)
    n = jnp.abs(rel)
    large = max_exact + (jnp.log(jnp.maximum(n, max_exact).astype(jnp.float32) / max_exact)
                         / math.log(MAX_DISTANCE / max_exact) * (nb - max_exact)).astype(jnp.int32)
    large = jnp.minimum(large, nb - 1)
    return side + jnp.where(n < max_exact, n, large)


def _dilated_bias_tiles(rel_bias):
    bq = BQ_B
    reach_all = B_REACH + bq - 1
    delta = jnp.arange(-reach_all, reach_all + 1)
    count = jnp.zeros(delta.shape, jnp.float32)
    for window, dilation in DILATED_PATTERNS:
        inside = (delta % dilation == 0) & (jnp.abs(delta) <= window // 2)
        count = count + inside.astype(jnp.float32)
    table = rel_bias[_t5_bucket_index(delta)].astype(jnp.float32)
    table = jnp.where((count > 0)[:, None],
                      (table + jnp.log(jnp.maximum(count, 1.0))[:, None]) * LOG2E, NEG_BIG)
    span_rows = bq + 2 * B_REACH
    n = delta.shape[0] + 1
    f = jnp.pad(table.T, ((0, 0), (0, 1)))
    skew = jnp.tile(f, (1, bq))[:, :bq * (n - 1)].reshape(N_HEADS_B, bq, n - 1)
    tiles = skew[:, :, bq - 1:bq - 1 + span_rows]
    tiles = tiles.reshape(N_HEADS_B // 2, 2, bq, span_rows)
    return tiles.transpose(0, 3, 1, 2).reshape(N_HEADS_B // 2, span_rows, 2 * bq)


def _out_kernel(ya_ref, yb_ref, x_ref, ga_ref, gb_ref, w_ref, gpost_ref, h_ref):
    def norm_t(y_t, g_col):
        ms = jnp.mean(y_t * y_t, axis=0, keepdims=True)
        return (y_t * lax.rsqrt(ms + EPS) * g_col).astype(jnp.bfloat16)

    y_t = jnp.concatenate([norm_t(ya_ref[...], ga_ref[...]),
                           norm_t(yb_ref[...], gb_ref[...])], axis=0)
    y = lax.dot_general(y_t, w_ref[...], (((0,), (0,)), ((), ())),
                        preferred_element_type=jnp.float32)
    h_ref[...] = x_ref[...] + _rms_rows(y, gpost_ref[...])


def _out_call(ya_t, yb_t, x2, ga, gb, w_out, g_post):
    s_len = x2.shape[0]
    tm = TM_OUT
    const = lambda i: (0, 0)
    return pl.pallas_call(
        _out_kernel,
        grid=(s_len // tm,),
        in_specs=[
            pl.BlockSpec((D_A, tm), lambda i: (0, i)),
            pl.BlockSpec((D_B, tm), lambda i: (0, i)),
            pl.BlockSpec((tm, D_MODEL), lambda i: (i, 0)),
            pl.BlockSpec((D_A, 1), const),
            pl.BlockSpec((D_B, 1), const),
            pl.BlockSpec((D_A + D_B, D_MODEL), const),
            pl.BlockSpec((1, D_MODEL), const),
        ],
        out_specs=pl.BlockSpec((tm, D_MODEL), lambda i: (i, 0)),
        out_shape=jax.ShapeDtypeStruct((s_len, D_MODEL), jnp.float32),
        compiler_params=pltpu.CompilerParams(
            dimension_semantics=("arbitrary",), vmem_limit_bytes=VMEM_LIMIT),
        name="out_proj",
    )(ya_t, yb_t, x2, ga, gb, w_out, g_post)


def _ffn_kernel(h_ref, p_ref, g1_ref, w1_ref, w2_ref, g2_ref, g3_ref, wg_ref, wp_ref, o_ref):
    h = h_ref[...]
    xn = _rms_rows(h, g1_ref[...]).astype(jnp.bfloat16)
    f = None
    for c in range(D_FF // FF_CHUNK):
        cols = slice(c * FF_CHUNK, (c + 1) * FF_CHUNK)
        u = jnp.dot(xn, w1_ref[:, cols], preferred_element_type=jnp.float32)
        u = jnp.square(jnp.maximum(u, 0.0)).astype(jnp.bfloat16)
        part = jnp.dot(u, w2_ref[cols, :], preferred_element_type=jnp.float32)
        f = part if f is None else f + part
    h = h + _rms_rows(f, g2_ref[...])
    gate_in = _rms_rows(h, g3_ref[...]).astype(jnp.bfloat16)
    gate = jax.nn.sigmoid(jnp.dot(gate_in, wg_ref[...], preferred_element_type=jnp.float32))
    emb = jnp.dot(p_ref[...].astype(jnp.bfloat16), wp_ref[...], preferred_element_type=jnp.float32)
    o_ref[...] = h + gate * emb


def _ffn_call(h1, p2, g1, w1, w2, g2, g3, wg, wp):
    s_len = h1.shape[0]
    tm = TM_FFN
    const = lambda i: (0, 0)
    resident = functools.partial(pl.BlockSpec, index_map=const, pipeline_mode=pl.Buffered(1))
    return pl.pallas_call(
        _ffn_kernel,
        grid=(s_len // tm,),
        in_specs=[
            pl.BlockSpec((tm, D_MODEL), lambda i: (i, 0)),
            pl.BlockSpec((tm, D_PLE), lambda i: (i, 0)),
            pl.BlockSpec((1, D_MODEL), const),
            resident((D_MODEL, D_FF)),
            resident((D_FF, D_MODEL)),
            pl.BlockSpec((1, D_MODEL), const),
            pl.BlockSpec((1, D_MODEL), const),
            resident((D_MODEL, D_MODEL)),
            resident((D_PLE, D_MODEL)),
        ],
        out_specs=pl.BlockSpec((tm, D_MODEL), lambda i: (i, 0)),
        out_shape=jax.ShapeDtypeStruct((s_len, D_MODEL), jnp.float32),
        compiler_params=pltpu.CompilerParams(
            dimension_semantics=("arbitrary",), vmem_limit_bytes=VMEM_LIMIT),
        name="ffn_ple",
    )(h1, p2, g1, w1, w2, g2, g3, wg, wp)


def _rope_tables_t(n_tokens):
    tok = jnp.arange(n_tokens)
    row = (tok // GRID_W).astype(jnp.float32)
    col = (tok % GRID_W).astype(jnp.float32)
    n_axis = ROPE_HALF // 2
    inv_freq = ROPE_THETA ** (-jnp.arange(n_axis, dtype=jnp.float32) / n_axis)
    ang = jnp.concatenate([inv_freq[:, None] * row[None, :], inv_freq[:, None] * col[None, :]], axis=0)
    return jnp.cos(ang), jnp.sin(ang)


def _layer(h, p_i, w_in, g_attn_pre, g_q, g_k, g_out_a, g_out_b, w_out, g_attn_post, bias_t,
           g_mlp_pre, w_ff1, w_ff2, g_mlp_post, g_ple, w_ple_gate, w_ple_proj, cos_t, sin_t):
    bf = jnp.bfloat16
    o_ka, o_va = D_A, D_A + D_KV_A
    o_qb = D_A + 2 * D_KV_A
    o_kb, o_vb = o_qb + D_B, o_qb + 2 * D_B
    wt = jnp.concatenate([w_in[:, :o_qb + D_B], w_in[:, o_vb:]], axis=1).T.astype(bf)
    wkb = w_in[:, o_kb:o_vb].astype(bf)
    row = lambda g: g.reshape(1, -1)
    col = lambda g: g.reshape(-1, 1)

    qa_t, ka, va_t, qb_t, kb, vb_t = _proj_call(
        h, row(g_attn_pre), wt, wkb, col(g_q), col(g_k), cos_t, sin_t)
    ya_t = _attn_a_call(qa_t, ka, va_t)
    yb_t = _attn_b_call(qb_t, kb, vb_t, bias_t)
    h1 = _out_call(ya_t, yb_t, h, col(g_out_a), col(g_out_b), w_out.astype(bf), row(g_attn_post))
    return _ffn_call(h1, p_i, row(g_mlp_pre), w_ff1.astype(bf), w_ff2.astype(bf), row(g_mlp_post),
                     row(g_ple), w_ple_gate.astype(bf), w_ple_proj.astype(bf))


def kernel(x, p, w_in, g_attn_pre, g_q, g_k, g_out_a, g_out_b, w_out, g_attn_post, rel_bias,
           g_mlp_pre, w_ff1, w_ff2, g_mlp_post, g_ple, w_ple_gate, w_ple_proj):
    b, s_len, _ = x.shape
    cos_t, sin_t = _rope_tables_t(s_len)
    bias_t = _dilated_bias_tiles(rel_bias)
    outs = []
    for bi in range(b):
        h = x[bi]
        for i in range(w_in.shape[0]):
            h = _layer(h, p[i, bi], w_in[i], g_attn_pre[i], g_q[i], g_k[i], g_out_a[i], g_out_b[i],
                       w_out[i], g_attn_post[i], bias_t, g_mlp_pre[i], w_ff1[i], w_ff2[i],
                       g_mlp_post[i], g_ple[i], w_ple_gate[i], w_ple_proj[i], cos_t, sin_t)
        outs.append(h)
    return jnp.stack(outs, axis=0)
```

```python
import functools
import math

import jax
import jax.numpy as jnp
from jax import lax
from jax.experimental import pallas as pl
from jax.experimental.pallas import tpu as pltpu

D_MODEL = 1024
HEAD_DIM = 64
N_HEADS_A = 8
N_KV_A = 2
N_HEADS_B = 8
D_A = N_HEADS_A * HEAD_DIM
D_KV_A = N_KV_A * HEAD_DIM
D_B = N_HEADS_B * HEAD_DIM
D_FF = 4 * D_MODEL
D_PLE = 256
GRID_W = 64
ROPE_THETA = 10000.0
ROPE_HALF = HEAD_DIM // 2
DILATED_PATTERNS = ((128, 1), (512, 4), (2048, 16))
N_BUCKETS = 32
MAX_DISTANCE = 1024
EPS = 1e-6
NEG_BIG = -1e30

LOG2E = math.log2(math.e)
Q_SCALE = HEAD_DIM ** -0.5 * LOG2E

MXU_TILE = 256
ONES_ROWS = 16
V_ROWS = 2 * HEAD_DIM + ONES_ROWS
VMEM_LIMIT = 56 * 1024 * 1024

TM_PROJ = 512
BQ_A = 128
NK_A = 512
UNROLL_A = 4
BQ_B = 256
NK_B = 256
B_REACH = 1024
TM_OUT = 512
TM_FFN = 512
FF_CHUNK = 1024

_T_QA, _T_KA, _T_VA, _T_QB, _T_VB = 0, 512, 640, 768, 1280
_T_ROWS = 1792


def _rms_rows(x, g_row):
    ms = jnp.mean(x * x, axis=-1, keepdims=True)
    return x * lax.rsqrt(ms + EPS) * g_row


def _proj_kernel(x_ref, g_ref, wt_ref, wkb_ref, gq_ref, gk_ref, cos_ref, sin_ref,
                 qa_ref, ka_ref, va_ref, qb_ref, kb_ref, vb_ref):
    xn = _rms_rows(x_ref[...], g_ref[...]).astype(jnp.bfloat16)
    pt = lax.dot_general(wt_ref[...], xn, (((1,), (1,)), ((), ())),
                         preferred_element_type=jnp.float32)
    kb = jnp.dot(xn, wkb_ref[...], preferred_element_type=jnp.float32)
    kb_ref[...] = kb.astype(jnp.bfloat16)

    cos = cos_ref[...]
    sin = sin_ref[...]

    def norm_rope(blk, g_col):
        ms = jnp.mean(blk * blk, axis=0, keepdims=True)
        y = blk * lax.rsqrt(ms + EPS) * g_col
        y1, y2 = y[:ROPE_HALF], y[ROPE_HALF:]
        return jnp.concatenate([y1 * cos - y2 * sin, y2 * cos + y1 * sin], axis=0)

    gq = gq_ref[...]
    gk = gk_ref[...]
    for h in range(N_HEADS_A):
        r0 = _T_QA + h * HEAD_DIM
        q = norm_rope(pt[r0:r0 + HEAD_DIM], gq) * Q_SCALE
        qa_ref[h * HEAD_DIM:(h + 1) * HEAD_DIM, :] = q.astype(jnp.bfloat16)
    k_heads = [norm_rope(pt[_T_KA + g * HEAD_DIM:_T_KA + (g + 1) * HEAD_DIM], gk)
               for g in range(N_KV_A)]
    ka_ref[...] = jnp.concatenate(k_heads, axis=0).T.astype(jnp.bfloat16)

    tm = x_ref.shape[0]
    va = pt[_T_VA:_T_VA + D_KV_A].astype(jnp.bfloat16)
    for c in range(tm // NK_A):
        va_ref[c, :D_KV_A, :] = va[:, c * NK_A:(c + 1) * NK_A]
        va_ref[c, D_KV_A:, :] = jnp.ones((ONES_ROWS, NK_A), jnp.bfloat16)
    qb_ref[...] = (pt[_T_QB:_T_QB + D_B] * Q_SCALE).astype(jnp.bfloat16)
    vb = pt[_T_VB:_T_VB + D_B].astype(jnp.bfloat16)
    pair = 2 * HEAD_DIM
    for c in range(tm // NK_B):
        for j in range(N_HEADS_B // 2):
            r0 = j * V_ROWS
            vb_ref[c, r0:r0 + pair, :] = vb[j * pair:(j + 1) * pair, c * NK_B:(c + 1) * NK_B]
            vb_ref[c, r0 + pair:r0 + V_ROWS, :] = jnp.ones((ONES_ROWS, NK_B), jnp.bfloat16)


def _proj_call(x2, g_pre, wt, wkb, gq, gk, cos_t, sin_t):
    s_len = x2.shape[0]
    tm = TM_PROJ
    const = lambda i: (0, 0)
    return pl.pallas_call(
        _proj_kernel,
        grid=(s_len // tm,),
        in_specs=[
            pl.BlockSpec((tm, D_MODEL), lambda i: (i, 0)),
            pl.BlockSpec((1, D_MODEL), const),
            pl.BlockSpec((_T_ROWS, D_MODEL), const),
            pl.BlockSpec((D_MODEL, D_B), const),
            pl.BlockSpec((HEAD_DIM, 1), const),
            pl.BlockSpec((HEAD_DIM, 1), const),
            pl.BlockSpec((ROPE_HALF, tm), lambda i: (0, i)),
            pl.BlockSpec((ROPE_HALF, tm), lambda i: (0, i)),
        ],
        out_specs=[
            pl.BlockSpec((D_A, tm), lambda i: (0, i)),
            pl.BlockSpec((tm, D_KV_A), lambda i: (i, 0)),
            pl.BlockSpec((tm // NK_A, V_ROWS, NK_A), lambda i: (i, 0, 0)),
            pl.BlockSpec((D_B, tm), lambda i: (0, i)),
            pl.BlockSpec((tm, D_B), lambda i: (i, 0)),
            pl.BlockSpec((tm // NK_B, V_ROWS * (N_HEADS_B // 2), NK_B), lambda i: (i, 0, 0)),
        ],
        out_shape=[
            jax.ShapeDtypeStruct((D_A, s_len), jnp.bfloat16),
            jax.ShapeDtypeStruct((s_len, D_KV_A), jnp.bfloat16),
            jax.ShapeDtypeStruct((s_len // NK_A, V_ROWS, NK_A), jnp.bfloat16),
            jax.ShapeDtypeStruct((D_B, s_len), jnp.bfloat16),
            jax.ShapeDtypeStruct((s_len, D_B), jnp.bfloat16),
            jax.ShapeDtypeStruct((s_len // NK_B, V_ROWS * (N_HEADS_B // 2), NK_B),
                                 jnp.bfloat16),
        ],
        compiler_params=pltpu.CompilerParams(
            dimension_semantics=("arbitrary",), vmem_limit_bytes=VMEM_LIMIT),
        name="proj",
    )(x2, g_pre, wt, wkb, gq, gk, cos_t, sin_t)


def _probs(s, m):
    return jnp.exp2(s - m).astype(jnp.bfloat16)


def _softmax_step(s, v_t, m, acc_ref):
    m_new = jnp.maximum(m, jnp.max(s, axis=0, keepdims=True))
    pv = jnp.dot(v_t, _probs(s, m_new), preferred_element_type=jnp.float32)
    acc_ref[...] = acc_ref[...] * jnp.exp2(m - m_new) + pv
    return m_new


def _logits_tiles(k_rows, qt_ref, bias_ref, s_ref, n_ct):
    maxima = []
    for ct in range(n_ct):
        cols = slice(ct * MXU_TILE, (ct + 1) * MXU_TILE)
        s = jnp.dot(k_rows, qt_ref[:, cols], preferred_element_type=jnp.float32)
        if bias_ref is not None:
            s = s + bias_ref[:, cols]
        s_ref[:, cols] = s
        maxima.append(jnp.max(s_ref[:, cols], axis=0, keepdims=True))
    return maxima


def _softmax_tiles(s_ref, mx, v_tiles, m, acc_ref):
    m_out = []
    for ct in range(len(mx)):
        cols = slice(ct * MXU_TILE, (ct + 1) * MXU_TILE)
        pv = None
        m_new = mx[ct] if m is None else jnp.maximum(m[ct], mx[ct])
        for r, v_t in enumerate(v_tiles):
            p = _probs(s_ref[r * MXU_TILE:(r + 1) * MXU_TILE, cols], m_new)
            part = jnp.dot(v_t, p, preferred_element_type=jnp.float32)
            pv = part if pv is None else pv + part
        if m is None:
            acc_ref[:, cols] = pv
        else:
            acc_ref[:, cols] = acc_ref[:, cols] * jnp.exp2(m[ct] - m_new) + pv
        m_out.append(m_new)
    return m_out


def _attn_a_kernel(q_ref, k_ref, v_ref, o_ref, qt_ref, acc_ref, *s_refs):
    bq = q_ref.shape[1]
    m_cols = N_HEADS_A * bq
    grp = N_HEADS_A // N_KV_A
    zeros = jnp.zeros((HEAD_DIM, bq), jnp.bfloat16)
    for h in range(N_HEADS_A):
        qh = q_ref[h * HEAD_DIM:(h + 1) * HEAD_DIM, :]
        parts = [qh if g == h // grp else zeros for g in range(N_KV_A)]
        qt_ref[:, h * bq:(h + 1) * bq] = jnp.concatenate(parts, axis=0)
    acc_ref[...] = jnp.zeros_like(acc_ref)
    n_chunks = k_ref.shape[0] // NK_A

    def logits(c):
        start = pl.multiple_of(c * NK_A, NK_A)
        return jnp.dot(k_ref[pl.ds(start, NK_A), :], qt_ref[...],
                       preferred_element_type=jnp.float32)

    s_refs[0][...] = logits(0)

    def body(t, m):
        c0 = t * UNROLL_A
        for u in range(UNROLL_A):
            nxt = jnp.minimum(c0 + u + 1, n_chunks - 1)
            s_refs[(u + 1) % 2][...] = logits(nxt)
            m = _softmax_step(s_refs[u % 2][...], v_ref[c0 + u], m, acc_ref)
        return m

    lax.fori_loop(0, n_chunks // UNROLL_A, body, jnp.full((1, m_cols), NEG_BIG, jnp.float32))
    inv = 1.0 / acc_ref[D_KV_A:D_KV_A + 1, :]
    for h in range(N_HEADS_A):
        g = h // grp
        cols = slice(h * bq, (h + 1) * bq)
        o_ref[h * HEAD_DIM:(h + 1) * HEAD_DIM, :] = (
            acc_ref[g * HEAD_DIM:(g + 1) * HEAD_DIM, cols] * inv[:, cols])


def _attn_a_call(qa_t, ka, va_t):
    s_len = ka.shape[0]
    bq = BQ_A
    return pl.pallas_call(
        _attn_a_kernel,
        grid=(s_len // bq,),
        in_specs=[
            pl.BlockSpec((D_A, bq), lambda i: (0, i)),
            pl.BlockSpec((s_len, D_KV_A), lambda i: (0, 0)),
            pl.BlockSpec((s_len // NK_A, V_ROWS, NK_A), lambda i: (0, 0, 0)),
        ],
        out_specs=pl.BlockSpec((D_A, bq), lambda i: (0, i)),
        out_shape=jax.ShapeDtypeStruct((D_A, s_len), jnp.float32),
        scratch_shapes=[
            pltpu.VMEM((D_KV_A, N_HEADS_A * bq), jnp.bfloat16),
            pltpu.VMEM((V_ROWS, N_HEADS_A * bq), jnp.float32),
        ] + [pltpu.VMEM((NK_A, N_HEADS_A * bq), jnp.float32)] * 2,
        compiler_params=pltpu.CompilerParams(
            dimension_semantics=("arbitrary",), vmem_limit_bytes=VMEM_LIMIT),
        name="attn_a",
    )(qa_t, ka, va_t)


def _attn_b_kernel(q_ref, k_ref, v_ref, frev_ref, o_ref, qt_ref, acc_ref, bias_ref, s_ref):
    i = pl.program_id(1)
    bq = q_ref.shape[1]
    zeros = jnp.zeros((HEAD_DIM, bq), jnp.bfloat16)
    q0 = q_ref[:HEAD_DIM, :]
    q1 = q_ref[HEAD_DIM:, :]
    qt_ref[:, :bq] = jnp.concatenate([q0, zeros], axis=0)
    qt_ref[:, bq:] = jnp.concatenate([zeros, q1], axis=0)

    n_chunks = k_ref.shape[0] // NK_B
    back = B_REACH // NK_B
    span = (bq + 2 * B_REACH) // NK_B
    first = i * (bq // NK_B) - back
    interior = jnp.logical_and(first >= 0, first + span <= n_chunks)

    @pl.when(i == 0)
    def _build_bias():
        for hh in range(2):
            for c in range(span):
                seg = (span - 1 - c) * NK_B
                row = frev_ref[hh:hh + 1, seg:seg + 2 * bq]
                rolled = pltpu.roll(jnp.broadcast_to(row, (NK_B, 2 * bq)), 0, 1,
                                    stride=1, stride_axis=0)
                bias_ref[c * NK_B:(c + 1) * NK_B, hh * bq:(hh + 1) * bq] = rolled[:, bq:]

    def write():
        inv = 1.0 / acc_ref[2 * HEAD_DIM:2 * HEAD_DIM + 1, :]
        o_ref[:HEAD_DIM, :] = acc_ref[:HEAD_DIM, :bq] * inv[:, :bq]
        o_ref[HEAD_DIM:, :] = acc_ref[HEAD_DIM:2 * HEAD_DIM, bq:] * inv[:, bq:]

    @pl.when(interior)
    def _whole_window():
        n_ct = 2 * bq // MXU_TILE
        k_start = pl.multiple_of(first * NK_B, NK_B)
        mx = _logits_tiles(k_ref[pl.ds(k_start, span * NK_B), :], qt_ref, bias_ref, s_ref, n_ct)
        _softmax_tiles(s_ref, mx, [v_ref[first + c] for c in range(span)], None, acc_ref)
        write()

    @pl.when(jnp.logical_not(interior))
    def _clipped_window():
        acc_ref[...] = jnp.zeros_like(acc_ref)
        c_lo = jnp.maximum(0, -first)
        c_hi = jnp.minimum(span, n_chunks - first)

        def body(c, m):
            kc = first + c
            k_start = pl.multiple_of(kc * NK_B, NK_B)
            b_start = pl.multiple_of(c * NK_B, NK_B)
            s = jnp.dot(k_ref[pl.ds(k_start, NK_B), :], qt_ref[...],
                        preferred_element_type=jnp.float32)
            s = s + bias_ref[pl.ds(b_start, NK_B), :]
            return _softmax_step(s, v_ref[kc], m, acc_ref)

        lax.fori_loop(c_lo, c_hi, body, jnp.full((1, 2 * bq), NEG_BIG, jnp.float32))
        write()


def _attn_b_call(qb_t, kb, vb_t, frev):
    s_len = kb.shape[0]
    bq = BQ_B
    pairs = N_HEADS_B // 2
    span_rows = bq + 2 * B_REACH
    assert bq == NK_B == MXU_TILE and frev.shape[-1] == span_rows + bq
    return pl.pallas_call(
        _attn_b_kernel,
        grid=(pairs, s_len // bq),
        in_specs=[
            pl.BlockSpec((2 * HEAD_DIM, bq), lambda j, i: (j, i)),
            pl.BlockSpec((s_len, 2 * HEAD_DIM), lambda j, i: (0, j)),
            pl.BlockSpec((s_len // NK_B, V_ROWS, NK_B), lambda j, i: (0, j, 0)),
            pl.BlockSpec((None, 2, span_rows + bq), lambda j, i: (j, 0, 0)),
        ],
        out_specs=pl.BlockSpec((2 * HEAD_DIM, bq), lambda j, i: (j, i)),
        out_shape=jax.ShapeDtypeStruct((D_B, s_len), jnp.float32),
        scratch_shapes=[
            pltpu.VMEM((2 * HEAD_DIM, 2 * bq), jnp.bfloat16),
            pltpu.VMEM((V_ROWS, 2 * bq), jnp.float32),
            pltpu.VMEM((span_rows, 2 * bq), jnp.float32),
            pltpu.VMEM((span_rows, 2 * bq), jnp.float32),
        ],
        compiler_params=pltpu.CompilerParams(
            dimension_semantics=("arbitrary", "arbitrary"), vmem_limit_bytes=VMEM_LIMIT),
        name="attn_b",
    )(qb_t, kb, vb_t, frev)


def _t5_bucket_index(rel):
    nb = N_BUCKETS // 2
    max_exact = nb // 2
    side = jnp.where(rel > 0, nb, 0)
    n = jnp.abs(rel)
    large = max_exact + (jnp.log(jnp.maximum(n, max_exact).astype(jnp.float32) / max_exact)
                         / math.log(MAX_DISTANCE / max_exact) * (nb - max_exact)).astype(jnp.int32)
    large = jnp.minimum(large, nb - 1)
    return side + jnp.where(n < max_exact, n, large)


def _dilated_bias_table(rel_bias):
    bq = BQ_B
    reach_all = B_REACH + bq - 1
    delta = jnp.arange(-reach_all, reach_all + 1)
    count = jnp.zeros(delta.shape, jnp.float32)
    for window, dilation in DILATED_PATTERNS:
        inside = (delta % dilation == 0) & (jnp.abs(delta) <= window // 2)
        count = count + inside.astype(jnp.float32)
    table = rel_bias[_t5_bucket_index(delta)].astype(jnp.float32)
    table = jnp.where((count > 0)[:, None],
                      (table + jnp.log(jnp.maximum(count, 1.0))[:, None]) * LOG2E, NEG_BIG)
    frev = jnp.pad(table[::-1].T, ((0, 0), (1, 0)))
    return frev.reshape(N_HEADS_B // 2, 2, frev.shape[1])


def _out_kernel(ya_ref, yb_ref, x_ref, ga_ref, gb_ref, w_ref, gpost_ref, h_ref):
    def norm_t(y_t, g_col):
        ms = jnp.mean(y_t * y_t, axis=0, keepdims=True)
        return (y_t * lax.rsqrt(ms + EPS) * g_col).astype(jnp.bfloat16)

    y_t = jnp.concatenate([norm_t(ya_ref[...], ga_ref[...]),
                           norm_t(yb_ref[...], gb_ref[...])], axis=0)
    y = lax.dot_general(y_t, w_ref[...], (((0,), (0,)), ((), ())),
                        preferred_element_type=jnp.float32)
    h_ref[...] = x_ref[...] + _rms_rows(y, gpost_ref[...])


def _out_call(ya_t, yb_t, x2, ga, gb, w_out, g_post):
    s_len = x2.shape[0]
    tm = TM_OUT
    const = lambda i: (0, 0)
    return pl.pallas_call(
        _out_kernel,
        grid=(s_len // tm,),
        in_specs=[
            pl.BlockSpec((D_A, tm), lambda i: (0, i)),
            pl.BlockSpec((D_B, tm), lambda i: (0, i)),
            pl.BlockSpec((tm, D_MODEL), lambda i: (i, 0)),
            pl.BlockSpec((D_A, 1), const),
            pl.BlockSpec((D_B, 1), const),
            pl.BlockSpec((D_A + D_B, D_MODEL), const),
            pl.BlockSpec((1, D_MODEL), const),
        ],
        out_specs=pl.BlockSpec((tm, D_MODEL), lambda i: (i, 0)),
        out_shape=jax.ShapeDtypeStruct((s_len, D_MODEL), jnp.float32),
        compiler_params=pltpu.CompilerParams(
            dimension_semantics=("arbitrary",), vmem_limit_bytes=VMEM_LIMIT),
        name="out_proj",
    )(ya_t, yb_t, x2, ga, gb, w_out, g_post)


def _ffn_kernel(h_ref, p_ref, g1_ref, w1_ref, w2_ref, g2_ref, g3_ref, wg_ref, wp_ref, o_ref):
    h = h_ref[...]
    xn = _rms_rows(h, g1_ref[...]).astype(jnp.bfloat16)
    f = None
    for c in range(D_FF // FF_CHUNK):
        cols = slice(c * FF_CHUNK, (c + 1) * FF_CHUNK)
        u = jnp.dot(xn, w1_ref[:, cols], preferred_element_type=jnp.float32)
        u = jnp.square(jnp.maximum(u, 0.0)).astype(jnp.bfloat16)
        part = jnp.dot(u, w2_ref[cols, :], preferred_element_type=jnp.float32)
        f = part if f is None else f + part
    h = h + _rms_rows(f, g2_ref[...])
    gate_in = _rms_rows(h, g3_ref[...]).astype(jnp.bfloat16)
    gate = jax.nn.sigmoid(jnp.dot(gate_in, wg_ref[...], preferred_element_type=jnp.float32))
    emb = jnp.dot(p_ref[...].astype(jnp.bfloat16), wp_ref[...], preferred_element_type=jnp.float32)
    o_ref[...] = h + gate * emb


def _ffn_call(h1, p2, g1, w1, w2, g2, g3, wg, wp):
    s_len = h1.shape[0]
    tm = TM_FFN
    const = lambda i: (0, 0)
    resident = functools.partial(pl.BlockSpec, index_map=const, pipeline_mode=pl.Buffered(1))
    return pl.pallas_call(
        _ffn_kernel,
        grid=(s_len // tm,),
        in_specs=[
            pl.BlockSpec((tm, D_MODEL), lambda i: (i, 0)),
            pl.BlockSpec((tm, D_PLE), lambda i: (i, 0)),
            pl.BlockSpec((1, D_MODEL), const),
            resident((D_MODEL, D_FF)),
            resident((D_FF, D_MODEL)),
            pl.BlockSpec((1, D_MODEL), const),
            pl.BlockSpec((1, D_MODEL), const),
            resident((D_MODEL, D_MODEL)),
            resident((D_PLE, D_MODEL)),
        ],
        out_specs=pl.BlockSpec((tm, D_MODEL), lambda i: (i, 0)),
        out_shape=jax.ShapeDtypeStruct((s_len, D_MODEL), jnp.float32),
        compiler_params=pltpu.CompilerParams(
            dimension_semantics=("arbitrary",), vmem_limit_bytes=VMEM_LIMIT),
        name="ffn_ple",
    )(h1, p2, g1, w1, w2, g2, g3, wg, wp)


def _rope_tables_t(n_tokens):
    tok = jnp.arange(n_tokens)
    row = (tok // GRID_W).astype(jnp.float32)
    col = (tok % GRID_W).astype(jnp.float32)
    n_axis = ROPE_HALF // 2
    inv_freq = ROPE_THETA ** (-jnp.arange(n_axis, dtype=jnp.float32) / n_axis)
    ang = jnp.concatenate([inv_freq[:, None] * row[None, :], inv_freq[:, None] * col[None, :]], axis=0)
    return jnp.cos(ang), jnp.sin(ang)


def _layer(h, p_i, w_in, g_attn_pre, g_q, g_k, g_out_a, g_out_b, w_out, g_attn_post, bias_t,
           g_mlp_pre, w_ff1, w_ff2, g_mlp_post, g_ple, w_ple_gate, w_ple_proj, cos_t, sin_t):
    bf = jnp.bfloat16
    o_ka, o_va = D_A, D_A + D_KV_A
    o_qb = D_A + 2 * D_KV_A
    o_kb, o_vb = o_qb + D_B, o_qb + 2 * D_B
    wt = jnp.concatenate([w_in[:, :o_qb + D_B], w_in[:, o_vb:]], axis=1).T.astype(bf)
    wkb = w_in[:, o_kb:o_vb].astype(bf)
    row = lambda g: g.reshape(1, -1)
    col = lambda g: g.reshape(-1, 1)

    qa_t, ka, va_t, qb_t, kb, vb_t = _proj_call(
        h, row(g_attn_pre), wt, wkb, col(g_q), col(g_k), cos_t, sin_t)
    ya_t = _attn_a_call(qa_t, ka, va_t)
    yb_t = _attn_b_call(qb_t, kb, vb_t, bias_t)
    h1 = _out_call(ya_t, yb_t, h, col(g_out_a), col(g_out_b), w_out.astype(bf), row(g_attn_post))
    return _ffn_call(h1, p_i, row(g_mlp_pre), w_ff1.astype(bf), w_ff2.astype(bf), row(g_mlp_post),
                     row(g_ple), w_ple_gate.astype(bf), w_ple_proj.astype(bf))


def kernel(x, p, w_in, g_attn_pre, g_q, g_k, g_out_a, g_out_b, w_out, g_attn_post, rel_bias,
           g_mlp_pre, w_ff1, w_ff2, g_mlp_post, g_ple, w_ple_gate, w_ple_proj):
    b, s_len, _ = x.shape
    cos_t, sin_t = _rope_tables_t(s_len)
    bias_t = _dilated_bias_table(rel_bias)
    outs = []
    for bi in range(b):
        h = x[bi]
        for i in range(w_in.shape[0]):
            h = _layer(h, p[i, bi], w_in[i], g_attn_pre[i], g_q[i], g_k[i], g_out_a[i], g_out_b[i],
                       w_out[i], g_attn_post[i], bias_t, g_mlp_pre[i], w_ff1[i], w_ff2[i],
                       g_mlp_post[i], g_ple[i], w_ple_gate[i], w_ple_proj[i], cos_t, sin_t)
        outs.append(h)
    return jnp.stack(outs, axis=0)
```

```python
import functools
import math

import jax
import jax.numpy as jnp
from jax import lax
from jax.experimental import pallas as pl
from jax.experimental.pallas import tpu as pltpu

D_MODEL = 1024
HEAD_DIM = 64
N_HEADS_A = 8
N_KV_A = 2
N_HEADS_B = 8
D_A = N_HEADS_A * HEAD_DIM
D_KV_A = N_KV_A * HEAD_DIM
D_B = N_HEADS_B * HEAD_DIM
D_FF = 4 * D_MODEL
D_PLE = 256
GRID_W = 64
ROPE_THETA = 10000.0
ROPE_HALF = HEAD_DIM // 2
DILATED_PATTERNS = ((128, 1), (512, 4), (2048, 16))
N_BUCKETS = 32
MAX_DISTANCE = 1024
EPS = 1e-6
NEG_BIG = -1e30

LOG2E = math.log2(math.e)
Q_SCALE = HEAD_DIM ** -0.5 * LOG2E

MXU_TILE = 256
ONES_ROWS = 16
V_ROWS = HEAD_DIM + ONES_ROWS
VMEM_LIMIT = 56 * 1024 * 1024

TM_PROJ = 512
BQ_A = 128
NK_A = 256
UNROLL_A = 8
BQ_B = 256
NK_B = 256
B_REACH = 1024
TM_OUT = 512
TM_FFN = 512
FF_CHUNK = 1024

_T_QA, _T_KA, _T_VA, _T_QB, _T_VB = 0, 512, 640, 768, 1280
_T_ROWS = 1792


def _rms_rows(x, g_row):
    ms = jnp.mean(x * x, axis=-1, keepdims=True)
    return x * lax.rsqrt(ms + EPS) * g_row


def _proj_kernel(x_ref, g_ref, wt_ref, wkb_ref, gq_ref, gk_ref, cos_ref, sin_ref,
                 qa_ref, ka_ref, va_ref, qb_ref, kb_ref, vb_ref):
    xn = _rms_rows(x_ref[...], g_ref[...]).astype(jnp.bfloat16)
    pt = lax.dot_general(wt_ref[...], xn, (((1,), (1,)), ((), ())),
                         preferred_element_type=jnp.float32)
    kb = jnp.dot(xn, wkb_ref[...], preferred_element_type=jnp.float32)
    kb_ref[...] = kb.astype(jnp.bfloat16)

    cos = cos_ref[...]
    sin = sin_ref[...]

    def norm_rope(blk, g_col):
        ms = jnp.mean(blk * blk, axis=0, keepdims=True)
        y = blk * lax.rsqrt(ms + EPS) * g_col
        y1, y2 = y[:ROPE_HALF], y[ROPE_HALF:]
        return jnp.concatenate([y1 * cos - y2 * sin, y2 * cos + y1 * sin], axis=0)

    gq = gq_ref[...]
    gk = gk_ref[...]
    for h in range(N_HEADS_A):
        r0 = _T_QA + h * HEAD_DIM
        q = norm_rope(pt[r0:r0 + HEAD_DIM], gq) * Q_SCALE
        qa_ref[h * HEAD_DIM:(h + 1) * HEAD_DIM, :] = q.astype(jnp.bfloat16)
    k_heads = [norm_rope(pt[_T_KA + g * HEAD_DIM:_T_KA + (g + 1) * HEAD_DIM], gk)
               for g in range(N_KV_A)]
    ka_ref[...] = jnp.concatenate(k_heads, axis=0).T.astype(jnp.bfloat16)

    tm = x_ref.shape[0]
    def store_slabs(v_ref, v_t, n_heads, nk):
        for c in range(tm // nk):
            for h in range(n_heads):
                r0 = h * V_ROWS
                v_ref[c, r0:r0 + HEAD_DIM, :] = v_t[h * HEAD_DIM:(h + 1) * HEAD_DIM,
                                                    c * nk:(c + 1) * nk]
                v_ref[c, r0 + HEAD_DIM:r0 + V_ROWS, :] = jnp.ones((ONES_ROWS, nk), jnp.bfloat16)

    store_slabs(va_ref, pt[_T_VA:_T_VA + D_KV_A].astype(jnp.bfloat16), N_KV_A, NK_A)
    qb_ref[...] = (pt[_T_QB:_T_QB + D_B] * Q_SCALE).astype(jnp.bfloat16)
    store_slabs(vb_ref, pt[_T_VB:_T_VB + D_B].astype(jnp.bfloat16), N_HEADS_B, NK_B)


def _proj_call(x2, g_pre, wt, wkb, gq, gk, cos_t, sin_t):
    s_len = x2.shape[0]
    tm = TM_PROJ
    const = lambda i: (0, 0)
    return pl.pallas_call(
        _proj_kernel,
        grid=(s_len // tm,),
        in_specs=[
            pl.BlockSpec((tm, D_MODEL), lambda i: (i, 0)),
            pl.BlockSpec((1, D_MODEL), const),
            pl.BlockSpec((_T_ROWS, D_MODEL), const),
            pl.BlockSpec((D_MODEL, D_B), const),
            pl.BlockSpec((HEAD_DIM, 1), const),
            pl.BlockSpec((HEAD_DIM, 1), const),
            pl.BlockSpec((ROPE_HALF, tm), lambda i: (0, i)),
            pl.BlockSpec((ROPE_HALF, tm), lambda i: (0, i)),
        ],
        out_specs=[
            pl.BlockSpec((D_A, tm), lambda i: (0, i)),
            pl.BlockSpec((tm, D_KV_A), lambda i: (i, 0)),
            pl.BlockSpec((tm // NK_A, V_ROWS * N_KV_A, NK_A), lambda i: (i, 0, 0)),
            pl.BlockSpec((D_B, tm), lambda i: (0, i)),
            pl.BlockSpec((tm, D_B), lambda i: (i, 0)),
            pl.BlockSpec((tm // NK_B, V_ROWS * N_HEADS_B, NK_B), lambda i: (i, 0, 0)),
        ],
        out_shape=[
            jax.ShapeDtypeStruct((D_A, s_len), jnp.bfloat16),
            jax.ShapeDtypeStruct((s_len, D_KV_A), jnp.bfloat16),
            jax.ShapeDtypeStruct((s_len // NK_A, V_ROWS * N_KV_A, NK_A),
                                 jnp.bfloat16),
            jax.ShapeDtypeStruct((D_B, s_len), jnp.bfloat16),
            jax.ShapeDtypeStruct((s_len, D_B), jnp.bfloat16),
            jax.ShapeDtypeStruct((s_len // NK_B, V_ROWS * N_HEADS_B, NK_B),
                                 jnp.bfloat16),
        ],
        compiler_params=pltpu.CompilerParams(
            dimension_semantics=("arbitrary",), vmem_limit_bytes=VMEM_LIMIT),
        name="proj",
    )(x2, g_pre, wt, wkb, gq, gk, cos_t, sin_t)


def _probs(s, m):
    return jnp.exp2(s - m).astype(jnp.bfloat16)


def _softmax_step(s, v_t, m, acc_ref):
    m_new = jnp.maximum(m, jnp.max(s, axis=0, keepdims=True))
    p = _probs(s, m_new)
    groups = v_t.shape[0] // V_ROWS
    width = s.shape[1] // groups
    pv = [jnp.dot(v_t[g * V_ROWS:(g + 1) * V_ROWS], p[:, g * width:(g + 1) * width],
                  preferred_element_type=jnp.float32) for g in range(groups)]
    acc_ref[...] = acc_ref[...] * jnp.exp2(m - m_new) + jnp.concatenate(pv, axis=1)
    return m_new


def _logits_tiles(k_rows, qt_ref, bias_ref, s_ref, n_ct):
    maxima = []
    for ct in range(n_ct):
        cols = slice(ct * MXU_TILE, (ct + 1) * MXU_TILE)
        s = jnp.dot(k_rows, qt_ref[:, cols], preferred_element_type=jnp.float32)
        if bias_ref is not None:
            s = s + bias_ref[:, cols]
        s_ref[:, cols] = s
        maxima.append(jnp.max(s_ref[:, cols], axis=0, keepdims=True))
    return maxima


def _softmax_tiles(s_ref, mx, v_tiles, m, acc_ref):
    m_out = []
    for ct in range(len(mx)):
        cols = slice(ct * MXU_TILE, (ct + 1) * MXU_TILE)
        pv = None
        m_new = mx[ct] if m is None else jnp.maximum(m[ct], mx[ct])
        for r, v_t in enumerate(v_tiles):
            p = _probs(s_ref[r * MXU_TILE:(r + 1) * MXU_TILE, cols], m_new)
            part = jnp.dot(v_t[ct], p, preferred_element_type=jnp.float32)
            pv = part if pv is None else pv + part
        if m is None:
            acc_ref[:, cols] = pv
        else:
            acc_ref[:, cols] = acc_ref[:, cols] * jnp.exp2(m[ct] - m_new) + pv
        m_out.append(m_new)
    return m_out


def _attn_a_kernel(q_ref, k_ref, v_ref, o_ref, qt_ref, acc_ref, *s_refs):
    bq = q_ref.shape[1]
    m_cols = N_HEADS_A * bq
    grp = N_HEADS_A // N_KV_A
    zeros = jnp.zeros((HEAD_DIM, bq), jnp.bfloat16)
    for h in range(N_HEADS_A):
        qh = q_ref[h * HEAD_DIM:(h + 1) * HEAD_DIM, :]
        parts = [qh if g == h // grp else zeros for g in range(N_KV_A)]
        qt_ref[:, h * bq:(h + 1) * bq] = jnp.concatenate(parts, axis=0)
    acc_ref[...] = jnp.zeros_like(acc_ref)
    n_chunks = k_ref.shape[0] // NK_A

    def logits(c):
        start = pl.multiple_of(c * NK_A, NK_A)
        return jnp.dot(k_ref[pl.ds(start, NK_A), :], qt_ref[...],
                       preferred_element_type=jnp.float32)

    s_refs[0][...] = logits(0)

    def body(t, m):
        c0 = t * UNROLL_A
        for u in range(UNROLL_A):
            nxt = jnp.minimum(c0 + u + 1, n_chunks - 1)
            s_refs[(u + 1) % 2][...] = logits(nxt)
            m = _softmax_step(s_refs[u % 2][...], v_ref[c0 + u], m, acc_ref)
        return m

    lax.fori_loop(0, n_chunks // UNROLL_A, body, jnp.full((1, m_cols), NEG_BIG, jnp.float32))
    inv = 1.0 / acc_ref[HEAD_DIM:HEAD_DIM + 1, :]
    for h in range(N_HEADS_A):
        cols = slice(h * bq, (h + 1) * bq)
        o_ref[h * HEAD_DIM:(h + 1) * HEAD_DIM, :] = acc_ref[:HEAD_DIM, cols] * inv[:, cols]


def _attn_a_call(qa_t, ka, va_t):
    s_len = ka.shape[0]
    bq = BQ_A
    return pl.pallas_call(
        _attn_a_kernel,
        grid=(s_len // bq,),
        in_specs=[
            pl.BlockSpec((D_A, bq), lambda i: (0, i)),
            pl.BlockSpec((s_len, D_KV_A), lambda i: (0, 0)),
            pl.BlockSpec((s_len // NK_A, V_ROWS * N_KV_A, NK_A), lambda i: (0, 0, 0)),
        ],
        out_specs=pl.BlockSpec((D_A, bq), lambda i: (0, i)),
        out_shape=jax.ShapeDtypeStruct((D_A, s_len), jnp.float32),
        scratch_shapes=[
            pltpu.VMEM((D_KV_A, N_HEADS_A * bq), jnp.bfloat16),
            pltpu.VMEM((V_ROWS, N_HEADS_A * bq), jnp.float32),
        ] + [pltpu.VMEM((NK_A, N_HEADS_A * bq), jnp.float32)] * 2,
        compiler_params=pltpu.CompilerParams(
            dimension_semantics=("arbitrary",), vmem_limit_bytes=VMEM_LIMIT),
        name="attn_a",
    )(qa_t, ka, va_t)


def _attn_b_kernel(q_ref, k_ref, v_ref, frev_ref, o_ref, qt_ref, acc_ref, bias_ref, s_ref):
    i = pl.program_id(1)
    bq = q_ref.shape[1]
    zeros = jnp.zeros((HEAD_DIM, bq), jnp.bfloat16)
    q0 = q_ref[:HEAD_DIM, :]
    q1 = q_ref[HEAD_DIM:, :]
    qt_ref[:, :bq] = jnp.concatenate([q0, zeros], axis=0)
    qt_ref[:, bq:] = jnp.concatenate([zeros, q1], axis=0)

    n_chunks = k_ref.shape[0] // NK_B
    back = B_REACH // NK_B
    span = (bq + 2 * B_REACH) // NK_B
    first = i * (bq // NK_B) - back
    interior = jnp.logical_and(first >= 0, first + span <= n_chunks)

    @pl.when(i == 0)
    def _build_bias():
        for hh in range(2):
            for c in range(span):
                seg = (span - 1 - c) * NK_B
                row = frev_ref[hh:hh + 1, seg:seg + 2 * bq]
                rolled = pltpu.roll(jnp.broadcast_to(row, (NK_B, 2 * bq)), 0, 1,
                                    stride=1, stride_axis=0)
                bias_ref[c * NK_B:(c + 1) * NK_B, hh * bq:(hh + 1) * bq] = rolled[:, bq:]

    def write():
        inv = 1.0 / acc_ref[HEAD_DIM:HEAD_DIM + 1, :]
        o_ref[:HEAD_DIM, :] = acc_ref[:HEAD_DIM, :bq] * inv[:, :bq]
        o_ref[HEAD_DIM:, :] = acc_ref[:HEAD_DIM, bq:] * inv[:, bq:]

    @pl.when(interior)
    def _whole_window():
        n_ct = 2 * bq // MXU_TILE
        k_start = pl.multiple_of(first * NK_B, NK_B)
        mx = _logits_tiles(k_ref[pl.ds(k_start, span * NK_B), :], qt_ref, bias_ref, s_ref, n_ct)
        v_tiles = [[v_ref[first + c, hh * V_ROWS:(hh + 1) * V_ROWS, :] for hh in range(n_ct)]
                   for c in range(span)]
        _softmax_tiles(s_ref, mx, v_tiles, None, acc_ref)
        write()

    @pl.when(jnp.logical_not(interior))
    def _clipped_window():
        acc_ref[...] = jnp.zeros_like(acc_ref)
        c_lo = jnp.maximum(0, -first)
        c_hi = jnp.minimum(span, n_chunks - first)

        def body(c, m):
            kc = first + c
            k_start = pl.multiple_of(kc * NK_B, NK_B)
            b_start = pl.multiple_of(c * NK_B, NK_B)
            s = jnp.dot(k_ref[pl.ds(k_start, NK_B), :], qt_ref[...],
                        preferred_element_type=jnp.float32)
            s = s + bias_ref[pl.ds(b_start, NK_B), :]
            return _softmax_step(s, v_ref[kc], m, acc_ref)

        lax.fori_loop(c_lo, c_hi, body, jnp.full((1, 2 * bq), NEG_BIG, jnp.float32))
        write()


def _attn_b_call(qb_t, kb, vb_t, frev):
    s_len = kb.shape[0]
    bq = BQ_B
    pairs = N_HEADS_B // 2
    span_rows = bq + 2 * B_REACH
    assert bq == NK_B == MXU_TILE and frev.shape[-1] == span_rows + bq
    return pl.pallas_call(
        _attn_b_kernel,
        grid=(pairs, s_len // bq),
        in_specs=[
            pl.BlockSpec((2 * HEAD_DIM, bq), lambda j, i: (j, i)),
            pl.BlockSpec((s_len, 2 * HEAD_DIM), lambda j, i: (0, j)),
            pl.BlockSpec((s_len // NK_B, 2 * V_ROWS, NK_B), lambda j, i: (0, j, 0)),
            pl.BlockSpec((None, 2, span_rows + bq), lambda j, i: (j, 0, 0)),
        ],
        out_specs=pl.BlockSpec((2 * HEAD_DIM, bq), lambda j, i: (j, i)),
        out_shape=jax.ShapeDtypeStruct((D_B, s_len), jnp.float32),
        scratch_shapes=[
            pltpu.VMEM((2 * HEAD_DIM, 2 * bq), jnp.bfloat16),
            pltpu.VMEM((V_ROWS, 2 * bq), jnp.float32),
            pltpu.VMEM((span_rows, 2 * bq), jnp.float32),
            pltpu.VMEM((span_rows, 2 * bq), jnp.float32),
        ],
        compiler_params=pltpu.CompilerParams(
            dimension_semantics=("arbitrary", "arbitrary"), vmem_limit_bytes=VMEM_LIMIT),
        name="attn_b",
    )(qb_t, kb, vb_t, frev)


def _t5_bucket_index(rel):
    nb = N_BUCKETS // 2
    max_exact = nb // 2
    side = jnp.where(rel > 0, nb, 0)
    n = jnp.abs(rel)
    large = max_exact + (jnp.log(jnp.maximum(n, max_exact).astype(jnp.float32) / max_exact)
                         / math.log(MAX_DISTANCE / max_exact) * (nb - max_exact)).astype(jnp.int32)
    large = jnp.minimum(large, nb - 1)
    return side + jnp.where(n < max_exact, n, large)


def _dilated_bias_table(rel_bias):
    bq = BQ_B
    reach_all = B_REACH + bq - 1
    delta = jnp.arange(-reach_all, reach_all + 1)
    count = jnp.zeros(delta.shape, jnp.float32)
    for window, dilation in DILATED_PATTERNS:
        inside = (delta % dilation == 0) & (jnp.abs(delta) <= window // 2)
        count = count + inside.astype(jnp.float32)
    table = rel_bias[_t5_bucket_index(delta)].astype(jnp.float32)
    table = jnp.where((count > 0)[:, None],
                      (table + jnp.log(jnp.maximum(count, 1.0))[:, None]) * LOG2E, NEG_BIG)
    frev = jnp.pad(table[::-1].T, ((0, 0), (1, 0)))
    return frev.reshape(N_HEADS_B // 2, 2, frev.shape[1])


def _out_kernel(ya_ref, yb_ref, x_ref, ga_ref, gb_ref, w_ref, gpost_ref, h_ref):
    def norm_t(y_t, g_col):
        ms = jnp.mean(y_t * y_t, axis=0, keepdims=True)
        return (y_t * lax.rsqrt(ms + EPS) * g_col).astype(jnp.bfloat16)

    y_t = jnp.concatenate([norm_t(ya_ref[...], ga_ref[...]),
                           norm_t(yb_ref[...], gb_ref[...])], axis=0)
    y = lax.dot_general(y_t, w_ref[...], (((0,), (0,)), ((), ())),
                        preferred_element_type=jnp.float32)
    h_ref[...] = x_ref[...] + _rms_rows(y, gpost_ref[...])


def _out_call(ya_t, yb_t, x2, ga, gb, w_out, g_post):
    s_len = x2.shape[0]
    tm = TM_OUT
    const = lambda i: (0, 0)
    return pl.pallas_call(
        _out_kernel,
        grid=(s_len // tm,),
        in_specs=[
            pl.BlockSpec((D_A, tm), lambda i: (0, i)),
            pl.BlockSpec((D_B, tm), lambda i: (0, i)),
            pl.BlockSpec((tm, D_MODEL), lambda i: (i, 0)),
            pl.BlockSpec((D_A, 1), const),
            pl.BlockSpec((D_B, 1), const),
            pl.BlockSpec((D_A + D_B, D_MODEL), const),
            pl.BlockSpec((1, D_MODEL), const),
        ],
        out_specs=pl.BlockSpec((tm, D_MODEL), lambda i: (i, 0)),
        out_shape=jax.ShapeDtypeStruct((s_len, D_MODEL), jnp.float32),
        compiler_params=pltpu.CompilerParams(
            dimension_semantics=("arbitrary",), vmem_limit_bytes=VMEM_LIMIT),
        name="out_proj",
    )(ya_t, yb_t, x2, ga, gb, w_out, g_post)


def _ffn_kernel(h_ref, p_ref, g1_ref, w1_ref, w2_ref, g2_ref, g3_ref, wg_ref, wp_ref, o_ref):
    h = h_ref[...]
    xn = _rms_rows(h, g1_ref[...]).astype(jnp.bfloat16)
    f = None
    for c in range(D_FF // FF_CHUNK):
        cols = slice(c * FF_CHUNK, (c + 1) * FF_CHUNK)
        u = jnp.dot(xn, w1_ref[:, cols], preferred_element_type=jnp.float32)
        u = jnp.square(jnp.maximum(u, 0.0)).astype(jnp.bfloat16)
        part = jnp.dot(u, w2_ref[cols, :], preferred_element_type=jnp.float32)
        f = part if f is None else f + part
    h = h + _rms_rows(f, g2_ref[...])
    gate_in = _rms_rows(h, g3_ref[...]).astype(jnp.bfloat16)
    gate = jax.nn.sigmoid(jnp.dot(gate_in, wg_ref[...], preferred_element_type=jnp.float32))
    emb = jnp.dot(p_ref[...].astype(jnp.bfloat16), wp_ref[...], preferred_element_type=jnp.float32)
    o_ref[...] = h + gate * emb


def _ffn_call(h1, p2, g1, w1, w2, g2, g3, wg, wp):
    s_len = h1.shape[0]
    tm = TM_FFN
    const = lambda i: (0, 0)
    resident = functools.partial(pl.BlockSpec, index_map=const, pipeline_mode=pl.Buffered(1))
    return pl.pallas_call(
        _ffn_kernel,
        grid=(s_len // tm,),
        in_specs=[
            pl.BlockSpec((tm, D_MODEL), lambda i: (i, 0)),
            pl.BlockSpec((tm, D_PLE), lambda i: (i, 0)),
            pl.BlockSpec((1, D_MODEL), const),
            resident((D_MODEL, D_FF)),
            resident((D_FF, D_MODEL)),
            pl.BlockSpec((1, D_MODEL), const),
            pl.BlockSpec((1, D_MODEL), const),
            resident((D_MODEL, D_MODEL)),
            resident((D_PLE, D_MODEL)),
        ],
        out_specs=pl.BlockSpec((tm, D_MODEL), lambda i: (i, 0)),
        out_shape=jax.ShapeDtypeStruct((s_len, D_MODEL), jnp.float32),
        compiler_params=pltpu.CompilerParams(
            dimension_semantics=("arbitrary",), vmem_limit_bytes=VMEM_LIMIT),
        name="ffn_ple",
    )(h1, p2, g1, w1, w2, g2, g3, wg, wp)


def _rope_tables_t(n_tokens):
    tok = jnp.arange(n_tokens)
    row = (tok // GRID_W).astype(jnp.float32)
    col = (tok % GRID_W).astype(jnp.float32)
    n_axis = ROPE_HALF // 2
    inv_freq = ROPE_THETA ** (-jnp.arange(n_axis, dtype=jnp.float32) / n_axis)
    ang = jnp.concatenate([inv_freq[:, None] * row[None, :], inv_freq[:, None] * col[None, :]], axis=0)
    return jnp.cos(ang), jnp.sin(ang)


def _layer(h, p_i, w_in, g_attn_pre, g_q, g_k, g_out_a, g_out_b, w_out, g_attn_post, bias_t,
           g_mlp_pre, w_ff1, w_ff2, g_mlp_post, g_ple, w_ple_gate, w_ple_proj, cos_t, sin_t):
    bf = jnp.bfloat16
    o_ka, o_va = D_A, D_A + D_KV_A
    o_qb = D_A + 2 * D_KV_A
    o_kb, o_vb = o_qb + D_B, o_qb + 2 * D_B
    wt = jnp.concatenate([w_in[:, :o_qb + D_B], w_in[:, o_vb:]], axis=1).T.astype(bf)
    wkb = w_in[:, o_kb:o_vb].astype(bf)
    row = lambda g: g.reshape(1, -1)
    col = lambda g: g.reshape(-1, 1)

    qa_t, ka, va_t, qb_t, kb, vb_t = _proj_call(
        h, row(g_attn_pre), wt, wkb, col(g_q), col(g_k), cos_t, sin_t)
    ya_t = _attn_a_call(qa_t, ka, va_t)
    yb_t = _attn_b_call(qb_t, kb, vb_t, bias_t)
    h1 = _out_call(ya_t, yb_t, h, col(g_out_a), col(g_out_b), w_out.astype(bf), row(g_attn_post))
    return _ffn_call(h1, p_i, row(g_mlp_pre), w_ff1.astype(bf), w_ff2.astype(bf), row(g_mlp_post),
                     row(g_ple), w_ple_gate.astype(bf), w_ple_proj.astype(bf))


def kernel(x, p, w_in, g_attn_pre, g_q, g_k, g_out_a, g_out_b, w_out, g_attn_post, rel_bias,
           g_mlp_pre, w_ff1, w_ff2, g_mlp_post, g_ple, w_ple_gate, w_ple_proj):
    b, s_len, _ = x.shape
    cos_t, sin_t = _rope_tables_t(s_len)
    bias_t = _dilated_bias_table(rel_bias)
    outs = []
    for bi in range(b):
        h = x[bi]
        for i in range(w_in.shape[0]):
            h = _layer(h, p[i, bi], w_in[i], g_attn_pre[i], g_q[i], g_k[i], g_out_a[i], g_out_b[i],
                       w_out[i], g_attn_post[i], bias_t, g_mlp_pre[i], w_ff1[i], w_ff2[i],
                       g_mlp_post[i], g_ple[i], w_ple_gate[i], w_ple_proj[i], cos_t, sin_t)
        outs.append(h)
    return jnp.stack(outs, axis=0)
```

```python
import functools
import math

import jax
import jax.numpy as jnp
from jax import lax
from jax.experimental import pallas as pl
from jax.experimental.pallas import tpu as pltpu

D_MODEL = 1024
HEAD_DIM = 64
N_HEADS_A = 8
N_KV_A = 2
N_HEADS_B = 8
D_A = N_HEADS_A * HEAD_DIM
D_KV_A = N_KV_A * HEAD_DIM
D_B = N_HEADS_B * HEAD_DIM
D_FF = 4 * D_MODEL
D_PLE = 256
GRID_W = 64
ROPE_THETA = 10000.0
ROPE_HALF = HEAD_DIM // 2
DILATED_PATTERNS = ((128, 1), (512, 4), (2048, 16))
N_BUCKETS = 32
MAX_DISTANCE = 1024
EPS = 1e-6
NEG_BIG = -1e30

LOG2E = math.log2(math.e)
Q_SCALE = HEAD_DIM ** -0.5 * LOG2E

MXU_TILE = 256
ONES_ROWS = 16
V_ROWS = HEAD_DIM + ONES_ROWS
VMEM_LIMIT = 56 * 1024 * 1024

TM_PROJ = 512
BQ_A = 128
NK_A = 256
UNROLL_A = 8
UNROLL_A_BOUNDED = 16
SHIFT_SPAN_LIMIT = 96.0
BQ_B = 256
NK_B = 256
B_REACH = 1024
TM_OUT = 512
TM_FFN = 512
FF_CHUNK = 1024

_T_QA, _T_KA, _T_VA, _T_QB, _T_VB = 0, 512, 640, 768, 1280
_T_ROWS = 1792


def _rms_rows(x, g_row):
    ms = jnp.mean(x * x, axis=-1, keepdims=True)
    return x * lax.rsqrt(ms + EPS) * g_row


def _proj_kernel(x_ref, g_ref, wt_ref, wkb_ref, gq_ref, gk_ref, cos_ref, sin_ref,
                 qa_ref, ka_ref, va_ref, qb_ref, kb_ref, vb_ref, qa_len_ref, ka_len_ref):
    xn = _rms_rows(x_ref[...], g_ref[...]).astype(jnp.bfloat16)
    pt = lax.dot_general(wt_ref[...], xn, (((1,), (1,)), ((), ())),
                         preferred_element_type=jnp.float32)
    kb = jnp.dot(xn, wkb_ref[...], preferred_element_type=jnp.float32)
    kb_ref[...] = kb.astype(jnp.bfloat16)

    cos = cos_ref[...]
    sin = sin_ref[...]

    def norm_rope(blk, g_col):
        ms = jnp.mean(blk * blk, axis=0, keepdims=True)
        y = blk * lax.rsqrt(ms + EPS) * g_col
        y1, y2 = y[:ROPE_HALF], y[ROPE_HALF:]
        return jnp.concatenate([y1 * cos - y2 * sin, y2 * cos + y1 * sin], axis=0)

    def sq_norm(v):
        vf = v.astype(jnp.float32)
        return jnp.sum(vf * vf, axis=0, keepdims=True)

    gq = gq_ref[...]
    gk = gk_ref[...]
    q_len2 = None
    for h in range(N_HEADS_A):
        r0 = _T_QA + h * HEAD_DIM
        q = (norm_rope(pt[r0:r0 + HEAD_DIM], gq) * Q_SCALE).astype(jnp.bfloat16)
        qa_ref[h * HEAD_DIM:(h + 1) * HEAD_DIM, :] = q
        q_len2 = sq_norm(q) if q_len2 is None else jnp.maximum(q_len2, sq_norm(q))
    k_heads = [norm_rope(pt[_T_KA + g * HEAD_DIM:_T_KA + (g + 1) * HEAD_DIM], gk).astype(jnp.bfloat16)
               for g in range(N_KV_A)]
    k_len2 = sq_norm(k_heads[0])
    for k in k_heads[1:]:
        k_len2 = jnp.maximum(k_len2, sq_norm(k))
    qa_len_ref[...] = q_len2
    ka_len_ref[...] = k_len2
    ka_ref[...] = jnp.concatenate([k.astype(jnp.float32) for k in k_heads], axis=0).T.astype(
        jnp.bfloat16)

    tm = x_ref.shape[0]
    def store_slabs(v_ref, v_t, n_heads, nk):
        for c in range(tm // nk):
            for h in range(n_heads):
                r0 = h * V_ROWS
                v_ref[c, r0:r0 + HEAD_DIM, :] = v_t[h * HEAD_DIM:(h + 1) * HEAD_DIM,
                                                    c * nk:(c + 1) * nk]
                v_ref[c, r0 + HEAD_DIM:r0 + V_ROWS, :] = jnp.ones((ONES_ROWS, nk), jnp.bfloat16)

    store_slabs(va_ref, pt[_T_VA:_T_VA + D_KV_A].astype(jnp.bfloat16), N_KV_A, NK_A)
    qb_ref[...] = (pt[_T_QB:_T_QB + D_B] * Q_SCALE).astype(jnp.bfloat16)
    store_slabs(vb_ref, pt[_T_VB:_T_VB + D_B].astype(jnp.bfloat16), N_HEADS_B, NK_B)


def _proj_call(x2, g_pre, wt, wkb, gq, gk, cos_t, sin_t):
    s_len = x2.shape[0]
    tm = TM_PROJ
    const = lambda i: (0, 0)
    return pl.pallas_call(
        _proj_kernel,
        grid=(s_len // tm,),
        in_specs=[
            pl.BlockSpec((tm, D_MODEL), lambda i: (i, 0)),
            pl.BlockSpec((1, D_MODEL), const),
            pl.BlockSpec((_T_ROWS, D_MODEL), const),
            pl.BlockSpec((D_MODEL, D_B), const),
            pl.BlockSpec((HEAD_DIM, 1), const),
            pl.BlockSpec((HEAD_DIM, 1), const),
            pl.BlockSpec((ROPE_HALF, tm), lambda i: (0, i)),
            pl.BlockSpec((ROPE_HALF, tm), lambda i: (0, i)),
        ],
        out_specs=[
            pl.BlockSpec((D_A, tm), lambda i: (0, i)),
            pl.BlockSpec((tm, D_KV_A), lambda i: (i, 0)),
            pl.BlockSpec((tm // NK_A, V_ROWS * N_KV_A, NK_A), lambda i: (i, 0, 0)),
            pl.BlockSpec((D_B, tm), lambda i: (0, i)),
            pl.BlockSpec((tm, D_B), lambda i: (i, 0)),
            pl.BlockSpec((tm // NK_B, V_ROWS * N_HEADS_B, NK_B), lambda i: (i, 0, 0)),
            pl.BlockSpec((1, tm), lambda i: (0, i)),
            pl.BlockSpec((1, tm), lambda i: (0, i)),
        ],
        out_shape=[
            jax.ShapeDtypeStruct((D_A, s_len), jnp.bfloat16),
            jax.ShapeDtypeStruct((s_len, D_KV_A), jnp.bfloat16),
            jax.ShapeDtypeStruct((s_len // NK_A, V_ROWS * N_KV_A, NK_A),
                                 jnp.bfloat16),
            jax.ShapeDtypeStruct((D_B, s_len), jnp.bfloat16),
            jax.ShapeDtypeStruct((s_len, D_B), jnp.bfloat16),
            jax.ShapeDtypeStruct((s_len // NK_B, V_ROWS * N_HEADS_B, NK_B),
                                 jnp.bfloat16),
            jax.ShapeDtypeStruct((1, s_len), jnp.float32),
            jax.ShapeDtypeStruct((1, s_len), jnp.float32),
        ],
        compiler_params=pltpu.CompilerParams(
            dimension_semantics=("arbitrary",), vmem_limit_bytes=VMEM_LIMIT),
        name="proj",
    )(x2, g_pre, wt, wkb, gq, gk, cos_t, sin_t)


def _probs(s, m):
    return jnp.exp2(s - m).astype(jnp.bfloat16)


def _pv_product(p, v_t):
    groups = v_t.shape[0] // V_ROWS
    width = p.shape[1] // groups
    pv = [jnp.dot(v_t[g * V_ROWS:(g + 1) * V_ROWS], p[:, g * width:(g + 1) * width],
                  preferred_element_type=jnp.float32) for g in range(groups)]
    return jnp.concatenate(pv, axis=1)


def _softmax_step(s, v_t, m, acc_ref):
    m_new = jnp.maximum(m, jnp.max(s, axis=0, keepdims=True))
    acc_ref[...] = acc_ref[...] * jnp.exp2(m - m_new) + _pv_product(_probs(s, m_new), v_t)
    return m_new


def _logits_tiles(k_rows, qt_ref, bias_ref, s_ref, n_ct):
    maxima = []
    for ct in range(n_ct):
        cols = slice(ct * MXU_TILE, (ct + 1) * MXU_TILE)
        s = jnp.dot(k_rows, qt_ref[:, cols], preferred_element_type=jnp.float32)
        if bias_ref is not None:
            s = s + bias_ref[:, cols]
        s_ref[:, cols] = s
        maxima.append(jnp.max(s_ref[:, cols], axis=0, keepdims=True))
    return maxima


def _softmax_tiles(s_ref, mx, v_tiles, m, acc_ref):
    m_out = []
    for ct in range(len(mx)):
        cols = slice(ct * MXU_TILE, (ct + 1) * MXU_TILE)
        pv = None
        m_new = mx[ct] if m is None else jnp.maximum(m[ct], mx[ct])
        for r, v_t in enumerate(v_tiles):
            p = _probs(s_ref[r * MXU_TILE:(r + 1) * MXU_TILE, cols], m_new)
            part = jnp.dot(v_t[ct], p, preferred_element_type=jnp.float32)
            pv = part if pv is None else pv + part
        if m is None:
            acc_ref[:, cols] = pv
        else:
            acc_ref[:, cols] = acc_ref[:, cols] * jnp.exp2(m[ct] - m_new) + pv
        m_out.append(m_new)
    return m_out


def _attn_a_kernel(q_ref, k_ref, v_ref, *refs, bounded):
    if bounded:
        klen_ref, o_ref, qt_ref, acc_ref, *s_refs = refs
    else:
        o_ref, qt_ref, acc_ref, *s_refs = refs
    bq = q_ref.shape[1]
    m_cols = N_HEADS_A * bq
    grp = N_HEADS_A // N_KV_A
    zeros = jnp.zeros((HEAD_DIM, bq), jnp.bfloat16)
    q_len = []
    for h in range(N_HEADS_A):
        qh = q_ref[h * HEAD_DIM:(h + 1) * HEAD_DIM, :]
        parts = [qh if g == h // grp else zeros for g in range(N_KV_A)]
        qt_ref[:, h * bq:(h + 1) * bq] = jnp.concatenate(parts, axis=0)
        qf = qh.astype(jnp.float32)
        q_len.append(jnp.sqrt(jnp.sum(qf * qf, axis=0, keepdims=True)))
    acc_ref[...] = jnp.zeros_like(acc_ref)
    n_chunks = k_ref.shape[0] // NK_A
    unroll = UNROLL_A_BOUNDED if bounded else UNROLL_A

    def logits(c):
        start = pl.multiple_of(c * NK_A, NK_A)
        return jnp.dot(k_ref[pl.ds(start, NK_A), :], qt_ref[...],
                       preferred_element_type=jnp.float32)

    s_refs[0][...] = logits(0)

    if bounded:
        shift = jnp.concatenate(q_len, axis=1) * klen_ref[...]

        def body(t, carry):
            c0 = t * unroll
            total = None
            for u in range(unroll):
                nxt = jnp.minimum(c0 + u + 1, n_chunks - 1)
                s_refs[(u + 1) % 2][...] = logits(nxt)
                pv = _pv_product(_probs(s_refs[u % 2][...], shift), v_ref[c0 + u])
                total = pv if total is None else total + pv
            acc_ref[...] += total
            return carry

        lax.fori_loop(0, n_chunks // unroll, body, 0)
    else:
        def body(t, m):
            c0 = t * unroll
            for u in range(unroll):
                nxt = jnp.minimum(c0 + u + 1, n_chunks - 1)
                s_refs[(u + 1) % 2][...] = logits(nxt)
                m = _softmax_step(s_refs[u % 2][...], v_ref[c0 + u], m, acc_ref)
            return m

        lax.fori_loop(0, n_chunks // unroll, body, jnp.full((1, m_cols), NEG_BIG, jnp.float32))
    inv = 1.0 / acc_ref[HEAD_DIM:HEAD_DIM + 1, :]
    for h in range(N_HEADS_A):
        cols = slice(h * bq, (h + 1) * bq)
        o_ref[h * HEAD_DIM:(h + 1) * HEAD_DIM, :] = acc_ref[:HEAD_DIM, cols] * inv[:, cols]


def _attn_a_call(qa_t, ka, va_t, k_len, bounded):
    s_len = ka.shape[0]
    bq = BQ_A
    operands = (qa_t, ka, va_t) + ((k_len,) if bounded else ())
    return pl.pallas_call(
        functools.partial(_attn_a_kernel, bounded=bounded),
        grid=(s_len // bq,),
        in_specs=[
            pl.BlockSpec((D_A, bq), lambda i: (0, i)),
            pl.BlockSpec((s_len, D_KV_A), lambda i: (0, 0)),
            pl.BlockSpec((s_len // NK_A, V_ROWS * N_KV_A, NK_A), lambda i: (0, 0, 0)),
        ] + ([pl.BlockSpec((1, 1), lambda i: (0, 0))] if bounded else []),
        out_specs=pl.BlockSpec((D_A, bq), lambda i: (0, i)),
        out_shape=jax.ShapeDtypeStruct((D_A, s_len), jnp.float32),
        scratch_shapes=[
            pltpu.VMEM((D_KV_A, N_HEADS_A * bq), jnp.bfloat16),
            pltpu.VMEM((V_ROWS, N_HEADS_A * bq), jnp.float32),
        ] + [pltpu.VMEM((NK_A, N_HEADS_A * bq), jnp.float32)] * 2,
        compiler_params=pltpu.CompilerParams(
            dimension_semantics=("arbitrary",), vmem_limit_bytes=VMEM_LIMIT),
        name="attn_a_bounded" if bounded else "attn_a_online",
    )(*operands)


def _attn_a(qa_t, ka, va_t, q_len2, k_len2):
    q_len = jnp.sqrt(jnp.max(q_len2))
    k_len = jnp.sqrt(jnp.max(k_len2))
    k_len_arr = k_len.reshape(1, 1)
    return lax.cond(2.0 * q_len * k_len <= SHIFT_SPAN_LIMIT,
                    functools.partial(_attn_a_call, bounded=True),
                    functools.partial(_attn_a_call, bounded=False),
                    qa_t, ka, va_t, k_len_arr)


def _attn_b_kernel(q_ref, k_ref, v_ref, frev_ref, o_ref, qt_ref, acc_ref, bias_ref, s_ref):
    i = pl.program_id(1)
    bq = q_ref.shape[1]
    zeros = jnp.zeros((HEAD_DIM, bq), jnp.bfloat16)
    q0 = q_ref[:HEAD_DIM, :]
    q1 = q_ref[HEAD_DIM:, :]
    qt_ref[:, :bq] = jnp.concatenate([q0, zeros], axis=0)
    qt_ref[:, bq:] = jnp.concatenate([zeros, q1], axis=0)

    n_chunks = k_ref.shape[0] // NK_B
    back = B_REACH // NK_B
    span = (bq + 2 * B_REACH) // NK_B
    first = i * (bq // NK_B) - back
    interior = jnp.logical_and(first >= 0, first + span <= n_chunks)

    @pl.when(i == 0)
    def _build_bias():
        for hh in range(2):
            for c in range(span):
                seg = (span - 1 - c) * NK_B
                row = frev_ref[hh:hh + 1, seg:seg + 2 * bq]
                rolled = pltpu.roll(jnp.broadcast_to(row, (NK_B, 2 * bq)), 0, 1,
                                    stride=1, stride_axis=0)
                bias_ref[c * NK_B:(c + 1) * NK_B, hh * bq:(hh + 1) * bq] = rolled[:, bq:]

    def write():
        inv = 1.0 / acc_ref[HEAD_DIM:HEAD_DIM + 1, :]
        o_ref[:HEAD_DIM, :] = acc_ref[:HEAD_DIM, :bq] * inv[:, :bq]
        o_ref[HEAD_DIM:, :] = acc_ref[:HEAD_DIM, bq:] * inv[:, bq:]

    @pl.when(interior)
    def _whole_window():
        n_ct = 2 * bq // MXU_TILE
        k_start = pl.multiple_of(first * NK_B, NK_B)
        mx = _logits_tiles(k_ref[pl.ds(k_start, span * NK_B), :], qt_ref, bias_ref, s_ref, n_ct)
        v_tiles = [[v_ref[first + c, hh * V_ROWS:(hh + 1) * V_ROWS, :] for hh in range(n_ct)]
                   for c in range(span)]
        _softmax_tiles(s_ref, mx, v_tiles, None, acc_ref)
        write()

    @pl.when(jnp.logical_not(interior))
    def _clipped_window():
        acc_ref[...] = jnp.zeros_like(acc_ref)
        c_lo = jnp.maximum(0, -first)
        c_hi = jnp.minimum(span, n_chunks - first)

        def body(c, m):
            kc = first + c
            k_start = pl.multiple_of(kc * NK_B, NK_B)
            b_start = pl.multiple_of(c * NK_B, NK_B)
            s = jnp.dot(k_ref[pl.ds(k_start, NK_B), :], qt_ref[...],
                        preferred_element_type=jnp.float32)
            s = s + bias_ref[pl.ds(b_start, NK_B), :]
            return _softmax_step(s, v_ref[kc], m, acc_ref)

        lax.fori_loop(c_lo, c_hi, body, jnp.full((1, 2 * bq), NEG_BIG, jnp.float32))
        write()


def _attn_b_call(qb_t, kb, vb_t, frev):
    s_len = kb.shape[0]
    bq = BQ_B
    pairs = N_HEADS_B // 2
    span_rows = bq + 2 * B_REACH
    assert bq == NK_B == MXU_TILE and frev.shape[-1] == span_rows + bq
    return pl.pallas_call(
        _attn_b_kernel,
        grid=(pairs, s_len // bq),
        in_specs=[
            pl.BlockSpec((2 * HEAD_DIM, bq), lambda j, i: (j, i)),
            pl.BlockSpec((s_len, 2 * HEAD_DIM), lambda j, i: (0, j)),
            pl.BlockSpec((s_len // NK_B, 2 * V_ROWS, NK_B), lambda j, i: (0, j, 0)),
            pl.BlockSpec((None, 2, span_rows + bq), lambda j, i: (j, 0, 0)),
        ],
        out_specs=pl.BlockSpec((2 * HEAD_DIM, bq), lambda j, i: (j, i)),
        out_shape=jax.ShapeDtypeStruct((D_B, s_len), jnp.float32),
        scratch_shapes=[
            pltpu.VMEM((2 * HEAD_DIM, 2 * bq), jnp.bfloat16),
            pltpu.VMEM((V_ROWS, 2 * bq), jnp.float32),
            pltpu.VMEM((span_rows, 2 * bq), jnp.float32),
            pltpu.VMEM((span_rows, 2 * bq), jnp.float32),
        ],
        compiler_params=pltpu.CompilerParams(
            dimension_semantics=("arbitrary", "arbitrary"), vmem_limit_bytes=VMEM_LIMIT),
        name="attn_b",
    )(qb_t, kb, vb_t, frev)


def _t5_bucket_index(rel):
    nb = N_BUCKETS // 2
    max_exact = nb // 2
    side = jnp.where(rel > 0, nb, 0)
    n = jnp.abs(rel)
    large = max_exact + (jnp.log(jnp.maximum(n, max_exact).astype(jnp.float32) / max_exact)
                         / math.log(MAX_DISTANCE / max_exact) * (nb - max_exact)).astype(jnp.int32)
    large = jnp.minimum(large, nb - 1)
    return side + jnp.where(n < max_exact, n, large)


def _dilated_bias_table(rel_bias):
    bq = BQ_B
    reach_all = B_REACH + bq - 1
    delta = jnp.arange(-reach_all, reach_all + 1)
    count = jnp.zeros(delta.shape, jnp.float32)
    for window, dilation in DILATED_PATTERNS:
        inside = (delta % dilation == 0) & (jnp.abs(delta) <= window // 2)
        count = count + inside.astype(jnp.float32)
    table = rel_bias[_t5_bucket_index(delta)].astype(jnp.float32)
    table = jnp.where((count > 0)[:, None],
                      (table + jnp.log(jnp.maximum(count, 1.0))[:, None]) * LOG2E, NEG_BIG)
    frev = jnp.pad(table[::-1].T, ((0, 0), (1, 0)))
    return frev.reshape(N_HEADS_B // 2, 2, frev.shape[1])


def _out_kernel(ya_ref, yb_ref, x_ref, ga_ref, gb_ref, w_ref, gpost_ref, h_ref):
    def norm_t(y_t, g_col):
        ms = jnp.mean(y_t * y_t, axis=0, keepdims=True)
        return (y_t * lax.rsqrt(ms + EPS) * g_col).astype(jnp.bfloat16)

    y_t = jnp.concatenate([norm_t(ya_ref[...], ga_ref[...]),
                           norm_t(yb_ref[...], gb_ref[...])], axis=0)
    y = lax.dot_general(y_t, w_ref[...], (((0,), (0,)), ((), ())),
                        preferred_element_type=jnp.float32)
    h_ref[...] = x_ref[...] + _rms_rows(y, gpost_ref[...])


def _out_call(ya_t, yb_t, x2, ga, gb, w_out, g_post):
    s_len = x2.shape[0]
    tm = TM_OUT
    const = lambda i: (0, 0)
    return pl.pallas_call(
        _out_kernel,
        grid=(s_len // tm,),
        in_specs=[
            pl.BlockSpec((D_A, tm), lambda i: (0, i)),
            pl.BlockSpec((D_B, tm), lambda i: (0, i)),
            pl.BlockSpec((tm, D_MODEL), lambda i: (i, 0)),
            pl.BlockSpec((D_A, 1), const),
            pl.BlockSpec((D_B, 1), const),
            pl.BlockSpec((D_A + D_B, D_MODEL), const),
            pl.BlockSpec((1, D_MODEL), const),
        ],
        out_specs=pl.BlockSpec((tm, D_MODEL), lambda i: (i, 0)),
        out_shape=jax.ShapeDtypeStruct((s_len, D_MODEL), jnp.float32),
        compiler_params=pltpu.CompilerParams(
            dimension_semantics=("arbitrary",), vmem_limit_bytes=VMEM_LIMIT),
        name="out_proj",
    )(ya_t, yb_t, x2, ga, gb, w_out, g_post)


def _ffn_kernel(h_ref, p_ref, g1_ref, w1_ref, w2_ref, g2_ref, g3_ref, wg_ref, wp_ref, o_ref):
    h = h_ref[...]
    xn = _rms_rows(h, g1_ref[...]).astype(jnp.bfloat16)
    f = None
    for c in range(D_FF // FF_CHUNK):
        cols = slice(c * FF_CHUNK, (c + 1) * FF_CHUNK)
        u = jnp.dot(xn, w1_ref[:, cols], preferred_element_type=jnp.float32)
        u = jnp.square(jnp.maximum(u, 0.0)).astype(jnp.bfloat16)
        part = jnp.dot(u, w2_ref[cols, :], preferred_element_type=jnp.float32)
        f = part if f is None else f + part
    h = h + _rms_rows(f, g2_ref[...])
    gate_in = _rms_rows(h, g3_ref[...]).astype(jnp.bfloat16)
    gate = jax.nn.sigmoid(jnp.dot(gate_in, wg_ref[...], preferred_element_type=jnp.float32))
    emb = jnp.dot(p_ref[...].astype(jnp.bfloat16), wp_ref[...], preferred_element_type=jnp.float32)
    o_ref[...] = h + gate * emb


def _ffn_call(h1, p2, g1, w1, w2, g2, g3, wg, wp):
    s_len = h1.shape[0]
    tm = TM_FFN
    const = lambda i: (0, 0)
    resident = functools.partial(pl.BlockSpec, index_map=const, pipeline_mode=pl.Buffered(1))
    return pl.pallas_call(
        _ffn_kernel,
        grid=(s_len // tm,),
        in_specs=[
            pl.BlockSpec((tm, D_MODEL), lambda i: (i, 0)),
            pl.BlockSpec((tm, D_PLE), lambda i: (i, 0)),
            pl.BlockSpec((1, D_MODEL), const),
            resident((D_MODEL, D_FF)),
            resident((D_FF, D_MODEL)),
            pl.BlockSpec((1, D_MODEL), const),
            pl.BlockSpec((1, D_MODEL), const),
            resident((D_MODEL, D_MODEL)),
            resident((D_PLE, D_MODEL)),
        ],
        out_specs=pl.BlockSpec((tm, D_MODEL), lambda i: (i, 0)),
        out_shape=jax.ShapeDtypeStruct((s_len, D_MODEL), jnp.float32),
        compiler_params=pltpu.CompilerParams(
            dimension_semantics=("arbitrary",), vmem_limit_bytes=VMEM_LIMIT),
        name="ffn_ple",
    )(h1, p2, g1, w1, w2, g2, g3, wg, wp)


def _rope_tables_t(n_tokens):
    tok = jnp.arange(n_tokens)
    row = (tok // GRID_W).astype(jnp.float32)
    col = (tok % GRID_W).astype(jnp.float32)
    n_axis = ROPE_HALF // 2
    inv_freq = ROPE_THETA ** (-jnp.arange(n_axis, dtype=jnp.float32) / n_axis)
    ang = jnp.concatenate([inv_freq[:, None] * row[None, :], inv_freq[:, None] * col[None, :]], axis=0)
    return jnp.cos(ang), jnp.sin(ang)


def _layer(h, p_i, w_in, g_attn_pre, g_q, g_k, g_out_a, g_out_b, w_out, g_attn_post, bias_t,
           g_mlp_pre, w_ff1, w_ff2, g_mlp_post, g_ple, w_ple_gate, w_ple_proj, cos_t, sin_t):
    bf = jnp.bfloat16
    o_ka, o_va = D_A, D_A + D_KV_A
    o_qb = D_A + 2 * D_KV_A
    o_kb, o_vb = o_qb + D_B, o_qb + 2 * D_B
    wt = jnp.concatenate([w_in[:, :o_qb + D_B], w_in[:, o_vb:]], axis=1).T.astype(bf)
    wkb = w_in[:, o_kb:o_vb].astype(bf)
    row = lambda g: g.reshape(1, -1)
    col = lambda g: g.reshape(-1, 1)

    qa_t, ka, va_t, qb_t, kb, vb_t, qa_len2, ka_len2 = _proj_call(
        h, row(g_attn_pre), wt, wkb, col(g_q), col(g_k), cos_t, sin_t)
    ya_t = _attn_a(qa_t, ka, va_t, qa_len2, ka_len2)
    yb_t = _attn_b_call(qb_t, kb, vb_t, bias_t)
    h1 = _out_call(ya_t, yb_t, h, col(g_out_a), col(g_out_b), w_out.astype(bf), row(g_attn_post))
    return _ffn_call(h1, p_i, row(g_mlp_pre), w_ff1.astype(bf), w_ff2.astype(bf), row(g_mlp_post),
                     row(g_ple), w_ple_gate.astype(bf), w_ple_proj.astype(bf))


def kernel(x, p, w_in, g_attn_pre, g_q, g_k, g_out_a, g_out_b, w_out, g_attn_post, rel_bias,
           g_mlp_pre, w_ff1, w_ff2, g_mlp_post, g_ple, w_ple_gate, w_ple_proj):
    b, s_len, _ = x.shape
    cos_t, sin_t = _rope_tables_t(s_len)
    bias_t = _dilated_bias_table(rel_bias)
    outs = []
    for bi in range(b):
        h = x[bi]
        for i in range(w_in.shape[0]):
            h = _layer(h, p[i, bi], w_in[i], g_attn_pre[i], g_q[i], g_k[i], g_out_a[i], g_out_b[i],
                       w_out[i], g_attn_post[i], bias_t, g_mlp_pre[i], w_ff1[i], w_ff2[i],
                       g_mlp_post[i], g_ple[i], w_ple_gate[i], w_ple_proj[i], cos_t, sin_t)
        outs.append(h)
    return jnp.stack(outs, axis=0)
```

```python
import functools
import math

import jax
import jax.numpy as jnp
from jax import lax
from jax.experimental import pallas as pl
from jax.experimental.pallas import tpu as pltpu

D_MODEL = 1024
HEAD_DIM = 64
N_HEADS_A = 8
N_KV_A = 2
N_HEADS_B = 8
D_A = N_HEADS_A * HEAD_DIM
D_KV_A = N_KV_A * HEAD_DIM
D_B = N_HEADS_B * HEAD_DIM
D_FF = 4 * D_MODEL
D_PLE = 256
GRID_W = 64
ROPE_THETA = 10000.0
ROPE_HALF = HEAD_DIM // 2
DILATED_PATTERNS = ((128, 1), (512, 4), (2048, 16))
N_BUCKETS = 32
MAX_DISTANCE = 1024
EPS = 1e-6
NEG_BIG = -1e30

LOG2E = math.log2(math.e)
Q_SCALE = HEAD_DIM ** -0.5 * LOG2E

MXU_TILE = 256
ONES_ROWS = 16
V_ROWS = HEAD_DIM + ONES_ROWS
VMEM_LIMIT = 56 * 1024 * 1024

TM_PROJ = 512
BQ_A = 128
NK_A = 256
UNROLL_A = 8
UNROLL_A_BOUNDED = 16
SHIFT_SPAN_LIMIT = 96.0
BQ_B = 256
NK_B = 256
B_REACH = 1024
TM_OUT = 512
TM_FFN = 512
FF_CHUNK = 1024

_T_QA, _T_KA, _T_VA = 0, D_A, D_A + D_KV_A
_T_QB, _T_KB, _T_VB = D_A + 2 * D_KV_A, D_A + 2 * D_KV_A + D_B, D_A + 2 * D_KV_A + 2 * D_B
_T_ROWS = D_A + 2 * D_KV_A + 3 * D_B


def _rms_rows(x, g_row):
    ms = jnp.mean(x * x, axis=-1, keepdims=True)
    return x * lax.rsqrt(ms + EPS) * g_row


def _proj_kernel(x_ref, g_ref, wt_ref, gq_ref, gk_ref, cos_ref, sin_ref,
                 qa_ref, ka_ref, va_ref, qb_ref, kb_ref, vb_ref, len_ref):
    xn = _rms_rows(x_ref[...], g_ref[...]).astype(jnp.bfloat16)
    pt = lax.dot_general(wt_ref[...], xn, (((1,), (1,)), ((), ())),
                         preferred_element_type=jnp.float32)

    cos = cos_ref[...]
    sin = sin_ref[...]

    def norm_rope(blk, g_col):
        ms = jnp.mean(blk * blk, axis=0, keepdims=True)
        y = blk * lax.rsqrt(ms + EPS) * g_col
        y1, y2 = y[:ROPE_HALF], y[ROPE_HALF:]
        return jnp.concatenate([y1 * cos - y2 * sin, y2 * cos + y1 * sin], axis=0)

    def sq_norm(v):
        vf = v.astype(jnp.float32)
        return jnp.sum(vf * vf, axis=0, keepdims=True)

    gq = gq_ref[...]
    gk = gk_ref[...]
    q_len2 = None
    for h in range(N_HEADS_A):
        r0 = _T_QA + h * HEAD_DIM
        q = (norm_rope(pt[r0:r0 + HEAD_DIM], gq) * Q_SCALE).astype(jnp.bfloat16)
        qa_ref[h * HEAD_DIM:(h + 1) * HEAD_DIM, :] = q
        q_len2 = sq_norm(q) if q_len2 is None else jnp.maximum(q_len2, sq_norm(q))
    k_heads = [norm_rope(pt[_T_KA + g * HEAD_DIM:_T_KA + (g + 1) * HEAD_DIM], gk).astype(jnp.bfloat16)
               for g in range(N_KV_A)]
    ka_ref[...] = jnp.concatenate([k.astype(jnp.float32) for k in k_heads], axis=0).T.astype(
        jnp.bfloat16)

    def longest2(heads):
        out = sq_norm(heads[0])
        for v in heads[1:]:
            out = jnp.maximum(out, sq_norm(v))
        return out

    qb = (pt[_T_QB:_T_QB + D_B] * Q_SCALE).astype(jnp.bfloat16)
    kb = pt[_T_KB:_T_KB + D_B].astype(jnp.bfloat16)
    qb_ref[...] = qb
    kb_ref[...] = kb.astype(jnp.float32).T.astype(jnp.bfloat16)
    split = lambda v: [v[h * HEAD_DIM:(h + 1) * HEAD_DIM] for h in range(N_HEADS_B)]
    len_ref[0:1, :] = q_len2
    len_ref[1:2, :] = longest2(k_heads)
    len_ref[2:3, :] = longest2(split(qb))
    len_ref[3:4, :] = longest2(split(kb))
    len_ref[4:8, :] = jnp.zeros((4, len_ref.shape[1]), jnp.float32)

    tm = x_ref.shape[0]
    def store_slabs(v_ref, v_t, n_heads, nk):
        for c in range(tm // nk):
            for h in range(n_heads):
                r0 = h * V_ROWS
                v_ref[c, r0:r0 + HEAD_DIM, :] = v_t[h * HEAD_DIM:(h + 1) * HEAD_DIM,
                                                    c * nk:(c + 1) * nk]
                v_ref[c, r0 + HEAD_DIM:r0 + V_ROWS, :] = jnp.ones((ONES_ROWS, nk), jnp.bfloat16)

    store_slabs(va_ref, pt[_T_VA:_T_VA + D_KV_A].astype(jnp.bfloat16), N_KV_A, NK_A)
    store_slabs(vb_ref, pt[_T_VB:_T_VB + D_B].astype(jnp.bfloat16), N_HEADS_B, NK_B)


def _proj_call(x2, g_pre, wt, gq, gk, cos_t, sin_t):
    s_len = x2.shape[0]
    tm = TM_PROJ
    const = lambda i: (0, 0)
    return pl.pallas_call(
        _proj_kernel,
        grid=(s_len // tm,),
        in_specs=[
            pl.BlockSpec((tm, D_MODEL), lambda i: (i, 0)),
            pl.BlockSpec((1, D_MODEL), const),
            pl.BlockSpec((_T_ROWS, D_MODEL), const),
            pl.BlockSpec((HEAD_DIM, 1), const),
            pl.BlockSpec((HEAD_DIM, 1), const),
            pl.BlockSpec((ROPE_HALF, tm), lambda i: (0, i)),
            pl.BlockSpec((ROPE_HALF, tm), lambda i: (0, i)),
        ],
        out_specs=[
            pl.BlockSpec((D_A, tm), lambda i: (0, i)),
            pl.BlockSpec((tm, D_KV_A), lambda i: (i, 0)),
            pl.BlockSpec((tm // NK_A, V_ROWS * N_KV_A, NK_A), lambda i: (i, 0, 0)),
            pl.BlockSpec((D_B, tm), lambda i: (0, i)),
            pl.BlockSpec((tm, D_B), lambda i: (i, 0)),
            pl.BlockSpec((tm // NK_B, V_ROWS * N_HEADS_B, NK_B), lambda i: (i, 0, 0)),
            pl.BlockSpec((8, tm), lambda i: (0, i)),
        ],
        out_shape=[
            jax.ShapeDtypeStruct((D_A, s_len), jnp.bfloat16),
            jax.ShapeDtypeStruct((s_len, D_KV_A), jnp.bfloat16),
            jax.ShapeDtypeStruct((s_len // NK_A, V_ROWS * N_KV_A, NK_A),
                                 jnp.bfloat16),
            jax.ShapeDtypeStruct((D_B, s_len), jnp.bfloat16),
            jax.ShapeDtypeStruct((s_len, D_B), jnp.bfloat16),
            jax.ShapeDtypeStruct((s_len // NK_B, V_ROWS * N_HEADS_B, NK_B),
                                 jnp.bfloat16),
            jax.ShapeDtypeStruct((8, s_len), jnp.float32),
        ],
        compiler_params=pltpu.CompilerParams(
            dimension_semantics=("arbitrary",), vmem_limit_bytes=VMEM_LIMIT),
        name="proj",
    )(x2, g_pre, wt, gq, gk, cos_t, sin_t)


def _probs(s, m):
    return jnp.exp2(s - m).astype(jnp.bfloat16)


def _pv_product(p, v_t):
    groups = v_t.shape[0] // V_ROWS
    width = p.shape[1] // groups
    pv = [jnp.dot(v_t[g * V_ROWS:(g + 1) * V_ROWS], p[:, g * width:(g + 1) * width],
                  preferred_element_type=jnp.float32) for g in range(groups)]
    return jnp.concatenate(pv, axis=1)


def _softmax_step(s, v_t, m, acc_ref):
    m_new = jnp.maximum(m, jnp.max(s, axis=0, keepdims=True))
    acc_ref[...] = acc_ref[...] * jnp.exp2(m - m_new) + _pv_product(_probs(s, m_new), v_t)
    return m_new


def _logits_tiles(k_rows, qt_ref, bias_ref, s_ref, n_ct):
    maxima = []
    for ct in range(n_ct):
        cols = slice(ct * MXU_TILE, (ct + 1) * MXU_TILE)
        s = jnp.dot(k_rows, qt_ref[:, cols], preferred_element_type=jnp.float32)
        if bias_ref is not None:
            s = s + bias_ref[:, cols]
        s_ref[:, cols] = s
        maxima.append(jnp.max(s_ref[:, cols], axis=0, keepdims=True))
    return maxima


def _softmax_tiles(s_ref, mx, v_tiles, m, acc_ref):
    m_out = []
    for ct in range(len(mx)):
        cols = slice(ct * MXU_TILE, (ct + 1) * MXU_TILE)
        pv = None
        m_new = mx[ct] if m is None else jnp.maximum(m[ct], mx[ct])
        for r, v_t in enumerate(v_tiles):
            p = _probs(s_ref[r * MXU_TILE:(r + 1) * MXU_TILE, cols], m_new)
            part = jnp.dot(v_t[ct], p, preferred_element_type=jnp.float32)
            pv = part if pv is None else pv + part
        if m is None:
            acc_ref[:, cols] = pv
        else:
            acc_ref[:, cols] = acc_ref[:, cols] * jnp.exp2(m[ct] - m_new) + pv
        m_out.append(m_new)
    return m_out


def _attn_a_kernel(q_ref, k_ref, v_ref, *refs, bounded):
    if bounded:
        klen_ref, o_ref, qt_ref, acc_ref, *s_refs = refs
    else:
        o_ref, qt_ref, acc_ref, *s_refs = refs
    bq = q_ref.shape[1]
    m_cols = N_HEADS_A * bq
    grp = N_HEADS_A // N_KV_A
    zeros = jnp.zeros((HEAD_DIM, bq), jnp.bfloat16)
    q_len = []
    for h in range(N_HEADS_A):
        qh = q_ref[h * HEAD_DIM:(h + 1) * HEAD_DIM, :]
        parts = [qh if g == h // grp else zeros for g in range(N_KV_A)]
        qt_ref[:, h * bq:(h + 1) * bq] = jnp.concatenate(parts, axis=0)
        qf = qh.astype(jnp.float32)
        q_len.append(jnp.sqrt(jnp.sum(qf * qf, axis=0, keepdims=True)))
    acc_ref[...] = jnp.zeros_like(acc_ref)
    n_chunks = k_ref.shape[0] // NK_A
    unroll = UNROLL_A_BOUNDED if bounded else UNROLL_A

    def logits(c):
        start = pl.multiple_of(c * NK_A, NK_A)
        return jnp.dot(k_ref[pl.ds(start, NK_A), :], qt_ref[...],
                       preferred_element_type=jnp.float32)

    s_refs[0][...] = logits(0)

    if bounded:
        shift = jnp.concatenate(q_len, axis=1) * klen_ref[...]

        def body(t, carry):
            c0 = t * unroll
            total = None
            for u in range(unroll):
                nxt = jnp.minimum(c0 + u + 1, n_chunks - 1)
                s_refs[(u + 1) % 2][...] = logits(nxt)
                pv = _pv_product(_probs(s_refs[u % 2][...], shift), v_ref[c0 + u])
                total = pv if total is None else total + pv
            acc_ref[...] += total
            return carry

        lax.fori_loop(0, n_chunks // unroll, body, 0)
    else:
        def body(t, m):
            c0 = t * unroll
            for u in range(unroll):
                nxt = jnp.minimum(c0 + u + 1, n_chunks - 1)
                s_refs[(u + 1) % 2][...] = logits(nxt)
                m = _softmax_step(s_refs[u % 2][...], v_ref[c0 + u], m, acc_ref)
            return m

        lax.fori_loop(0, n_chunks // unroll, body, jnp.full((1, m_cols), NEG_BIG, jnp.float32))
    inv = 1.0 / acc_ref[HEAD_DIM:HEAD_DIM + 1, :]
    for h in range(N_HEADS_A):
        cols = slice(h * bq, (h + 1) * bq)
        o_ref[h * HEAD_DIM:(h + 1) * HEAD_DIM, :] = acc_ref[:HEAD_DIM, cols] * inv[:, cols]


def _attn_a_call(qa_t, ka, va_t, k_len, bounded):
    s_len = ka.shape[0]
    bq = BQ_A
    operands = (qa_t, ka, va_t) + ((k_len,) if bounded else ())
    return pl.pallas_call(
        functools.partial(_attn_a_kernel, bounded=bounded),
        grid=(s_len // bq,),
        in_specs=[
            pl.BlockSpec((D_A, bq), lambda i: (0, i)),
            pl.BlockSpec((s_len, D_KV_A), lambda i: (0, 0)),
            pl.BlockSpec((s_len // NK_A, V_ROWS * N_KV_A, NK_A), lambda i: (0, 0, 0)),
        ] + ([pl.BlockSpec((1, 1), lambda i: (0, 0))] if bounded else []),
        out_specs=pl.BlockSpec((D_A, bq), lambda i: (0, i)),
        out_shape=jax.ShapeDtypeStruct((D_A, s_len), jnp.float32),
        scratch_shapes=[
            pltpu.VMEM((D_KV_A, N_HEADS_A * bq), jnp.bfloat16),
            pltpu.VMEM((V_ROWS, N_HEADS_A * bq), jnp.float32),
        ] + [pltpu.VMEM((NK_A, N_HEADS_A * bq), jnp.float32)] * 2,
        compiler_params=pltpu.CompilerParams(
            dimension_semantics=("arbitrary",), vmem_limit_bytes=VMEM_LIMIT),
        name="attn_a_bounded" if bounded else "attn_a_online",
    )(*operands)


def _attn_a(qa_t, ka, va_t, q_len2, k_len2):
    q_len = jnp.sqrt(jnp.max(q_len2))
    k_len = jnp.sqrt(jnp.max(k_len2))
    k_len_arr = k_len.reshape(1, 1)
    return lax.cond(2.0 * q_len * k_len <= SHIFT_SPAN_LIMIT,
                    functools.partial(_attn_a_call, bounded=True),
                    functools.partial(_attn_a_call, bounded=False),
                    qa_t, ka, va_t, k_len_arr)


def _attn_b_kernel(q_ref, k_ref, v_ref, frev_ref, *refs, bounded):
    if bounded:
        stats_ref, o_ref, qt_ref, acc_ref, bias_ref, *s_refs = refs
    else:
        o_ref, qt_ref, acc_ref, bias_ref, s_ref = refs
    i = pl.program_id(1)
    bq = q_ref.shape[1]
    zeros = jnp.zeros((HEAD_DIM, bq), jnp.bfloat16)
    q0 = q_ref[:HEAD_DIM, :]
    q1 = q_ref[HEAD_DIM:, :]
    qt_ref[:, :bq] = jnp.concatenate([q0, zeros], axis=0)
    qt_ref[:, bq:] = jnp.concatenate([zeros, q1], axis=0)

    n_chunks = k_ref.shape[0] // NK_B
    back = B_REACH // NK_B
    span = (bq + 2 * B_REACH) // NK_B
    first = i * (bq // NK_B) - back
    interior = jnp.logical_and(first >= 0, first + span <= n_chunks)

    @pl.when(i == 0)
    def _build_bias():
        for hh in range(2):
            for c in range(span):
                seg = (span - 1 - c) * NK_B
                row = frev_ref[hh:hh + 1, seg:seg + 2 * bq]
                rolled = pltpu.roll(jnp.broadcast_to(row, (NK_B, 2 * bq)), 0, 1,
                                    stride=1, stride_axis=0)
                bias_ref[c * NK_B:(c + 1) * NK_B, hh * bq:(hh + 1) * bq] = rolled[:, bq:]

    def write():
        inv = 1.0 / acc_ref[HEAD_DIM:HEAD_DIM + 1, :]
        o_ref[:HEAD_DIM, :] = acc_ref[:HEAD_DIM, :bq] * inv[:, :bq]
        o_ref[HEAD_DIM:, :] = acc_ref[:HEAD_DIM, bq:] * inv[:, bq:]

    def logits(kc, c):
        k_start = pl.multiple_of(kc * NK_B, NK_B)
        b_start = c * NK_B if isinstance(c, int) else pl.multiple_of(c * NK_B, NK_B)
        s = jnp.dot(k_ref[pl.ds(k_start, NK_B), :], qt_ref[...],
                    preferred_element_type=jnp.float32)
        return s + bias_ref[pl.ds(b_start, NK_B), :]

    if bounded:
        stats = stats_ref[...]
        widen = lambda r: jnp.concatenate([stats[r:r + 1, :]] * (bq // stats.shape[1]), axis=1)
        shift = []
        for hh, q in enumerate((q0, q1)):
            qf = q.astype(jnp.float32)
            shift.append(jnp.sqrt(jnp.sum(qf * qf, axis=0, keepdims=True)) * widen(2) + widen(hh))
        shift = jnp.concatenate(shift, axis=1)

        @pl.when(interior)
        def _whole_window():
            s_refs[0][...] = logits(first, 0)
            total = None
            for c in range(span):
                if c + 1 < span:
                    s_refs[(c + 1) % 2][...] = logits(first + c + 1, c + 1)
                pv = _pv_product(_probs(s_refs[c % 2][...], shift), v_ref[first + c])
                total = pv if total is None else total + pv
            acc_ref[...] = total
            write()

        @pl.when(jnp.logical_not(interior))
        def _clipped_window():
            acc_ref[...] = jnp.zeros_like(acc_ref)

            def body(c, carry):
                acc_ref[...] += _pv_product(_probs(logits(first + c, c), shift), v_ref[first + c])
                return carry

            lax.fori_loop(jnp.maximum(0, -first), jnp.minimum(span, n_chunks - first), body, 0)
            write()
    else:
        @pl.when(interior)
        def _whole_window():
            n_ct = 2 * bq // MXU_TILE
            k_start = pl.multiple_of(first * NK_B, NK_B)
            mx = _logits_tiles(k_ref[pl.ds(k_start, span * NK_B), :], qt_ref, bias_ref, s_ref, n_ct)
            v_tiles = [[v_ref[first + c, hh * V_ROWS:(hh + 1) * V_ROWS, :] for hh in range(n_ct)]
                       for c in range(span)]
            _softmax_tiles(s_ref, mx, v_tiles, None, acc_ref)
            write()

        @pl.when(jnp.logical_not(interior))
        def _clipped_window():
            acc_ref[...] = jnp.zeros_like(acc_ref)

            def body(c, m):
                return _softmax_step(logits(first + c, c), v_ref[first + c], m, acc_ref)

            lax.fori_loop(jnp.maximum(0, -first), jnp.minimum(span, n_chunks - first), body,
                          jnp.full((1, 2 * bq), NEG_BIG, jnp.float32))
            write()


def _attn_b_call(qb_t, kb, vb_t, frev, stats, bounded):
    s_len = kb.shape[0]
    bq = BQ_B
    pairs = N_HEADS_B // 2
    span_rows = bq + 2 * B_REACH
    assert bq == NK_B == MXU_TILE and frev.shape[-1] == span_rows + bq
    operands = (qb_t, kb, vb_t, frev) + ((stats,) if bounded else ())
    logits_scratch = ([pltpu.VMEM((NK_B, 2 * bq), jnp.float32)] * 2 if bounded
                      else [pltpu.VMEM((span_rows, 2 * bq), jnp.float32)])
    return pl.pallas_call(
        functools.partial(_attn_b_kernel, bounded=bounded),
        grid=(pairs, s_len // bq),
        in_specs=[
            pl.BlockSpec((2 * HEAD_DIM, bq), lambda j, i: (j, i)),
            pl.BlockSpec((s_len, 2 * HEAD_DIM), lambda j, i: (0, j)),
            pl.BlockSpec((s_len // NK_B, 2 * V_ROWS, NK_B), lambda j, i: (0, j, 0)),
            pl.BlockSpec((None, 2, span_rows + bq), lambda j, i: (j, 0, 0)),
        ] + ([pl.BlockSpec((None, 8, 128), lambda j, i: (j, 0, 0))] if bounded else []),
        out_specs=pl.BlockSpec((2 * HEAD_DIM, bq), lambda j, i: (j, i)),
        out_shape=jax.ShapeDtypeStruct((D_B, s_len), jnp.float32),
        scratch_shapes=[
            pltpu.VMEM((2 * HEAD_DIM, 2 * bq), jnp.bfloat16),
            pltpu.VMEM((V_ROWS, 2 * bq), jnp.float32),
            pltpu.VMEM((span_rows, 2 * bq), jnp.float32),
        ] + logits_scratch,
        compiler_params=pltpu.CompilerParams(
            dimension_semantics=("arbitrary", "arbitrary"), vmem_limit_bytes=VMEM_LIMIT),
        name="attn_b_bounded" if bounded else "attn_b_online",
    )(*operands)


def _attn_b(qb_t, kb, vb_t, bias, q_len2, k_len2):
    frev, bias_max, bias_self = bias
    q_len = jnp.sqrt(jnp.max(q_len2))
    k_len = jnp.sqrt(jnp.max(k_len2))
    span = 2.0 * q_len * k_len + jnp.max(bias_max - bias_self)
    pairs = N_HEADS_B // 2
    stats = jnp.concatenate([bias_max.reshape(pairs, 2), jnp.broadcast_to(k_len, (pairs, 1)),
                             jnp.zeros((pairs, 5), jnp.float32)], axis=1)
    stats = jnp.broadcast_to(stats[:, :, None], (pairs, 8, 128))
    return lax.cond(span <= SHIFT_SPAN_LIMIT,
                    functools.partial(_attn_b_call, bounded=True),
                    functools.partial(_attn_b_call, bounded=False),
                    qb_t, kb, vb_t, frev, stats)


def _t5_bucket_index(rel):
    nb = N_BUCKETS // 2
    max_exact = nb // 2
    side = jnp.where(rel > 0, nb, 0)
    n = jnp.abs(rel)
    large = max_exact + (jnp.log(jnp.maximum(n, max_exact).astype(jnp.float32) / max_exact)
                         / math.log(MAX_DISTANCE / max_exact) * (nb - max_exact)).astype(jnp.int32)
    large = jnp.minimum(large, nb - 1)
    return side + jnp.where(n < max_exact, n, large)


def _dilated_bias_table(rel_bias):
    bq = BQ_B
    reach_all = B_REACH + bq - 1
    delta = jnp.arange(-reach_all, reach_all + 1)
    count = jnp.zeros(delta.shape, jnp.float32)
    for window, dilation in DILATED_PATTERNS:
        inside = (delta % dilation == 0) & (jnp.abs(delta) <= window // 2)
        count = count + inside.astype(jnp.float32)
    table = rel_bias[_t5_bucket_index(delta)].astype(jnp.float32)
    table = jnp.where((count > 0)[:, None],
                      (table + jnp.log(jnp.maximum(count, 1.0))[:, None]) * LOG2E, NEG_BIG)
    frev = jnp.pad(table[::-1].T, ((0, 0), (1, 0)))
    frev = frev.reshape(N_HEADS_B // 2, 2, frev.shape[1])
    bias_max = jnp.max(jnp.where((count > 0)[:, None], table, -jnp.inf), axis=0)
    return frev, bias_max, table[reach_all]


def _out_kernel(ya_ref, yb_ref, x_ref, ga_ref, gb_ref, w_ref, gpost_ref, h_ref):
    def norm_t(y_t, g_col):
        ms = jnp.mean(y_t * y_t, axis=0, keepdims=True)
        return (y_t * lax.rsqrt(ms + EPS) * g_col).astype(jnp.bfloat16)

    y_t = jnp.concatenate([norm_t(ya_ref[...], ga_ref[...]),
                           norm_t(yb_ref[...], gb_ref[...])], axis=0)
    y = lax.dot_general(y_t, w_ref[...], (((0,), (0,)), ((), ())),
                        preferred_element_type=jnp.float32)
    h_ref[...] = x_ref[...] + _rms_rows(y, gpost_ref[...])


def _out_call(ya_t, yb_t, x2, ga, gb, w_out, g_post):
    s_len = x2.shape[0]
    tm = TM_OUT
    const = lambda i: (0, 0)
    return pl.pallas_call(
        _out_kernel,
        grid=(s_len // tm,),
        in_specs=[
            pl.BlockSpec((D_A, tm), lambda i: (0, i)),
            pl.BlockSpec((D_B, tm), lambda i: (0, i)),
            pl.BlockSpec((tm, D_MODEL), lambda i: (i, 0)),
            pl.BlockSpec((D_A, 1), const),
            pl.BlockSpec((D_B, 1), const),
            pl.BlockSpec((D_A + D_B, D_MODEL), const),
            pl.BlockSpec((1, D_MODEL), const),
        ],
        out_specs=pl.BlockSpec((tm, D_MODEL), lambda i: (i, 0)),
        out_shape=jax.ShapeDtypeStruct((s_len, D_MODEL), jnp.float32),
        compiler_params=pltpu.CompilerParams(
            dimension_semantics=("arbitrary",), vmem_limit_bytes=VMEM_LIMIT),
        name="out_proj",
    )(ya_t, yb_t, x2, ga, gb, w_out, g_post)


def _ffn_kernel(h_ref, p_ref, g1_ref, w1_ref, w2_ref, g2_ref, g3_ref, wg_ref, wp_ref, o_ref):
    h = h_ref[...]
    xn = _rms_rows(h, g1_ref[...]).astype(jnp.bfloat16)
    f = None
    for c in range(D_FF // FF_CHUNK):
        cols = slice(c * FF_CHUNK, (c + 1) * FF_CHUNK)
        u = jnp.dot(xn, w1_ref[:, cols], preferred_element_type=jnp.float32)
        u = jnp.square(jnp.maximum(u, 0.0)).astype(jnp.bfloat16)
        part = jnp.dot(u, w2_ref[cols, :], preferred_element_type=jnp.float32)
        f = part if f is None else f + part
    h = h + _rms_rows(f, g2_ref[...])
    gate_in = _rms_rows(h, g3_ref[...]).astype(jnp.bfloat16)
    gate = jax.nn.sigmoid(jnp.dot(gate_in, wg_ref[...], preferred_element_type=jnp.float32))
    emb = jnp.dot(p_ref[...].astype(jnp.bfloat16), wp_ref[...], preferred_element_type=jnp.float32)
    o_ref[...] = h + gate * emb


def _ffn_call(h1, p2, g1, w1, w2, g2, g3, wg, wp):
    s_len = h1.shape[0]
    tm = TM_FFN
    const = lambda i: (0, 0)
    resident = functools.partial(pl.BlockSpec, index_map=const, pipeline_mode=pl.Buffered(1))
    return pl.pallas_call(
        _ffn_kernel,
        grid=(s_len // tm,),
        in_specs=[
            pl.BlockSpec((tm, D_MODEL), lambda i: (i, 0)),
            pl.BlockSpec((tm, D_PLE), lambda i: (i, 0)),
            pl.BlockSpec((1, D_MODEL), const),
            resident((D_MODEL, D_FF)),
            resident((D_FF, D_MODEL)),
            pl.BlockSpec((1, D_MODEL), const),
            pl.BlockSpec((1, D_MODEL), const),
            resident((D_MODEL, D_MODEL)),
            resident((D_PLE, D_MODEL)),
        ],
        out_specs=pl.BlockSpec((tm, D_MODEL), lambda i: (i, 0)),
        out_shape=jax.ShapeDtypeStruct((s_len, D_MODEL), jnp.float32),
        compiler_params=pltpu.CompilerParams(
            dimension_semantics=("arbitrary",), vmem_limit_bytes=VMEM_LIMIT),
        name="ffn_ple",
    )(h1, p2, g1, w1, w2, g2, g3, wg, wp)


def _rope_tables_t(n_tokens):
    tok = jnp.arange(n_tokens)
    row = (tok // GRID_W).astype(jnp.float32)
    col = (tok % GRID_W).astype(jnp.float32)
    n_axis = ROPE_HALF // 2
    inv_freq = ROPE_THETA ** (-jnp.arange(n_axis, dtype=jnp.float32) / n_axis)
    ang = jnp.concatenate([inv_freq[:, None] * row[None, :], inv_freq[:, None] * col[None, :]], axis=0)
    return jnp.cos(ang), jnp.sin(ang)


def _layer(h, p_i, w_in, g_attn_pre, g_q, g_k, g_out_a, g_out_b, w_out, g_attn_post, bias_t,
           g_mlp_pre, w_ff1, w_ff2, g_mlp_post, g_ple, w_ple_gate, w_ple_proj, cos_t, sin_t):
    bf = jnp.bfloat16
    row = lambda g: g.reshape(1, -1)
    col = lambda g: g.reshape(-1, 1)

    qa_t, ka, va_t, qb_t, kb, vb_t, len2 = _proj_call(
        h, row(g_attn_pre), w_in.T.astype(bf), col(g_q), col(g_k), cos_t, sin_t)
    ya_t = _attn_a(qa_t, ka, va_t, len2[0], len2[1])
    yb_t = _attn_b(qb_t, kb, vb_t, bias_t, len2[2], len2[3])
    h1 = _out_call(ya_t, yb_t, h, col(g_out_a), col(g_out_b), w_out.astype(bf), row(g_attn_post))
    return _ffn_call(h1, p_i, row(g_mlp_pre), w_ff1.astype(bf), w_ff2.astype(bf), row(g_mlp_post),
                     row(g_ple), w_ple_gate.astype(bf), w_ple_proj.astype(bf))


def kernel(x, p, w_in, g_attn_pre, g_q, g_k, g_out_a, g_out_b, w_out, g_attn_post, rel_bias,
           g_mlp_pre, w_ff1, w_ff2, g_mlp_post, g_ple, w_ple_gate, w_ple_proj):
    b, s_len, _ = x.shape
    cos_t, sin_t = _rope_tables_t(s_len)
    bias_t = _dilated_bias_table(rel_bias)
    outs = []
    for bi in range(b):
        h = x[bi]
        for i in range(w_in.shape[0]):
            h = _layer(h, p[i, bi], w_in[i], g_attn_pre[i], g_q[i], g_k[i], g_out_a[i], g_out_b[i],
                       w_out[i], g_attn_post[i], bias_t, g_mlp_pre[i], w_ff1[i], w_ff2[i],
                       g_mlp_post[i], g_ple[i], w_ple_gate[i], w_ple_proj[i], cos_t, sin_t)
        outs.append(h)
    return jnp.stack(outs, axis=0)
```

```python
import functools
import math

import jax
import jax.numpy as jnp
from jax import lax
from jax.experimental import pallas as pl
from jax.experimental.pallas import tpu as pltpu

D_MODEL = 1024
HEAD_DIM = 64
N_HEADS_A = 8
N_KV_A = 2
N_HEADS_B = 8
D_A = N_HEADS_A * HEAD_DIM
D_KV_A = N_KV_A * HEAD_DIM
D_B = N_HEADS_B * HEAD_DIM
D_FF = 4 * D_MODEL
D_PLE = 256
GRID_W = 64
ROPE_THETA = 10000.0
ROPE_HALF = HEAD_DIM // 2
DILATED_PATTERNS = ((128, 1), (512, 4), (2048, 16))
N_BUCKETS = 32
MAX_DISTANCE = 1024
EPS = 1e-6
NEG_BIG = -1e30

LOG2E = math.log2(math.e)
Q_SCALE = HEAD_DIM ** -0.5 * LOG2E

MXU_TILE = 256
ONES_ROWS = 16
V_ROWS = HEAD_DIM + ONES_ROWS
VMEM_LIMIT = 56 * 1024 * 1024

TM_PROJ = 512
BQ_A = 128
NK_A = 256
UNROLL_A = 8
UNROLL_A_BOUNDED = 64
SHIFT_SPAN_LIMIT = 96.0
BQ_B = 256
NK_B = 256
B_REACH = 1024
TM_OUT = 512
TM_FFN = 512
FF_CHUNK = 1024

_T_QA, _T_KA, _T_VA = 0, D_A, D_A + D_KV_A
_T_QB, _T_KB, _T_VB = D_A + 2 * D_KV_A, D_A + 2 * D_KV_A + D_B, D_A + 2 * D_KV_A + 2 * D_B
_T_ROWS = D_A + 2 * D_KV_A + 3 * D_B


def _rms_rows(x, g_row):
    ms = jnp.mean(x * x, axis=-1, keepdims=True)
    return x * lax.rsqrt(ms + EPS) * g_row


def _proj_kernel(x_ref, g_ref, wt_ref, gq_ref, gk_ref, cos_ref, sin_ref,
                 qa_ref, ka_ref, va_ref, qb_ref, kb_ref, vb_ref, len_ref):
    xn = _rms_rows(x_ref[...], g_ref[...]).astype(jnp.bfloat16)
    pt = lax.dot_general(wt_ref[...], xn, (((1,), (1,)), ((), ())),
                         preferred_element_type=jnp.float32)

    cos = cos_ref[...]
    sin = sin_ref[...]

    def norm_rope(blk, g_col):
        ms = jnp.mean(blk * blk, axis=0, keepdims=True)
        y = blk * lax.rsqrt(ms + EPS) * g_col
        y1, y2 = y[:ROPE_HALF], y[ROPE_HALF:]
        return jnp.concatenate([y1 * cos - y2 * sin, y2 * cos + y1 * sin], axis=0)

    def sq_norm(v):
        vf = v.astype(jnp.float32)
        return jnp.sum(vf * vf, axis=0, keepdims=True)

    gq = gq_ref[...]
    gk = gk_ref[...]
    q_len2 = None
    for h in range(N_HEADS_A):
        r0 = _T_QA + h * HEAD_DIM
        q = (norm_rope(pt[r0:r0 + HEAD_DIM], gq) * Q_SCALE).astype(jnp.bfloat16)
        qa_ref[h * HEAD_DIM:(h + 1) * HEAD_DIM, :] = q
        q_len2 = sq_norm(q) if q_len2 is None else jnp.maximum(q_len2, sq_norm(q))
    k_heads = [norm_rope(pt[_T_KA + g * HEAD_DIM:_T_KA + (g + 1) * HEAD_DIM], gk).astype(jnp.bfloat16)
               for g in range(N_KV_A)]
    ka_ref[...] = jnp.concatenate([k.astype(jnp.float32) for k in k_heads], axis=0).T.astype(
        jnp.bfloat16)

    def longest2(heads):
        out = sq_norm(heads[0])
        for v in heads[1:]:
            out = jnp.maximum(out, sq_norm(v))
        return out

    qb = (pt[_T_QB:_T_QB + D_B] * Q_SCALE).astype(jnp.bfloat16)
    kb = pt[_T_KB:_T_KB + D_B].astype(jnp.bfloat16)
    qb_ref[...] = qb
    kb_ref[...] = kb.astype(jnp.float32).T.astype(jnp.bfloat16)
    split = lambda v: [v[h * HEAD_DIM:(h + 1) * HEAD_DIM] for h in range(N_HEADS_B)]
    len_ref[0:1, :] = q_len2
    len_ref[1:2, :] = longest2(k_heads)
    len_ref[2:3, :] = longest2(split(qb))
    len_ref[3:4, :] = longest2(split(kb))
    len_ref[4:8, :] = jnp.zeros((4, len_ref.shape[1]), jnp.float32)

    tm = x_ref.shape[0]
    def store_slabs(v_ref, v_t, n_heads, nk):
        for c in range(tm // nk):
            for h in range(n_heads):
                r0 = h * V_ROWS
                v_ref[c, r0:r0 + HEAD_DIM, :] = v_t[h * HEAD_DIM:(h + 1) * HEAD_DIM,
                                                    c * nk:(c + 1) * nk]
                v_ref[c, r0 + HEAD_DIM:r0 + V_ROWS, :] = jnp.ones((ONES_ROWS, nk), jnp.bfloat16)

    store_slabs(va_ref, pt[_T_VA:_T_VA + D_KV_A].astype(jnp.bfloat16), N_KV_A, NK_A)
    store_slabs(vb_ref, pt[_T_VB:_T_VB + D_B].astype(jnp.bfloat16), N_HEADS_B, NK_B)


def _proj_call(x2, g_pre, wt, gq, gk, cos_t, sin_t):
    s_len = x2.shape[0]
    tm = TM_PROJ
    const = lambda i: (0, 0)
    return pl.pallas_call(
        _proj_kernel,
        grid=(s_len // tm,),
        in_specs=[
            pl.BlockSpec((tm, D_MODEL), lambda i: (i, 0)),
            pl.BlockSpec((1, D_MODEL), const),
            pl.BlockSpec((_T_ROWS, D_MODEL), const),
            pl.BlockSpec((HEAD_DIM, 1), const),
            pl.BlockSpec((HEAD_DIM, 1), const),
            pl.BlockSpec((ROPE_HALF, tm), lambda i: (0, i)),
            pl.BlockSpec((ROPE_HALF, tm), lambda i: (0, i)),
        ],
        out_specs=[
            pl.BlockSpec((D_A, tm), lambda i: (0, i)),
            pl.BlockSpec((tm, D_KV_A), lambda i: (i, 0)),
            pl.BlockSpec((tm // NK_A, V_ROWS * N_KV_A, NK_A), lambda i: (i, 0, 0)),
            pl.BlockSpec((D_B, tm), lambda i: (0, i)),
            pl.BlockSpec((tm, D_B), lambda i: (i, 0)),
            pl.BlockSpec((tm // NK_B, V_ROWS * N_HEADS_B, NK_B), lambda i: (i, 0, 0)),
            pl.BlockSpec((8, tm), lambda i: (0, i)),
        ],
        out_shape=[
            jax.ShapeDtypeStruct((D_A, s_len), jnp.bfloat16),
            jax.ShapeDtypeStruct((s_len, D_KV_A), jnp.bfloat16),
            jax.ShapeDtypeStruct((s_len // NK_A, V_ROWS * N_KV_A, NK_A),
                                 jnp.bfloat16),
            jax.ShapeDtypeStruct((D_B, s_len), jnp.bfloat16),
            jax.ShapeDtypeStruct((s_len, D_B), jnp.bfloat16),
            jax.ShapeDtypeStruct((s_len // NK_B, V_ROWS * N_HEADS_B, NK_B),
                                 jnp.bfloat16),
            jax.ShapeDtypeStruct((8, s_len), jnp.float32),
        ],
        compiler_params=pltpu.CompilerParams(
            dimension_semantics=("arbitrary",), vmem_limit_bytes=VMEM_LIMIT),
        name="proj",
    )(x2, g_pre, wt, gq, gk, cos_t, sin_t)


def _probs(s, m):
    return jnp.exp2(s - m).astype(jnp.bfloat16)


def _pv_product(p, v_t):
    groups = v_t.shape[0] // V_ROWS
    width = p.shape[1] // groups
    pv = [jnp.dot(v_t[g * V_ROWS:(g + 1) * V_ROWS], p[:, g * width:(g + 1) * width],
                  preferred_element_type=jnp.float32) for g in range(groups)]
    return jnp.concatenate(pv, axis=1)


def _softmax_step(s, v_t, m, acc_ref):
    m_new = jnp.maximum(m, jnp.max(s, axis=0, keepdims=True))
    acc_ref[...] = acc_ref[...] * jnp.exp2(m - m_new) + _pv_product(_probs(s, m_new), v_t)
    return m_new


def _logits_tiles(k_rows, qt_ref, bias_ref, s_ref, n_ct):
    maxima = []
    for ct in range(n_ct):
        cols = slice(ct * MXU_TILE, (ct + 1) * MXU_TILE)
        s = jnp.dot(k_rows, qt_ref[:, cols], preferred_element_type=jnp.float32)
        if bias_ref is not None:
            s = s + bias_ref[:, cols]
        s_ref[:, cols] = s
        maxima.append(jnp.max(s_ref[:, cols], axis=0, keepdims=True))
    return maxima


def _softmax_tiles(s_ref, mx, v_tiles, m, acc_ref):
    m_out = []
    for ct in range(len(mx)):
        cols = slice(ct * MXU_TILE, (ct + 1) * MXU_TILE)
        pv = None
        m_new = mx[ct] if m is None else jnp.maximum(m[ct], mx[ct])
        for r, v_t in enumerate(v_tiles):
            p = _probs(s_ref[r * MXU_TILE:(r + 1) * MXU_TILE, cols], m_new)
            part = jnp.dot(v_t[ct], p, preferred_element_type=jnp.float32)
            pv = part if pv is None else pv + part
        if m is None:
            acc_ref[:, cols] = pv
        else:
            acc_ref[:, cols] = acc_ref[:, cols] * jnp.exp2(m[ct] - m_new) + pv
        m_out.append(m_new)
    return m_out


def _attn_a_kernel(q_ref, k_ref, v_ref, *refs, bounded):
    if bounded:
        klen_ref, o_ref, qt_ref, acc_ref, *s_refs = refs
    else:
        o_ref, qt_ref, acc_ref, *s_refs = refs
    bq = q_ref.shape[1]
    m_cols = N_HEADS_A * bq
    grp = N_HEADS_A // N_KV_A
    zeros = jnp.zeros((HEAD_DIM, bq), jnp.bfloat16)
    q_len = []
    for h in range(N_HEADS_A):
        qh = q_ref[h * HEAD_DIM:(h + 1) * HEAD_DIM, :]
        parts = [qh if g == h // grp else zeros for g in range(N_KV_A)]
        qt_ref[:, h * bq:(h + 1) * bq] = jnp.concatenate(parts, axis=0)
        qf = qh.astype(jnp.float32)
        q_len.append(jnp.sqrt(jnp.sum(qf * qf, axis=0, keepdims=True)))
    acc_ref[...] = jnp.zeros_like(acc_ref)
    n_chunks = k_ref.shape[0] // NK_A
    unroll = UNROLL_A_BOUNDED if bounded else UNROLL_A

    def logits(c):
        start = pl.multiple_of(c * NK_A, NK_A)
        return jnp.dot(k_ref[pl.ds(start, NK_A), :], qt_ref[...],
                       preferred_element_type=jnp.float32)

    s_refs[0][...] = logits(0)

    if bounded:
        shift = jnp.concatenate(q_len, axis=1) * klen_ref[...]

        def body(t, carry):
            c0 = t * unroll
            total = None
            for u in range(unroll):
                nxt = jnp.minimum(c0 + u + 1, n_chunks - 1)
                s_refs[(u + 1) % 2][...] = logits(nxt)
                pv = _pv_product(_probs(s_refs[u % 2][...], shift), v_ref[c0 + u])
                total = pv if total is None else total + pv
            acc_ref[...] += total
            return carry

        lax.fori_loop(0, n_chunks // unroll, body, 0)
    else:
        def body(t, m):
            c0 = t * unroll
            for u in range(unroll):
                nxt = jnp.minimum(c0 + u + 1, n_chunks - 1)
                s_refs[(u + 1) % 2][...] = logits(nxt)
                m = _softmax_step(s_refs[u % 2][...], v_ref[c0 + u], m, acc_ref)
            return m

        lax.fori_loop(0, n_chunks // unroll, body, jnp.full((1, m_cols), NEG_BIG, jnp.float32))
    inv = 1.0 / acc_ref[HEAD_DIM:HEAD_DIM + 1, :]
    for h in range(N_HEADS_A):
        cols = slice(h * bq, (h + 1) * bq)
        o_ref[h * HEAD_DIM:(h + 1) * HEAD_DIM, :] = acc_ref[:HEAD_DIM, cols] * inv[:, cols]


def _attn_a_call(qa_t, ka, va_t, k_len, bounded):
    s_len = ka.shape[0]
    bq = BQ_A
    operands = (qa_t, ka, va_t) + ((k_len,) if bounded else ())
    return pl.pallas_call(
        functools.partial(_attn_a_kernel, bounded=bounded),
        grid=(s_len // bq,),
        in_specs=[
            pl.BlockSpec((D_A, bq), lambda i: (0, i)),
            pl.BlockSpec((s_len, D_KV_A), lambda i: (0, 0)),
            pl.BlockSpec((s_len // NK_A, V_ROWS * N_KV_A, NK_A), lambda i: (0, 0, 0)),
        ] + ([pl.BlockSpec((1, 1), lambda i: (0, 0))] if bounded else []),
        out_specs=pl.BlockSpec((D_A, bq), lambda i: (0, i)),
        out_shape=jax.ShapeDtypeStruct((D_A, s_len), jnp.float32),
        scratch_shapes=[
            pltpu.VMEM((D_KV_A, N_HEADS_A * bq), jnp.bfloat16),
            pltpu.VMEM((V_ROWS, N_HEADS_A * bq), jnp.float32),
        ] + [pltpu.VMEM((NK_A, N_HEADS_A * bq), jnp.float32)] * 2,
        compiler_params=pltpu.CompilerParams(
            dimension_semantics=("arbitrary",), vmem_limit_bytes=VMEM_LIMIT),
        name="attn_a_bounded" if bounded else "attn_a_online",
    )(*operands)


def _attn_a(qa_t, ka, va_t, q_len2, k_len2):
    q_len = jnp.sqrt(jnp.max(q_len2))
    k_len = jnp.sqrt(jnp.max(k_len2))
    k_len_arr = k_len.reshape(1, 1)
    return lax.cond(2.0 * q_len * k_len <= SHIFT_SPAN_LIMIT,
                    functools.partial(_attn_a_call, bounded=True),
                    functools.partial(_attn_a_call, bounded=False),
                    qa_t, ka, va_t, k_len_arr)


def _attn_b_kernel(q_ref, k_ref, v_ref, frev_ref, *refs, bounded):
    if bounded:
        stats_ref, o_ref, qt_ref, acc_ref, bias_ref, *s_refs = refs
    else:
        o_ref, qt_ref, acc_ref, bias_ref, s_ref = refs
    i = pl.program_id(1)
    bq = q_ref.shape[1]
    def stage_queries():
        zeros = jnp.zeros((HEAD_DIM, bq), jnp.bfloat16)
        q0 = q_ref[:HEAD_DIM, :]
        q1 = q_ref[HEAD_DIM:, :]
        qt_ref[:, :bq] = jnp.concatenate([q0, zeros], axis=0)
        qt_ref[:, bq:] = jnp.concatenate([zeros, q1], axis=0)
        return q0, q1

    n_chunks = k_ref.shape[0] // NK_B
    back = B_REACH // NK_B
    span = (bq + 2 * B_REACH) // NK_B
    first = i * (bq // NK_B) - back
    interior = jnp.logical_and(first >= 0, first + span <= n_chunks)

    @pl.when(i == 0)
    def _build_bias():
        for hh in range(2):
            for c in range(span):
                seg = (span - 1 - c) * NK_B
                row = frev_ref[hh:hh + 1, seg:seg + 2 * bq]
                rolled = pltpu.roll(jnp.broadcast_to(row, (NK_B, 2 * bq)), 0, 1,
                                    stride=1, stride_axis=0)
                bias_ref[c * NK_B:(c + 1) * NK_B, hh * bq:(hh + 1) * bq] = rolled[:, bq:]

    def write():
        inv = 1.0 / acc_ref[HEAD_DIM:HEAD_DIM + 1, :]
        o_ref[:HEAD_DIM, :] = acc_ref[:HEAD_DIM, :bq] * inv[:, :bq]
        o_ref[HEAD_DIM:, :] = acc_ref[:HEAD_DIM, bq:] * inv[:, bq:]

    def logits(kc, c):
        k_start = pl.multiple_of(kc * NK_B, NK_B)
        b_start = c * NK_B if isinstance(c, int) else pl.multiple_of(c * NK_B, NK_B)
        s = jnp.dot(k_ref[pl.ds(k_start, NK_B), :], qt_ref[...],
                    preferred_element_type=jnp.float32)
        return s + bias_ref[pl.ds(b_start, NK_B), :]

    if bounded:
        def stage_and_shift():
            stats = stats_ref[...]
            widen = lambda r: jnp.concatenate([stats[r:r + 1, :]] * (bq // stats.shape[1]), axis=1)
            shift = []
            for hh, q in enumerate(stage_queries()):
                qf = q.astype(jnp.float32)
                shift.append(jnp.sqrt(jnp.sum(qf * qf, axis=0, keepdims=True)) * widen(2) + widen(hh))
            return jnp.concatenate(shift, axis=1)

        @pl.when(interior)
        def _whole_window():
            shift = stage_and_shift()
            s_refs[0][...] = logits(first, 0)
            total = None
            for c in range(span):
                if c + 1 < span:
                    s_refs[(c + 1) % 2][...] = logits(first + c + 1, c + 1)
                pv = _pv_product(_probs(s_refs[c % 2][...], shift), v_ref[first + c])
                total = pv if total is None else total + pv
            acc_ref[...] = total
            write()

        @pl.when(jnp.logical_not(interior))
        def _clipped_window():
            acc_ref[...] = jnp.zeros_like(acc_ref)
            shift = stage_and_shift()

            def body(c, carry):
                acc_ref[...] += _pv_product(_probs(logits(first + c, c), shift), v_ref[first + c])
                return carry

            lax.fori_loop(jnp.maximum(0, -first), jnp.minimum(span, n_chunks - first), body, 0)
            write()
    else:
        @pl.when(interior)
        def _whole_window():
            stage_queries()
            n_ct = 2 * bq // MXU_TILE
            k_start = pl.multiple_of(first * NK_B, NK_B)
            mx = _logits_tiles(k_ref[pl.ds(k_start, span * NK_B), :], qt_ref, bias_ref, s_ref, n_ct)
            v_tiles = [[v_ref[first + c, hh * V_ROWS:(hh + 1) * V_ROWS, :] for hh in range(n_ct)]
                       for c in range(span)]
            _softmax_tiles(s_ref, mx, v_tiles, None, acc_ref)
            write()

        @pl.when(jnp.logical_not(interior))
        def _clipped_window():
            acc_ref[...] = jnp.zeros_like(acc_ref)
            stage_queries()

            def body(c, m):
                return _softmax_step(logits(first + c, c), v_ref[first + c], m, acc_ref)

            lax.fori_loop(jnp.maximum(0, -first), jnp.minimum(span, n_chunks - first), body,
                          jnp.full((1, 2 * bq), NEG_BIG, jnp.float32))
            write()


def _attn_b_call(qb_t, kb, vb_t, frev, stats, bounded):
    s_len = kb.shape[0]
    bq = BQ_B
    pairs = N_HEADS_B // 2
    span_rows = bq + 2 * B_REACH
    assert bq == NK_B == MXU_TILE and frev.shape[-1] == span_rows + bq
    operands = (qb_t, kb, vb_t, frev) + ((stats,) if bounded else ())
    logits_scratch = ([pltpu.VMEM((NK_B, 2 * bq), jnp.float32)] * 2 if bounded
                      else [pltpu.VMEM((span_rows, 2 * bq), jnp.float32)])
    return pl.pallas_call(
        functools.partial(_attn_b_kernel, bounded=bounded),
        grid=(pairs, s_len // bq),
        in_specs=[
            pl.BlockSpec((2 * HEAD_DIM, bq), lambda j, i: (j, i)),
            pl.BlockSpec((s_len, 2 * HEAD_DIM), lambda j, i: (0, j)),
            pl.BlockSpec((s_len // NK_B, 2 * V_ROWS, NK_B), lambda j, i: (0, j, 0)),
            pl.BlockSpec((None, 2, span_rows + bq), lambda j, i: (j, 0, 0)),
        ] + ([pl.BlockSpec((None, 8, 128), lambda j, i: (j, 0, 0))] if bounded else []),
        out_specs=pl.BlockSpec((2 * HEAD_DIM, bq), lambda j, i: (j, i)),
        out_shape=jax.ShapeDtypeStruct((D_B, s_len), jnp.float32),
        scratch_shapes=[
            pltpu.VMEM((2 * HEAD_DIM, 2 * bq), jnp.bfloat16),
            pltpu.VMEM((V_ROWS, 2 * bq), jnp.float32),
            pltpu.VMEM((span_rows, 2 * bq), jnp.float32),
        ] + logits_scratch,
        compiler_params=pltpu.CompilerParams(
            dimension_semantics=("arbitrary", "arbitrary"), vmem_limit_bytes=VMEM_LIMIT),
        name="attn_b_bounded" if bounded else "attn_b_online",
    )(*operands)


def _attn_b(qb_t, kb, vb_t, bias, q_len2, k_len2):
    frev, bias_max, bias_self = bias
    q_len = jnp.sqrt(jnp.max(q_len2))
    k_len = jnp.sqrt(jnp.max(k_len2))
    span = 2.0 * q_len * k_len + jnp.max(bias_max - bias_self)
    pairs = N_HEADS_B // 2
    stats = jnp.concatenate([bias_max.reshape(pairs, 2), jnp.broadcast_to(k_len, (pairs, 1)),
                             jnp.zeros((pairs, 5), jnp.float32)], axis=1)
    stats = jnp.broadcast_to(stats[:, :, None], (pairs, 8, 128))
    return lax.cond(span <= SHIFT_SPAN_LIMIT,
                    functools.partial(_attn_b_call, bounded=True),
                    functools.partial(_attn_b_call, bounded=False),
                    qb_t, kb, vb_t, frev, stats)


def _t5_bucket_index(rel):
    nb = N_BUCKETS // 2
    max_exact = nb // 2
    side = jnp.where(rel > 0, nb, 0)
    n = jnp.abs(rel)
    large = max_exact + (jnp.log(jnp.maximum(n, max_exact).astype(jnp.float32) / max_exact)
                         / math.log(MAX_DISTANCE / max_exact) * (nb - max_exact)).astype(jnp.int32)
    large = jnp.minimum(large, nb - 1)
    return side + jnp.where(n < max_exact, n, large)


def _dilated_bias_table(rel_bias):
    bq = BQ_B
    reach_all = B_REACH + bq - 1
    delta = jnp.arange(reach_all, -reach_all - 1, -1)
    count = jnp.zeros(delta.shape, jnp.float32)
    for window, dilation in DILATED_PATTERNS:
        inside = (delta % dilation == 0) & (jnp.abs(delta) <= window // 2)
        count = count + inside.astype(jnp.float32)
    table = rel_bias[_t5_bucket_index(delta)].astype(jnp.float32)
    table = jnp.where((count > 0)[:, None],
                      (table + jnp.log(jnp.maximum(count, 1.0))[:, None]) * LOG2E, NEG_BIG)
    frev = jnp.pad(table.T, ((0, 0), (1, 0)))
    frev = frev.reshape(N_HEADS_B // 2, 2, frev.shape[1])
    bias_max = jnp.max(jnp.where((count > 0)[:, None], table, -jnp.inf), axis=0)
    return frev, bias_max, table[reach_all]


def _out_kernel(ya_ref, yb_ref, x_ref, ga_ref, gb_ref, w_ref, gpost_ref, h_ref):
    def norm_t(y_t, g_col):
        ms = jnp.mean(y_t * y_t, axis=0, keepdims=True)
        return (y_t * lax.rsqrt(ms + EPS) * g_col).astype(jnp.bfloat16)

    y_t = jnp.concatenate([norm_t(ya_ref[...], ga_ref[...]),
                           norm_t(yb_ref[...], gb_ref[...])], axis=0)
    y = lax.dot_general(y_t, w_ref[...], (((0,), (0,)), ((), ())),
                        preferred_element_type=jnp.float32)
    h_ref[...] = x_ref[...] + _rms_rows(y, gpost_ref[...])


def _out_call(ya_t, yb_t, x2, ga, gb, w_out, g_post):
    s_len = x2.shape[0]
    tm = TM_OUT
    const = lambda i: (0, 0)
    return pl.pallas_call(
        _out_kernel,
        grid=(s_len // tm,),
        in_specs=[
            pl.BlockSpec((D_A, tm), lambda i: (0, i)),
            pl.BlockSpec((D_B, tm), lambda i: (0, i)),
            pl.BlockSpec((tm, D_MODEL), lambda i: (i, 0)),
            pl.BlockSpec((D_A, 1), const),
            pl.BlockSpec((D_B, 1), const),
            pl.BlockSpec((D_A + D_B, D_MODEL), const),
            pl.BlockSpec((1, D_MODEL), const),
        ],
        out_specs=pl.BlockSpec((tm, D_MODEL), lambda i: (i, 0)),
        out_shape=jax.ShapeDtypeStruct((s_len, D_MODEL), jnp.float32),
        compiler_params=pltpu.CompilerParams(
            dimension_semantics=("arbitrary",), vmem_limit_bytes=VMEM_LIMIT),
        name="out_proj",
    )(ya_t, yb_t, x2, ga, gb, w_out, g_post)


def _ffn_kernel(h_ref, p_ref, g1_ref, w1_ref, w2_ref, g2_ref, g3_ref, wg_ref, wp_ref, o_ref):
    h = h_ref[...]
    xn = _rms_rows(h, g1_ref[...]).astype(jnp.bfloat16)
    f = None
    for c in range(D_FF // FF_CHUNK):
        cols = slice(c * FF_CHUNK, (c + 1) * FF_CHUNK)
        u = jnp.dot(xn, w1_ref[:, cols], preferred_element_type=jnp.float32)
        u = jnp.square(jnp.maximum(u, 0.0)).astype(jnp.bfloat16)
        part = jnp.dot(u, w2_ref[cols, :], preferred_element_type=jnp.float32)
        f = part if f is None else f + part
    h = h + _rms_rows(f, g2_ref[...])
    gate_in = _rms_rows(h, g3_ref[...]).astype(jnp.bfloat16)
    gate = jax.nn.sigmoid(jnp.dot(gate_in, wg_ref[...], preferred_element_type=jnp.float32))
    emb = jnp.dot(p_ref[...].astype(jnp.bfloat16), wp_ref[...], preferred_element_type=jnp.float32)
    o_ref[...] = h + gate * emb


def _ffn_call(h1, p2, g1, w1, w2, g2, g3, wg, wp):
    s_len = h1.shape[0]
    tm = TM_FFN
    const = lambda i: (0, 0)
    resident = functools.partial(pl.BlockSpec, index_map=const, pipeline_mode=pl.Buffered(1))
    return pl.pallas_call(
        _ffn_kernel,
        grid=(s_len // tm,),
        in_specs=[
            pl.BlockSpec((tm, D_MODEL), lambda i: (i, 0)),
            pl.BlockSpec((tm, D_PLE), lambda i: (i, 0)),
            pl.BlockSpec((1, D_MODEL), const),
            resident((D_MODEL, D_FF)),
            resident((D_FF, D_MODEL)),
            pl.BlockSpec((1, D_MODEL), const),
            pl.BlockSpec((1, D_MODEL), const),
            resident((D_MODEL, D_MODEL)),
            resident((D_PLE, D_MODEL)),
        ],
        out_specs=pl.BlockSpec((tm, D_MODEL), lambda i: (i, 0)),
        out_shape=jax.ShapeDtypeStruct((s_len, D_MODEL), jnp.float32),
        compiler_params=pltpu.CompilerParams(
            dimension_semantics=("arbitrary",), vmem_limit_bytes=VMEM_LIMIT),
        name="ffn_ple",
    )(h1, p2, g1, w1, w2, g2, g3, wg, wp)


def _rope_tables_t(n_tokens):
    tok = jnp.arange(n_tokens)
    row = (tok // GRID_W).astype(jnp.float32)
    col = (tok % GRID_W).astype(jnp.float32)
    n_axis = ROPE_HALF // 2
    inv_freq = ROPE_THETA ** (-jnp.arange(n_axis, dtype=jnp.float32) / n_axis)
    ang = jnp.concatenate([inv_freq[:, None] * row[None, :], inv_freq[:, None] * col[None, :]], axis=0)
    return jnp.cos(ang), jnp.sin(ang)


def _layer(h, p_i, w_in, g_attn_pre, g_q, g_k, g_out_a, g_out_b, w_out, g_attn_post, bias_t,
           g_mlp_pre, w_ff1, w_ff2, g_mlp_post, g_ple, w_ple_gate, w_ple_proj, cos_t, sin_t):
    bf = jnp.bfloat16
    row = lambda g: g.reshape(1, -1)
    col = lambda g: g.reshape(-1, 1)

    qa_t, ka, va_t, qb_t, kb, vb_t, len2 = _proj_call(
        h, row(g_attn_pre), w_in.T.astype(bf), col(g_q), col(g_k), cos_t, sin_t)
    ya_t = _attn_a(qa_t, ka, va_t, len2[0], len2[1])
    yb_t = _attn_b(qb_t, kb, vb_t, bias_t, len2[2], len2[3])
    h1 = _out_call(ya_t, yb_t, h, col(g_out_a), col(g_out_b), w_out.astype(bf), row(g_attn_post))
    return _ffn_call(h1, p_i, row(g_mlp_pre), w_ff1.astype(bf), w_ff2.astype(bf), row(g_mlp_post),
                     row(g_ple), w_ple_gate.astype(bf), w_ple_proj.astype(bf))


def kernel(x, p, w_in, g_attn_pre, g_q, g_k, g_out_a, g_out_b, w_out, g_attn_post, rel_bias,
           g_mlp_pre, w_ff1, w_ff2, g_mlp_post, g_ple, w_ple_gate, w_ple_proj):
    b, s_len, _ = x.shape
    cos_t, sin_t = _rope_tables_t(s_len)
    bias_t = _dilated_bias_table(rel_bias)
    outs = []
    for bi in range(b):
        h = x[bi]
        for i in range(w_in.shape[0]):
            h = _layer(h, p[i, bi], w_in[i], g_attn_pre[i], g_q[i], g_k[i], g_out_a[i], g_out_b[i],
                       w_out[i], g_attn_post[i], bias_t, g_mlp_pre[i], w_ff1[i], w_ff2[i],
                       g_mlp_post[i], g_ple[i], w_ple_gate[i], w_ple_proj[i], cos_t, sin_t)
        outs.append(h)
    return jnp.stack(outs, axis=0)
```

```python
import functools
import math

import jax
import jax.numpy as jnp
from jax import lax
from jax.experimental import pallas as pl
from jax.experimental.pallas import tpu as pltpu

D_MODEL = 1024
HEAD_DIM = 64
N_HEADS_A = 8
N_KV_A = 2
N_HEADS_B = 8
D_A = N_HEADS_A * HEAD_DIM
D_KV_A = N_KV_A * HEAD_DIM
D_B = N_HEADS_B * HEAD_DIM
D_FF = 4 * D_MODEL
D_PLE = 256
GRID_W = 64
ROPE_THETA = 10000.0
ROPE_HALF = HEAD_DIM // 2
DILATED_PATTERNS = ((128, 1), (512, 4), (2048, 16))
N_BUCKETS = 32
MAX_DISTANCE = 1024
EPS = 1e-6
NEG_BIG = -1e30

LOG2E = math.log2(math.e)
Q_SCALE = HEAD_DIM ** -0.5 * LOG2E

MXU_TILE = 256
ONES_ROWS = 16
V_ROWS = HEAD_DIM + ONES_ROWS
VMEM_LIMIT = 56 * 1024 * 1024

TM_PROJ = 512
BQ_A = 128
NK_A = 256
UNROLL_A = 8
UNROLL_A_BOUNDED = 64
SHIFT_SPAN_LIMIT = 96.0
BQ_B = 256
NK_B = 256
QB_B = 4
B_REACH = 1024
TM_OUT = 512
TM_FFN = 512
FF_CHUNK = 1024

_T_QA, _T_KA, _T_VA = 0, D_A, D_A + D_KV_A
_T_QB, _T_KB, _T_VB = D_A + 2 * D_KV_A, D_A + 2 * D_KV_A + D_B, D_A + 2 * D_KV_A + 2 * D_B
_T_ROWS = D_A + 2 * D_KV_A + 3 * D_B


def _rms_rows(x, g_row):
    ms = jnp.mean(x * x, axis=-1, keepdims=True)
    return x * lax.rsqrt(ms + EPS) * g_row


def _proj_kernel(x_ref, g_ref, wt_ref, gq_ref, gk_ref, cos_ref, sin_ref,
                 qa_ref, ka_ref, va_ref, qb_ref, kb_ref, vb_ref, len_ref):
    xn = _rms_rows(x_ref[...], g_ref[...]).astype(jnp.bfloat16)
    pt = lax.dot_general(wt_ref[...], xn, (((1,), (1,)), ((), ())),
                         preferred_element_type=jnp.float32)

    cos = cos_ref[...]
    sin = sin_ref[...]

    def norm_rope(blk, g_col):
        ms = jnp.mean(blk * blk, axis=0, keepdims=True)
        y = blk * lax.rsqrt(ms + EPS) * g_col
        y1, y2 = y[:ROPE_HALF], y[ROPE_HALF:]
        return jnp.concatenate([y1 * cos - y2 * sin, y2 * cos + y1 * sin], axis=0)

    def sq_norm(v):
        vf = v.astype(jnp.float32)
        return jnp.sum(vf * vf, axis=0, keepdims=True)

    gq = gq_ref[...]
    gk = gk_ref[...]
    q_len2 = None
    for h in range(N_HEADS_A):
        r0 = _T_QA + h * HEAD_DIM
        q = (norm_rope(pt[r0:r0 + HEAD_DIM], gq) * Q_SCALE).astype(jnp.bfloat16)
        qa_ref[h * HEAD_DIM:(h + 1) * HEAD_DIM, :] = q
        q_len2 = sq_norm(q) if q_len2 is None else jnp.maximum(q_len2, sq_norm(q))
    k_heads = [norm_rope(pt[_T_KA + g * HEAD_DIM:_T_KA + (g + 1) * HEAD_DIM], gk).astype(jnp.bfloat16)
               for g in range(N_KV_A)]
    ka_ref[...] = jnp.concatenate([k.astype(jnp.float32) for k in k_heads], axis=0).T.astype(
        jnp.bfloat16)

    def longest2(heads):
        out = sq_norm(heads[0])
        for v in heads[1:]:
            out = jnp.maximum(out, sq_norm(v))
        return out

    qb = (pt[_T_QB:_T_QB + D_B] * Q_SCALE).astype(jnp.bfloat16)
    kb = pt[_T_KB:_T_KB + D_B].astype(jnp.bfloat16)
    qb_ref[...] = qb
    kb_ref[...] = kb.astype(jnp.float32).T.astype(jnp.bfloat16)
    split = lambda v: [v[h * HEAD_DIM:(h + 1) * HEAD_DIM] for h in range(N_HEADS_B)]
    len_ref[0:1, :] = q_len2
    len_ref[1:2, :] = longest2(k_heads)
    len_ref[2:3, :] = longest2(split(qb))
    len_ref[3:4, :] = longest2(split(kb))
    len_ref[4:8, :] = jnp.zeros((4, len_ref.shape[1]), jnp.float32)

    tm = x_ref.shape[0]
    def store_slabs(v_ref, v_t, n_heads, nk):
        for c in range(tm // nk):
            for h in range(n_heads):
                r0 = h * V_ROWS
                v_ref[c, r0:r0 + HEAD_DIM, :] = v_t[h * HEAD_DIM:(h + 1) * HEAD_DIM,
                                                    c * nk:(c + 1) * nk]
                v_ref[c, r0 + HEAD_DIM:r0 + V_ROWS, :] = jnp.ones((ONES_ROWS, nk), jnp.bfloat16)

    store_slabs(va_ref, pt[_T_VA:_T_VA + D_KV_A].astype(jnp.bfloat16), N_KV_A, NK_A)
    store_slabs(vb_ref, pt[_T_VB:_T_VB + D_B].astype(jnp.bfloat16), N_HEADS_B, NK_B)


def _proj_call(x2, g_pre, wt, gq, gk, cos_t, sin_t):
    s_len = x2.shape[0]
    tm = TM_PROJ
    const = lambda i: (0, 0)
    return pl.pallas_call(
        _proj_kernel,
        grid=(s_len // tm,),
        in_specs=[
            pl.BlockSpec((tm, D_MODEL), lambda i: (i, 0)),
            pl.BlockSpec((1, D_MODEL), const),
            pl.BlockSpec((_T_ROWS, D_MODEL), const),
            pl.BlockSpec((HEAD_DIM, 1), const),
            pl.BlockSpec((HEAD_DIM, 1), const),
            pl.BlockSpec((ROPE_HALF, tm), lambda i: (0, i)),
            pl.BlockSpec((ROPE_HALF, tm), lambda i: (0, i)),
        ],
        out_specs=[
            pl.BlockSpec((D_A, tm), lambda i: (0, i)),
            pl.BlockSpec((tm, D_KV_A), lambda i: (i, 0)),
            pl.BlockSpec((tm // NK_A, V_ROWS * N_KV_A, NK_A), lambda i: (i, 0, 0)),
            pl.BlockSpec((D_B, tm), lambda i: (0, i)),
            pl.BlockSpec((tm, D_B), lambda i: (i, 0)),
            pl.BlockSpec((tm // NK_B, V_ROWS * N_HEADS_B, NK_B), lambda i: (i, 0, 0)),
            pl.BlockSpec((8, tm), lambda i: (0, i)),
        ],
        out_shape=[
            jax.ShapeDtypeStruct((D_A, s_len), jnp.bfloat16),
            jax.ShapeDtypeStruct((s_len, D_KV_A), jnp.bfloat16),
            jax.ShapeDtypeStruct((s_len // NK_A, V_ROWS * N_KV_A, NK_A),
                                 jnp.bfloat16),
            jax.ShapeDtypeStruct((D_B, s_len), jnp.bfloat16),
            jax.ShapeDtypeStruct((s_len, D_B), jnp.bfloat16),
            jax.ShapeDtypeStruct((s_len // NK_B, V_ROWS * N_HEADS_B, NK_B),
                                 jnp.bfloat16),
            jax.ShapeDtypeStruct((8, s_len), jnp.float32),
        ],
        compiler_params=pltpu.CompilerParams(
            dimension_semantics=("arbitrary",), vmem_limit_bytes=VMEM_LIMIT),
        name="proj",
    )(x2, g_pre, wt, gq, gk, cos_t, sin_t)


def _probs(s, m):
    return jnp.exp2(s - m).astype(jnp.bfloat16)


def _pv_product(p, v_t):
    groups = v_t.shape[0] // V_ROWS
    width = p.shape[1] // groups
    pv = [jnp.dot(v_t[g * V_ROWS:(g + 1) * V_ROWS], p[:, g * width:(g + 1) * width],
                  preferred_element_type=jnp.float32) for g in range(groups)]
    return jnp.concatenate(pv, axis=1)


def _softmax_step(s, v_t, m, acc_ref):
    m_new = jnp.maximum(m, jnp.max(s, axis=0, keepdims=True))
    acc_ref[...] = acc_ref[...] * jnp.exp2(m - m_new) + _pv_product(_probs(s, m_new), v_t)
    return m_new


def _logits_tiles(k_rows, qt_ref, bias_ref, s_ref, n_ct):
    maxima = []
    for ct in range(n_ct):
        cols = slice(ct * MXU_TILE, (ct + 1) * MXU_TILE)
        s = jnp.dot(k_rows, qt_ref[:, cols], preferred_element_type=jnp.float32)
        if bias_ref is not None:
            s = s + bias_ref[:, cols]
        s_ref[:, cols] = s
        maxima.append(jnp.max(s_ref[:, cols], axis=0, keepdims=True))
    return maxima


def _softmax_tiles(s_ref, mx, v_tiles, m, acc_ref):
    m_out = []
    for ct in range(len(mx)):
        cols = slice(ct * MXU_TILE, (ct + 1) * MXU_TILE)
        pv = None
        m_new = mx[ct] if m is None else jnp.maximum(m[ct], mx[ct])
        for r, v_t in enumerate(v_tiles):
            p = _probs(s_ref[r * MXU_TILE:(r + 1) * MXU_TILE, cols], m_new)
            part = jnp.dot(v_t[ct], p, preferred_element_type=jnp.float32)
            pv = part if pv is None else pv + part
        if m is None:
            acc_ref[:, cols] = pv
        else:
            acc_ref[:, cols] = acc_ref[:, cols] * jnp.exp2(m[ct] - m_new) + pv
        m_out.append(m_new)
    return m_out


def _attn_a_kernel(q_ref, k_ref, v_ref, *refs, bounded):
    if bounded:
        klen_ref, o_ref, qt_ref, acc_ref, *s_refs = refs
    else:
        o_ref, qt_ref, acc_ref, *s_refs = refs
    bq = q_ref.shape[1]
    m_cols = N_HEADS_A * bq
    grp = N_HEADS_A // N_KV_A
    zeros = jnp.zeros((HEAD_DIM, bq), jnp.bfloat16)
    q_len = []
    for h in range(N_HEADS_A):
        qh = q_ref[h * HEAD_DIM:(h + 1) * HEAD_DIM, :]
        parts = [qh if g == h // grp else zeros for g in range(N_KV_A)]
        qt_ref[:, h * bq:(h + 1) * bq] = jnp.concatenate(parts, axis=0)
        qf = qh.astype(jnp.float32)
        q_len.append(jnp.sqrt(jnp.sum(qf * qf, axis=0, keepdims=True)))
    acc_ref[...] = jnp.zeros_like(acc_ref)
    n_chunks = k_ref.shape[0] // NK_A
    unroll = UNROLL_A_BOUNDED if bounded else UNROLL_A

    def logits(c):
        start = pl.multiple_of(c * NK_A, NK_A)
        return jnp.dot(k_ref[pl.ds(start, NK_A), :], qt_ref[...],
                       preferred_element_type=jnp.float32)

    s_refs[0][...] = logits(0)

    if bounded:
        shift = jnp.concatenate(q_len, axis=1) * klen_ref[...]

        def body(t, carry):
            c0 = t * unroll
            total = None
            for u in range(unroll):
                nxt = jnp.minimum(c0 + u + 1, n_chunks - 1)
                s_refs[(u + 1) % 2][...] = logits(nxt)
                pv = _pv_product(_probs(s_refs[u % 2][...], shift), v_ref[c0 + u])
                total = pv if total is None else total + pv
            acc_ref[...] += total
            return carry

        lax.fori_loop(0, n_chunks // unroll, body, 0)
    else:
        def body(t, m):
            c0 = t * unroll
            for u in range(unroll):
                nxt = jnp.minimum(c0 + u + 1, n_chunks - 1)
                s_refs[(u + 1) % 2][...] = logits(nxt)
                m = _softmax_step(s_refs[u % 2][...], v_ref[c0 + u], m, acc_ref)
            return m

        lax.fori_loop(0, n_chunks // unroll, body, jnp.full((1, m_cols), NEG_BIG, jnp.float32))
    inv = 1.0 / acc_ref[HEAD_DIM:HEAD_DIM + 1, :]
    for h in range(N_HEADS_A):
        cols = slice(h * bq, (h + 1) * bq)
        o_ref[h * HEAD_DIM:(h + 1) * HEAD_DIM, :] = acc_ref[:HEAD_DIM, cols] * inv[:, cols]


def _attn_a_call(qa_t, ka, va_t, k_len, bounded):
    s_len = ka.shape[0]
    bq = BQ_A
    operands = (qa_t, ka, va_t) + ((k_len,) if bounded else ())
    return pl.pallas_call(
        functools.partial(_attn_a_kernel, bounded=bounded),
        grid=(s_len // bq,),
        in_specs=[
            pl.BlockSpec((D_A, bq), lambda i: (0, i)),
            pl.BlockSpec((s_len, D_KV_A), lambda i: (0, 0)),
            pl.BlockSpec((s_len // NK_A, V_ROWS * N_KV_A, NK_A), lambda i: (0, 0, 0)),
        ] + ([pl.BlockSpec((1, 1), lambda i: (0, 0))] if bounded else []),
        out_specs=pl.BlockSpec((D_A, bq), lambda i: (0, i)),
        out_shape=jax.ShapeDtypeStruct((D_A, s_len), jnp.float32),
        scratch_shapes=[
            pltpu.VMEM((D_KV_A, N_HEADS_A * bq), jnp.bfloat16),
            pltpu.VMEM((V_ROWS, N_HEADS_A * bq), jnp.float32),
        ] + [pltpu.VMEM((NK_A, N_HEADS_A * bq), jnp.float32)] * 2,
        compiler_params=pltpu.CompilerParams(
            dimension_semantics=("arbitrary",), vmem_limit_bytes=VMEM_LIMIT),
        name="attn_a_bounded" if bounded else "attn_a_online",
    )(*operands)


def _attn_a(qa_t, ka, va_t, q_len2, k_len2):
    q_len = jnp.sqrt(jnp.max(q_len2))
    k_len = jnp.sqrt(jnp.max(k_len2))
    k_len_arr = k_len.reshape(1, 1)
    return lax.cond(2.0 * q_len * k_len <= SHIFT_SPAN_LIMIT,
                    functools.partial(_attn_a_call, bounded=True),
                    functools.partial(_attn_a_call, bounded=False),
                    qa_t, ka, va_t, k_len_arr)


def _attn_b_kernel(q_ref, k_ref, v_ref, frev_ref, *refs, bounded):
    if bounded:
        stats_ref, o_ref, qt_ref, acc_ref, bias_ref, *s_refs = refs
    else:
        o_ref, qt_ref, acc_ref, bias_ref, s_ref = refs
    step = pl.program_id(1)
    bq = BQ_B
    n_chunks = k_ref.shape[0] // NK_B
    back = B_REACH // NK_B
    span = (bq + 2 * B_REACH) // NK_B

    def first_chunk(sb):
        return (step * QB_B + sb) * (bq // NK_B) - back

    def is_interior(sb):
        return jnp.logical_and(first_chunk(sb) >= 0, first_chunk(sb) + span <= n_chunks)

    @pl.when(step == 0)
    def _build_bias():
        for hh in range(2):
            for c in range(span):
                seg = (span - 1 - c) * NK_B
                row = frev_ref[hh:hh + 1, seg:seg + 2 * bq]
                rolled = pltpu.roll(jnp.broadcast_to(row, (NK_B, 2 * bq)), 0, 1,
                                    stride=1, stride_axis=0)
                bias_ref[c * NK_B:(c + 1) * NK_B, hh * bq:(hh + 1) * bq] = rolled[:, bq:]

    def stage_queries(sb):
        zeros = jnp.zeros((HEAD_DIM, bq), jnp.bfloat16)
        q0 = q_ref[:HEAD_DIM, sb * bq:(sb + 1) * bq]
        q1 = q_ref[HEAD_DIM:, sb * bq:(sb + 1) * bq]
        qt_ref[:, 2 * sb * bq:(2 * sb + 1) * bq] = jnp.concatenate([q0, zeros], axis=0)
        qt_ref[:, (2 * sb + 1) * bq:(2 * sb + 2) * bq] = jnp.concatenate([zeros, q1], axis=0)
        return q0, q1

    def write(sb, acc):
        inv = 1.0 / acc[HEAD_DIM:HEAD_DIM + 1, :]
        o_ref[:HEAD_DIM, sb * bq:(sb + 1) * bq] = acc[:HEAD_DIM, :bq] * inv[:, :bq]
        o_ref[HEAD_DIM:, sb * bq:(sb + 1) * bq] = acc[:HEAD_DIM, bq:] * inv[:, bq:]

    def logits(sb, c):
        k_start = pl.multiple_of((first_chunk(sb) + c) * NK_B, NK_B)
        b_start = c * NK_B if isinstance(c, int) else pl.multiple_of(c * NK_B, NK_B)
        s = jnp.dot(k_ref[pl.ds(k_start, NK_B), :], qt_ref[:, 2 * sb * bq:(2 * sb + 2) * bq],
                    preferred_element_type=jnp.float32)
        return s + bias_ref[pl.ds(b_start, NK_B), :]

    def clipped_bounds(sb):
        return jnp.maximum(0, -first_chunk(sb)), jnp.minimum(span, n_chunks - first_chunk(sb))

    if bounded:
        def stage_and_shift(sb):
            stats = stats_ref[...]
            widen = lambda r: jnp.concatenate([stats[r:r + 1, :]] * (bq // stats.shape[1]), axis=1)
            shift = []
            for hh, q in enumerate(stage_queries(sb)):
                qf = q.astype(jnp.float32)
                shift.append(jnp.sqrt(jnp.sum(qf * qf, axis=0, keepdims=True)) * widen(2) + widen(hh))
            return jnp.concatenate(shift, axis=1)

        all_interior = jnp.logical_and(is_interior(0), is_interior(QB_B - 1))

        @pl.when(all_interior)
        def _whole_windows():
            shifts = [stage_and_shift(sb) for sb in range(QB_B)]
            order = [(sb, c) for sb in range(QB_B) for c in range(span)]
            s_refs[0][...] = logits(*order[0])
            total = None
            for n, (sb, c) in enumerate(order):
                if n + 1 < len(order):
                    s_refs[(n + 1) % 2][...] = logits(*order[n + 1])
                pv = _pv_product(_probs(s_refs[n % 2][...], shifts[sb]), v_ref[first_chunk(sb) + c])
                total = pv if c == 0 else total + pv
                if c == span - 1:
                    write(sb, total)

        @pl.when(jnp.logical_not(all_interior))
        def _clipped_windows():
            for sb in range(QB_B):
                acc_ref[...] = jnp.zeros_like(acc_ref)
                shift = stage_and_shift(sb)

                def body(c, carry, sb=sb, shift=shift):
                    acc_ref[...] += _pv_product(_probs(logits(sb, c), shift),
                                                v_ref[first_chunk(sb) + c])
                    return carry

                lax.fori_loop(*clipped_bounds(sb), body, 0)
                write(sb, acc_ref[...])
    else:
        for sb in range(QB_B):
            @pl.when(is_interior(sb))
            def _whole_window(sb=sb):
                stage_queries(sb)
                n_ct = 2 * bq // MXU_TILE
                k_start = pl.multiple_of(first_chunk(sb) * NK_B, NK_B)
                mx = _logits_tiles(k_ref[pl.ds(k_start, span * NK_B), :],
                                   qt_ref.at[:, 2 * sb * bq:(2 * sb + 2) * bq], bias_ref, s_ref, n_ct)
                v_tiles = [[v_ref[first_chunk(sb) + c, hh * V_ROWS:(hh + 1) * V_ROWS, :]
                            for hh in range(n_ct)] for c in range(span)]
                _softmax_tiles(s_ref, mx, v_tiles, None, acc_ref)
                write(sb, acc_ref[...])

            @pl.when(jnp.logical_not(is_interior(sb)))
            def _clipped_window(sb=sb):
                acc_ref[...] = jnp.zeros_like(acc_ref)
                stage_queries(sb)

                def body(c, m):
                    return _softmax_step(logits(sb, c), v_ref[first_chunk(sb) + c], m, acc_ref)

                lax.fori_loop(*clipped_bounds(sb), body,
                              jnp.full((1, 2 * bq), NEG_BIG, jnp.float32))
                write(sb, acc_ref[...])


def _attn_b_call(qb_t, kb, vb_t, frev, stats, bounded):
    s_len = kb.shape[0]
    bq = BQ_B
    pairs = N_HEADS_B // 2
    span_rows = bq + 2 * B_REACH
    assert bq == NK_B == MXU_TILE and frev.shape[-1] == span_rows + bq
    assert B_REACH // bq % QB_B == 0 and (s_len // bq) % QB_B == 0
    operands = (qb_t, kb, vb_t, frev) + ((stats,) if bounded else ())
    logits_scratch = ([pltpu.VMEM((NK_B, 2 * bq), jnp.float32)] * 2 if bounded
                      else [pltpu.VMEM((span_rows, 2 * bq), jnp.float32)])
    return pl.pallas_call(
        functools.partial(_attn_b_kernel, bounded=bounded),
        grid=(pairs, s_len // (bq * QB_B)),
        in_specs=[
            pl.BlockSpec((2 * HEAD_DIM, bq * QB_B), lambda j, i: (j, i)),
            pl.BlockSpec((s_len, 2 * HEAD_DIM), lambda j, i: (0, j)),
            pl.BlockSpec((s_len // NK_B, 2 * V_ROWS, NK_B), lambda j, i: (0, j, 0)),
            pl.BlockSpec((None, 2, span_rows + bq), lambda j, i: (j, 0, 0)),
        ] + ([pl.BlockSpec((None, 8, 128), lambda j, i: (j, 0, 0))] if bounded else []),
        out_specs=pl.BlockSpec((2 * HEAD_DIM, bq * QB_B), lambda j, i: (j, i)),
        out_shape=jax.ShapeDtypeStruct((D_B, s_len), jnp.float32),
        scratch_shapes=[
            pltpu.VMEM((2 * HEAD_DIM, 2 * bq * QB_B), jnp.bfloat16),
            pltpu.VMEM((V_ROWS, 2 * bq), jnp.float32),
            pltpu.VMEM((span_rows, 2 * bq), jnp.float32),
        ] + logits_scratch,
        compiler_params=pltpu.CompilerParams(
            dimension_semantics=("arbitrary", "arbitrary"), vmem_limit_bytes=VMEM_LIMIT),
        name="attn_b_bounded" if bounded else "attn_b_online",
    )(*operands)


def _attn_b(qb_t, kb, vb_t, bias, q_len2, k_len2):
    frev, bias_max, bias_self = bias
    q_len = jnp.sqrt(jnp.max(q_len2))
    k_len = jnp.sqrt(jnp.max(k_len2))
    span = 2.0 * q_len * k_len + jnp.max(bias_max - bias_self)
    pairs = N_HEADS_B // 2
    stats = jnp.concatenate([bias_max.reshape(pairs, 2), jnp.broadcast_to(k_len, (pairs, 1)),
                             jnp.zeros((pairs, 5), jnp.float32)], axis=1)
    stats = jnp.broadcast_to(stats[:, :, None], (pairs, 8, 128))
    return lax.cond(span <= SHIFT_SPAN_LIMIT,
                    functools.partial(_attn_b_call, bounded=True),
                    functools.partial(_attn_b_call, bounded=False),
                    qb_t, kb, vb_t, frev, stats)


def _t5_bucket_index(rel):
    nb = N_BUCKETS // 2
    max_exact = nb // 2
    side = jnp.where(rel > 0, nb, 0)
    n = jnp.abs(rel)
    large = max_exact + (jnp.log(jnp.maximum(n, max_exact).astype(jnp.float32) / max_exact)
                         / math.log(MAX_DISTANCE / max_exact) * (nb - max_exact)).astype(jnp.int32)
    large = jnp.minimum(large, nb - 1)
    return side + jnp.where(n < max_exact, n, large)


def _dilated_bias_table(rel_bias):
    bq = BQ_B
    reach_all = B_REACH + bq - 1
    delta = jnp.arange(reach_all, -reach_all - 1, -1)
    count = jnp.zeros(delta.shape, jnp.float32)
    for window, dilation in DILATED_PATTERNS:
        inside = (delta % dilation == 0) & (jnp.abs(delta) <= window // 2)
        count = count + inside.astype(jnp.float32)
    table = rel_bias[_t5_bucket_index(delta)].astype(jnp.float32)
    table = jnp.where((count > 0)[:, None],
                      (table + jnp.log(jnp.maximum(count, 1.0))[:, None]) * LOG2E, NEG_BIG)
    frev = jnp.pad(table.T, ((0, 0), (1, 0)))
    frev = frev.reshape(N_HEADS_B // 2, 2, frev.shape[1])
    bias_max = jnp.max(jnp.where((count > 0)[:, None], table, -jnp.inf), axis=0)
    return frev, bias_max, table[reach_all]


def _out_kernel(ya_ref, yb_ref, x_ref, ga_ref, gb_ref, w_ref, gpost_ref, h_ref):
    def norm_t(y_t, g_col):
        ms = jnp.mean(y_t * y_t, axis=0, keepdims=True)
        return (y_t * lax.rsqrt(ms + EPS) * g_col).astype(jnp.bfloat16)

    y_t = jnp.concatenate([norm_t(ya_ref[...], ga_ref[...]),
                           norm_t(yb_ref[...], gb_ref[...])], axis=0)
    y = lax.dot_general(y_t, w_ref[...], (((0,), (0,)), ((), ())),
                        preferred_element_type=jnp.float32)
    h_ref[...] = x_ref[...] + _rms_rows(y, gpost_ref[...])


def _out_call(ya_t, yb_t, x2, ga, gb, w_out, g_post):
    s_len = x2.shape[0]
    tm = TM_OUT
    const = lambda i: (0, 0)
    return pl.pallas_call(
        _out_kernel,
        grid=(s_len // tm,),
        in_specs=[
            pl.BlockSpec((D_A, tm), lambda i: (0, i)),
            pl.BlockSpec((D_B, tm), lambda i: (0, i)),
            pl.BlockSpec((tm, D_MODEL), lambda i: (i, 0)),
            pl.BlockSpec((D_A, 1), const),
            pl.BlockSpec((D_B, 1), const),
            pl.BlockSpec((D_A + D_B, D_MODEL), const),
            pl.BlockSpec((1, D_MODEL), const),
        ],
        out_specs=pl.BlockSpec((tm, D_MODEL), lambda i: (i, 0)),
        out_shape=jax.ShapeDtypeStruct((s_len, D_MODEL), jnp.float32),
        compiler_params=pltpu.CompilerParams(
            dimension_semantics=("arbitrary",), vmem_limit_bytes=VMEM_LIMIT),
        name="out_proj",
    )(ya_t, yb_t, x2, ga, gb, w_out, g_post)


def _ffn_kernel(h_ref, p_ref, g1_ref, w1_ref, w2_ref, g2_ref, g3_ref, wg_ref, wp_ref, o_ref):
    h = h_ref[...]
    xn = _rms_rows(h, g1_ref[...]).astype(jnp.bfloat16)
    f = None
    for c in range(D_FF // FF_CHUNK):
        cols = slice(c * FF_CHUNK, (c + 1) * FF_CHUNK)
        u = jnp.dot(xn, w1_ref[:, cols], preferred_element_type=jnp.float32)
        u = jnp.square(jnp.maximum(u, 0.0)).astype(jnp.bfloat16)
        part = jnp.dot(u, w2_ref[cols, :], preferred_element_type=jnp.float32)
        f = part if f is None else f + part
    h = h + _rms_rows(f, g2_ref[...])
    gate_in = _rms_rows(h, g3_ref[...]).astype(jnp.bfloat16)
    gate = jax.nn.sigmoid(jnp.dot(gate_in, wg_ref[...], preferred_element_type=jnp.float32))
    emb = jnp.dot(p_ref[...].astype(jnp.bfloat16), wp_ref[...], preferred_element_type=jnp.float32)
    o_ref[...] = h + gate * emb


def _ffn_call(h1, p2, g1, w1, w2, g2, g3, wg, wp):
    s_len = h1.shape[0]
    tm = TM_FFN
    const = lambda i: (0, 0)
    resident = functools.partial(pl.BlockSpec, index_map=const, pipeline_mode=pl.Buffered(1))
    return pl.pallas_call(
        _ffn_kernel,
        grid=(s_len // tm,),
        in_specs=[
            pl.BlockSpec((tm, D_MODEL), lambda i: (i, 0)),
            pl.BlockSpec((tm, D_PLE), lambda i: (i, 0)),
            pl.BlockSpec((1, D_MODEL), const),
            resident((D_MODEL, D_FF)),
            resident((D_FF, D_MODEL)),
            pl.BlockSpec((1, D_MODEL), const),
            pl.BlockSpec((1, D_MODEL), const),
            resident((D_MODEL, D_MODEL)),
            resident((D_PLE, D_MODEL)),
        ],
        out_specs=pl.BlockSpec((tm, D_MODEL), lambda i: (i, 0)),
        out_shape=jax.ShapeDtypeStruct((s_len, D_MODEL), jnp.float32),
        compiler_params=pltpu.CompilerParams(
            dimension_semantics=("arbitrary",), vmem_limit_bytes=VMEM_LIMIT),
        name="ffn_ple",
    )(h1, p2, g1, w1, w2, g2, g3, wg, wp)


def _rope_tables_t(n_tokens):
    tok = jnp.arange(n_tokens)
    row = (tok // GRID_W).astype(jnp.float32)
    col = (tok % GRID_W).astype(jnp.float32)
    n_axis = ROPE_HALF // 2
    inv_freq = ROPE_THETA ** (-jnp.arange(n_axis, dtype=jnp.float32) / n_axis)
    ang = jnp.concatenate([inv_freq[:, None] * row[None, :], inv_freq[:, None] * col[None, :]], axis=0)
    return jnp.cos(ang), jnp.sin(ang)


def _layer(h, p_i, w_in, g_attn_pre, g_q, g_k, g_out_a, g_out_b, w_out, g_attn_post, bias_t,
           g_mlp_pre, w_ff1, w_ff2, g_mlp_post, g_ple, w_ple_gate, w_ple_proj, cos_t, sin_t):
    bf = jnp.bfloat16
    row = lambda g: g.reshape(1, -1)
    col = lambda g: g.reshape(-1, 1)

    qa_t, ka, va_t, qb_t, kb, vb_t, len2 = _proj_call(
        h, row(g_attn_pre), w_in.T.astype(bf), col(g_q), col(g_k), cos_t, sin_t)
    ya_t = _attn_a(qa_t, ka, va_t, len2[0], len2[1])
    yb_t = _attn_b(qb_t, kb, vb_t, bias_t, len2[2], len2[3])
    h1 = _out_call(ya_t, yb_t, h, col(g_out_a), col(g_out_b), w_out.astype(bf), row(g_attn_post))
    return _ffn_call(h1, p_i, row(g_mlp_pre), w_ff1.astype(bf), w_ff2.astype(bf), row(g_mlp_post),
                     row(g_ple), w_ple_gate.astype(bf), w_ple_proj.astype(bf))


def kernel(x, p, w_in, g_attn_pre, g_q, g_k, g_out_a, g_out_b, w_out, g_attn_post, rel_bias,
           g_mlp_pre, w_ff1, w_ff2, g_mlp_post, g_ple, w_ple_gate, w_ple_proj):
    b, s_len, _ = x.shape
    cos_t, sin_t = _rope_tables_t(s_len)
    bias_t = _dilated_bias_table(rel_bias)
    outs = []
    for bi in range(b):
        h = x[bi]
        for i in range(w_in.shape[0]):
            h = _layer(h, p[i, bi], w_in[i], g_attn_pre[i], g_q[i], g_k[i], g_out_a[i], g_out_b[i],
                       w_out[i], g_attn_post[i], bias_t, g_mlp_pre[i], w_ff1[i], w_ff2[i],
                       g_mlp_post[i], g_ple[i], w_ple_gate[i], w_ple_proj[i], cos_t, sin_t)
        outs.append(h)
    return jnp.stack(outs, axis=0)
```

```python
import functools
import math

import jax
import jax.numpy as jnp
from jax import lax
from jax.experimental import pallas as pl
from jax.experimental.pallas import tpu as pltpu

D_MODEL = 1024
HEAD_DIM = 64
N_HEADS_A = 8
N_KV_A = 2
N_HEADS_B = 8
D_A = N_HEADS_A * HEAD_DIM
D_KV_A = N_KV_A * HEAD_DIM
D_B = N_HEADS_B * HEAD_DIM
D_FF = 4 * D_MODEL
D_PLE = 256
GRID_W = 64
ROPE_THETA = 10000.0
ROPE_HALF = HEAD_DIM // 2
DILATED_PATTERNS = ((128, 1), (512, 4), (2048, 16))
N_BUCKETS = 32
MAX_DISTANCE = 1024
EPS = 1e-6
NEG_BIG = -1e30

LOG2E = math.log2(math.e)
Q_SCALE = HEAD_DIM ** -0.5 * LOG2E

MXU_TILE = 256
ONES_ROWS = 16
V_ROWS = HEAD_DIM + ONES_ROWS
VMEM_LIMIT = 56 * 1024 * 1024

TM_PROJ = 512
BQ_A = 128
NK_A = 256
UNROLL_A = 8
UNROLL_A_BOUNDED = 64
ROUNDING_SLACK = (1.0 + 2.0 ** -8) ** 2
SHIFT_SPAN_LIMIT = 96.0
BQ_B = 256
NK_B = 256
QB_B = 4
B_REACH = 1024
TM_TAIL = 512
FF_CHUNK = 1024

_T_QA, _T_KA, _T_VA = 0, D_A, D_A + D_KV_A
_T_QB, _T_KB, _T_VB = D_A + 2 * D_KV_A, D_A + 2 * D_KV_A + D_B, D_A + 2 * D_KV_A + 2 * D_B
_T_ROWS = D_A + 2 * D_KV_A + 3 * D_B


def _rms_rows(x, g_row):
    ms = jnp.mean(x * x, axis=-1, keepdims=True)
    return x * lax.rsqrt(ms + EPS) * g_row


def _proj_kernel(x_ref, g_ref, wt_ref, gq_ref, gk_ref, cos_ref, sin_ref,
                 qa_ref, ka_ref, va_ref, qb_ref, kb_ref, vb_ref, len_ref):
    xn = _rms_rows(x_ref[...], g_ref[...]).astype(jnp.bfloat16)
    cuts = (0, _T_KA, _T_QB, _T_KB, _T_VB, _T_ROWS)
    groups = {}

    def rows(lo, hi):
        for g_lo, g_hi in zip(cuts[:-1], cuts[1:]):
            if g_lo <= lo and hi <= g_hi:
                if g_lo not in groups:
                    groups[g_lo] = lax.dot_general(wt_ref[g_lo:g_hi, :], xn, (((1,), (1,)), ((), ())),
                                                   preferred_element_type=jnp.float32)
                return groups[g_lo][lo - g_lo:hi - g_lo]
        raise ValueError((lo, hi))

    cos = cos_ref[...]
    sin = sin_ref[...]

    def norm_rope(blk, g_col):
        ms = jnp.mean(blk * blk, axis=0, keepdims=True)
        y = blk * lax.rsqrt(ms + EPS) * g_col
        y1, y2 = y[:ROPE_HALF], y[ROPE_HALF:]
        return jnp.concatenate([y1 * cos - y2 * sin, y2 * cos + y1 * sin], axis=0)

    def longest2(heads):
        out = None
        for v in heads:
            n2 = jnp.sum(v * v, axis=0, keepdims=True)
            out = n2 if out is None else jnp.maximum(out, n2)
        return out

    gq = gq_ref[...]
    gk = gk_ref[...]
    q_heads = []
    for h in range(N_HEADS_A):
        r0 = _T_QA + h * HEAD_DIM
        q_heads.append(norm_rope(rows(r0, r0 + HEAD_DIM), gq) * Q_SCALE)
        qa_ref[h * HEAD_DIM:(h + 1) * HEAD_DIM, :] = q_heads[-1].astype(jnp.bfloat16)
    k_heads = [norm_rope(rows(_T_KA + g * HEAD_DIM, _T_KA + (g + 1) * HEAD_DIM), gk)
               for g in range(N_KV_A)]
    ka_ref[...] = jnp.concatenate(k_heads, axis=0).T.astype(jnp.bfloat16)

    qb = rows(_T_QB, _T_QB + D_B) * Q_SCALE
    kb = rows(_T_KB, _T_KB + D_B)
    qb_ref[...] = qb.astype(jnp.bfloat16)
    kb_ref[...] = kb.T.astype(jnp.bfloat16)
    split = lambda v: [v[h * HEAD_DIM:(h + 1) * HEAD_DIM] for h in range(N_HEADS_B)]
    len_ref[0:1, :] = longest2(q_heads) * ROUNDING_SLACK
    len_ref[1:2, :] = longest2(k_heads) * ROUNDING_SLACK
    len_ref[2:3, :] = longest2(split(qb)) * ROUNDING_SLACK
    len_ref[3:4, :] = longest2(split(kb)) * ROUNDING_SLACK
    len_ref[4:8, :] = jnp.zeros((4, len_ref.shape[1]), jnp.float32)

    tm = x_ref.shape[0]
    def store_slabs(v_ref, v_t, n_heads, nk):
        for c in range(tm // nk):
            for h in range(n_heads):
                r0 = h * V_ROWS
                v_ref[c, r0:r0 + HEAD_DIM, :] = v_t[h * HEAD_DIM:(h + 1) * HEAD_DIM,
                                                    c * nk:(c + 1) * nk]
                v_ref[c, r0 + HEAD_DIM:r0 + V_ROWS, :] = jnp.ones((ONES_ROWS, nk), jnp.bfloat16)

    store_slabs(va_ref, rows(_T_VA, _T_VA + D_KV_A).astype(jnp.bfloat16), N_KV_A, NK_A)
    store_slabs(vb_ref, rows(_T_VB, _T_VB + D_B).astype(jnp.bfloat16), N_HEADS_B, NK_B)


def _proj_call(x2, g_pre, wt, gq, gk, cos_t, sin_t):
    s_len = x2.shape[0]
    tm = TM_PROJ
    const = lambda i: (0, 0)
    return pl.pallas_call(
        _proj_kernel,
        grid=(s_len // tm,),
        in_specs=[
            pl.BlockSpec((tm, D_MODEL), lambda i: (i, 0)),
            pl.BlockSpec((1, D_MODEL), const),
            pl.BlockSpec((_T_ROWS, D_MODEL), const),
            pl.BlockSpec((HEAD_DIM, 1), const),
            pl.BlockSpec((HEAD_DIM, 1), const),
            pl.BlockSpec((ROPE_HALF, tm), lambda i: (0, i)),
            pl.BlockSpec((ROPE_HALF, tm), lambda i: (0, i)),
        ],
        out_specs=[
            pl.BlockSpec((D_A, tm), lambda i: (0, i)),
            pl.BlockSpec((tm, D_KV_A), lambda i: (i, 0)),
            pl.BlockSpec((tm // NK_A, V_ROWS * N_KV_A, NK_A), lambda i: (i, 0, 0)),
            pl.BlockSpec((D_B, tm), lambda i: (0, i)),
            pl.BlockSpec((tm, D_B), lambda i: (i, 0)),
            pl.BlockSpec((tm // NK_B, V_ROWS * N_HEADS_B, NK_B), lambda i: (i, 0, 0)),
            pl.BlockSpec((8, tm), lambda i: (0, i)),
        ],
        out_shape=[
            jax.ShapeDtypeStruct((D_A, s_len), jnp.bfloat16),
            jax.ShapeDtypeStruct((s_len, D_KV_A), jnp.bfloat16),
            jax.ShapeDtypeStruct((s_len // NK_A, V_ROWS * N_KV_A, NK_A),
                                 jnp.bfloat16),
            jax.ShapeDtypeStruct((D_B, s_len), jnp.bfloat16),
            jax.ShapeDtypeStruct((s_len, D_B), jnp.bfloat16),
            jax.ShapeDtypeStruct((s_len // NK_B, V_ROWS * N_HEADS_B, NK_B),
                                 jnp.bfloat16),
            jax.ShapeDtypeStruct((8, s_len), jnp.float32),
        ],
        compiler_params=pltpu.CompilerParams(
            dimension_semantics=("arbitrary",), vmem_limit_bytes=VMEM_LIMIT),
        name="proj",
    )(x2, g_pre, wt, gq, gk, cos_t, sin_t)


def _probs(s, m):
    return jnp.exp2(s - m).astype(jnp.bfloat16)


def _pv_product(p, v_t):
    groups = v_t.shape[0] // V_ROWS
    width = p.shape[1] // groups
    pv = [jnp.dot(v_t[g * V_ROWS:(g + 1) * V_ROWS], p[:, g * width:(g + 1) * width],
                  preferred_element_type=jnp.float32) for g in range(groups)]
    return jnp.concatenate(pv, axis=1)


def _softmax_step(s, v_t, m, acc_ref):
    m_new = jnp.maximum(m, jnp.max(s, axis=0, keepdims=True))
    acc_ref[...] = acc_ref[...] * jnp.exp2(m - m_new) + _pv_product(_probs(s, m_new), v_t)
    return m_new


def _logits_tiles(k_rows, qt_ref, bias_ref, s_ref, n_ct):
    maxima = []
    for ct in range(n_ct):
        cols = slice(ct * MXU_TILE, (ct + 1) * MXU_TILE)
        s = jnp.dot(k_rows, qt_ref[:, cols], preferred_element_type=jnp.float32)
        if bias_ref is not None:
            s = s + bias_ref[:, cols]
        s_ref[:, cols] = s
        maxima.append(jnp.max(s_ref[:, cols], axis=0, keepdims=True))
    return maxima


def _softmax_tiles(s_ref, mx, v_tiles, m, acc_ref):
    m_out = []
    for ct in range(len(mx)):
        cols = slice(ct * MXU_TILE, (ct + 1) * MXU_TILE)
        pv = None
        m_new = mx[ct] if m is None else jnp.maximum(m[ct], mx[ct])
        for r, v_t in enumerate(v_tiles):
            p = _probs(s_ref[r * MXU_TILE:(r + 1) * MXU_TILE, cols], m_new)
            part = jnp.dot(v_t[ct], p, preferred_element_type=jnp.float32)
            pv = part if pv is None else pv + part
        if m is None:
            acc_ref[:, cols] = pv
        else:
            acc_ref[:, cols] = acc_ref[:, cols] * jnp.exp2(m[ct] - m_new) + pv
        m_out.append(m_new)
    return m_out


def _attn_a_kernel(q_ref, k_ref, v_ref, *refs, bounded):
    if bounded:
        klen_ref, o_ref, qt_ref, acc_ref, *s_refs = refs
    else:
        o_ref, qt_ref, acc_ref, *s_refs = refs
    bq = q_ref.shape[1]
    m_cols = N_HEADS_A * bq
    grp = N_HEADS_A // N_KV_A
    zeros = jnp.zeros((HEAD_DIM, bq), jnp.bfloat16)
    q_len = []
    for h in range(N_HEADS_A):
        qh = q_ref[h * HEAD_DIM:(h + 1) * HEAD_DIM, :]
        parts = [qh if g == h // grp else zeros for g in range(N_KV_A)]
        qt_ref[:, h * bq:(h + 1) * bq] = jnp.concatenate(parts, axis=0)
        qf = qh.astype(jnp.float32)
        q_len.append(jnp.sqrt(jnp.sum(qf * qf, axis=0, keepdims=True)))
    acc_ref[...] = jnp.zeros_like(acc_ref)
    n_chunks = k_ref.shape[0] // NK_A
    unroll = UNROLL_A_BOUNDED if bounded else UNROLL_A

    def logits(c):
        start = pl.multiple_of(c * NK_A, NK_A)
        return jnp.dot(k_ref[pl.ds(start, NK_A), :], qt_ref[...],
                       preferred_element_type=jnp.float32)

    s_refs[0][...] = logits(0)

    if bounded:
        shift = jnp.concatenate(q_len, axis=1) * klen_ref[...]

        def body(t, carry):
            c0 = t * unroll
            total = None
            for u in range(unroll):
                nxt = jnp.minimum(c0 + u + 1, n_chunks - 1)
                s_refs[(u + 1) % 2][...] = logits(nxt)
                pv = _pv_product(_probs(s_refs[u % 2][...], shift), v_ref[c0 + u])
                total = pv if total is None else total + pv
            acc_ref[...] += total
            return carry

        lax.fori_loop(0, n_chunks // unroll, body, 0)
    else:
        def body(t, m):
            c0 = t * unroll
            for u in range(unroll):
                nxt = jnp.minimum(c0 + u + 1, n_chunks - 1)
                s_refs[(u + 1) % 2][...] = logits(nxt)
                m = _softmax_step(s_refs[u % 2][...], v_ref[c0 + u], m, acc_ref)
            return m

        lax.fori_loop(0, n_chunks // unroll, body, jnp.full((1, m_cols), NEG_BIG, jnp.float32))
    inv = 1.0 / acc_ref[HEAD_DIM:HEAD_DIM + 1, :]
    for h in range(N_HEADS_A):
        cols = slice(h * bq, (h + 1) * bq)
        o_ref[h * HEAD_DIM:(h + 1) * HEAD_DIM, :] = acc_ref[:HEAD_DIM, cols] * inv[:, cols]


def _attn_a_call(qa_t, ka, va_t, k_len, bounded):
    s_len = ka.shape[0]
    bq = BQ_A
    operands = (qa_t, ka, va_t) + ((k_len,) if bounded else ())
    return pl.pallas_call(
        functools.partial(_attn_a_kernel, bounded=bounded),
        grid=(s_len // bq,),
        in_specs=[
            pl.BlockSpec((D_A, bq), lambda i: (0, i)),
            pl.BlockSpec((s_len, D_KV_A), lambda i: (0, 0)),
            pl.BlockSpec((s_len // NK_A, V_ROWS * N_KV_A, NK_A), lambda i: (0, 0, 0)),
        ] + ([pl.BlockSpec((1, 1), lambda i: (0, 0))] if bounded else []),
        out_specs=pl.BlockSpec((D_A, bq), lambda i: (0, i)),
        out_shape=jax.ShapeDtypeStruct((D_A, s_len), jnp.float32),
        scratch_shapes=[
            pltpu.VMEM((D_KV_A, N_HEADS_A * bq), jnp.bfloat16),
            pltpu.VMEM((V_ROWS, N_HEADS_A * bq), jnp.float32),
        ] + [pltpu.VMEM((NK_A, N_HEADS_A * bq), jnp.float32)] * 2,
        compiler_params=pltpu.CompilerParams(
            dimension_semantics=("arbitrary",), vmem_limit_bytes=VMEM_LIMIT),
        name="attn_a_bounded" if bounded else "attn_a_online",
    )(*operands)


def _attn_a(qa_t, ka, va_t, q_len2, k_len2):
    q_len = jnp.sqrt(jnp.max(q_len2))
    k_len = jnp.sqrt(jnp.max(k_len2))
    k_len_arr = k_len.reshape(1, 1)
    return lax.cond(2.0 * q_len * k_len <= SHIFT_SPAN_LIMIT,
                    functools.partial(_attn_a_call, bounded=True),
                    functools.partial(_attn_a_call, bounded=False),
                    qa_t, ka, va_t, k_len_arr)


def _attn_b_kernel(q_ref, k_ref, v_ref, frev_ref, *refs, bounded):
    if bounded:
        stats_ref, o_ref, qt_ref, acc_ref, bias_ref, *s_refs = refs
    else:
        o_ref, qt_ref, acc_ref, bias_ref, s_ref = refs
    step = pl.program_id(1)
    bq = BQ_B
    n_chunks = k_ref.shape[0] // NK_B
    back = B_REACH // NK_B
    span = (bq + 2 * B_REACH) // NK_B

    def first_chunk(sb):
        return (step * QB_B + sb) * (bq // NK_B) - back

    def is_interior(sb):
        return jnp.logical_and(first_chunk(sb) >= 0, first_chunk(sb) + span <= n_chunks)

    @pl.when(step == 0)
    def _build_bias():
        for hh in range(2):
            for c in range(span):
                seg = (span - 1 - c) * NK_B
                row = frev_ref[hh:hh + 1, seg:seg + 2 * bq]
                rolled = pltpu.roll(jnp.broadcast_to(row, (NK_B, 2 * bq)), 0, 1,
                                    stride=1, stride_axis=0)
                bias_ref[c * NK_B:(c + 1) * NK_B, hh * bq:(hh + 1) * bq] = rolled[:, bq:]

    def stage_queries(sb):
        zeros = jnp.zeros((HEAD_DIM, bq), jnp.bfloat16)
        q0 = q_ref[:HEAD_DIM, sb * bq:(sb + 1) * bq]
        q1 = q_ref[HEAD_DIM:, sb * bq:(sb + 1) * bq]
        qt_ref[:, 2 * sb * bq:(2 * sb + 1) * bq] = jnp.concatenate([q0, zeros], axis=0)
        qt_ref[:, (2 * sb + 1) * bq:(2 * sb + 2) * bq] = jnp.concatenate([zeros, q1], axis=0)
        return q0, q1

    def write(sb, acc):
        inv = 1.0 / acc[HEAD_DIM:HEAD_DIM + 1, :]
        o_ref[:HEAD_DIM, sb * bq:(sb + 1) * bq] = acc[:HEAD_DIM, :bq] * inv[:, :bq]
        o_ref[HEAD_DIM:, sb * bq:(sb + 1) * bq] = acc[:HEAD_DIM, bq:] * inv[:, bq:]

    def logits(sb, c):
        k_start = pl.multiple_of((first_chunk(sb) + c) * NK_B, NK_B)
        b_start = c * NK_B if isinstance(c, int) else pl.multiple_of(c * NK_B, NK_B)
        s = jnp.dot(k_ref[pl.ds(k_start, NK_B), :], qt_ref[:, 2 * sb * bq:(2 * sb + 2) * bq],
                    preferred_element_type=jnp.float32)
        return s + bias_ref[pl.ds(b_start, NK_B), :]

    def clipped_bounds(sb):
        return jnp.maximum(0, -first_chunk(sb)), jnp.minimum(span, n_chunks - first_chunk(sb))

    if bounded:
        def stage_and_shift(sb):
            stats = stats_ref[...]
            widen = lambda r: jnp.concatenate([stats[r:r + 1, :]] * (bq // stats.shape[1]), axis=1)
            shift = []
            for hh, q in enumerate(stage_queries(sb)):
                qf = q.astype(jnp.float32)
                shift.append(jnp.sqrt(jnp.sum(qf * qf, axis=0, keepdims=True)) * widen(2) + widen(hh))
            return jnp.concatenate(shift, axis=1)

        all_interior = jnp.logical_and(is_interior(0), is_interior(QB_B - 1))

        @pl.when(all_interior)
        def _whole_windows():
            shifts = [stage_and_shift(sb) for sb in range(QB_B)]
            order = [(sb, c) for sb in range(QB_B) for c in range(span)]
            s_refs[0][...] = logits(*order[0])
            total = None
            for n, (sb, c) in enumerate(order):
                if n + 1 < len(order):
                    s_refs[(n + 1) % 2][...] = logits(*order[n + 1])
                pv = _pv_product(_probs(s_refs[n % 2][...], shifts[sb]), v_ref[first_chunk(sb) + c])
                total = pv if c == 0 else total + pv
                if c == span - 1:
                    write(sb, total)

        @pl.when(jnp.logical_not(all_interior))
        def _clipped_windows():
            for sb in range(QB_B):
                acc_ref[...] = jnp.zeros_like(acc_ref)
                shift = stage_and_shift(sb)

                def body(c, carry, sb=sb, shift=shift):
                    acc_ref[...] += _pv_product(_probs(logits(sb, c), shift),
                                                v_ref[first_chunk(sb) + c])
                    return carry

                lax.fori_loop(*clipped_bounds(sb), body, 0)
                write(sb, acc_ref[...])
    else:
        for sb in range(QB_B):
            @pl.when(is_interior(sb))
            def _whole_window(sb=sb):
                stage_queries(sb)
                n_ct = 2 * bq // MXU_TILE
                k_start = pl.multiple_of(first_chunk(sb) * NK_B, NK_B)
                mx = _logits_tiles(k_ref[pl.ds(k_start, span * NK_B), :],
                                   qt_ref.at[:, 2 * sb * bq:(2 * sb + 2) * bq], bias_ref, s_ref, n_ct)
                v_tiles = [[v_ref[first_chunk(sb) + c, hh * V_ROWS:(hh + 1) * V_ROWS, :]
                            for hh in range(n_ct)] for c in range(span)]
                _softmax_tiles(s_ref, mx, v_tiles, None, acc_ref)
                write(sb, acc_ref[...])

            @pl.when(jnp.logical_not(is_interior(sb)))
            def _clipped_window(sb=sb):
                acc_ref[...] = jnp.zeros_like(acc_ref)
                stage_queries(sb)

                def body(c, m):
                    return _softmax_step(logits(sb, c), v_ref[first_chunk(sb) + c], m, acc_ref)

                lax.fori_loop(*clipped_bounds(sb), body,
                              jnp.full((1, 2 * bq), NEG_BIG, jnp.float32))
                write(sb, acc_ref[...])


def _attn_b_call(qb_t, kb, vb_t, frev, stats, bounded):
    s_len = kb.shape[0]
    bq = BQ_B
    pairs = N_HEADS_B // 2
    span_rows = bq + 2 * B_REACH
    assert bq == NK_B == MXU_TILE and frev.shape[-1] == span_rows + bq
    assert B_REACH // bq % QB_B == 0 and (s_len // bq) % QB_B == 0
    operands = (qb_t, kb, vb_t, frev) + ((stats,) if bounded else ())
    logits_scratch = ([pltpu.VMEM((NK_B, 2 * bq), jnp.float32)] * 2 if bounded
                      else [pltpu.VMEM((span_rows, 2 * bq), jnp.float32)])
    return pl.pallas_call(
        functools.partial(_attn_b_kernel, bounded=bounded),
        grid=(pairs, s_len // (bq * QB_B)),
        in_specs=[
            pl.BlockSpec((2 * HEAD_DIM, bq * QB_B), lambda j, i: (j, i)),
            pl.BlockSpec((s_len, 2 * HEAD_DIM), lambda j, i: (0, j)),
            pl.BlockSpec((s_len // NK_B, 2 * V_ROWS, NK_B), lambda j, i: (0, j, 0)),
            pl.BlockSpec((None, 2, span_rows + bq), lambda j, i: (j, 0, 0)),
        ] + ([pl.BlockSpec((None, 8, 128), lambda j, i: (j, 0, 0))] if bounded else []),
        out_specs=pl.BlockSpec((2 * HEAD_DIM, bq * QB_B), lambda j, i: (j, i)),
        out_shape=jax.ShapeDtypeStruct((D_B, s_len), jnp.float32),
        scratch_shapes=[
            pltpu.VMEM((2 * HEAD_DIM, 2 * bq * QB_B), jnp.bfloat16),
            pltpu.VMEM((V_ROWS, 2 * bq), jnp.float32),
            pltpu.VMEM((span_rows, 2 * bq), jnp.float32),
        ] + logits_scratch,
        compiler_params=pltpu.CompilerParams(
            dimension_semantics=("arbitrary", "arbitrary"), vmem_limit_bytes=VMEM_LIMIT),
        name="attn_b_bounded" if bounded else "attn_b_online",
    )(*operands)


def _attn_b(qb_t, kb, vb_t, bias, q_len2, k_len2):
    frev, bias_max, bias_self = bias
    q_len = jnp.sqrt(jnp.max(q_len2))
    k_len = jnp.sqrt(jnp.max(k_len2))
    span = 2.0 * q_len * k_len + jnp.max(bias_max - bias_self)
    pairs = N_HEADS_B // 2
    stats = jnp.concatenate([bias_max.reshape(pairs, 2), jnp.broadcast_to(k_len, (pairs, 1)),
                             jnp.zeros((pairs, 5), jnp.float32)], axis=1)
    stats = jnp.broadcast_to(stats[:, :, None], (pairs, 8, 128))
    return lax.cond(span <= SHIFT_SPAN_LIMIT,
                    functools.partial(_attn_b_call, bounded=True),
                    functools.partial(_attn_b_call, bounded=False),
                    qb_t, kb, vb_t, frev, stats)


def _t5_bucket_index(rel):
    nb = N_BUCKETS // 2
    max_exact = nb // 2
    side = jnp.where(rel > 0, nb, 0)
    n = jnp.abs(rel)
    large = max_exact + (jnp.log(jnp.maximum(n, max_exact).astype(jnp.float32) / max_exact)
                         / math.log(MAX_DISTANCE / max_exact) * (nb - max_exact)).astype(jnp.int32)
    large = jnp.minimum(large, nb - 1)
    return side + jnp.where(n < max_exact, n, large)


def _dilated_bias_table(rel_bias):
    bq = BQ_B
    reach_all = B_REACH + bq - 1
    delta = jnp.arange(reach_all, -reach_all - 1, -1)
    count = jnp.zeros(delta.shape, jnp.float32)
    for window, dilation in DILATED_PATTERNS:
        inside = (delta % dilation == 0) & (jnp.abs(delta) <= window // 2)
        count = count + inside.astype(jnp.float32)
    table = rel_bias[_t5_bucket_index(delta)].astype(jnp.float32)
    table = jnp.where((count > 0)[:, None],
                      (table + jnp.log(jnp.maximum(count, 1.0))[:, None]) * LOG2E, NEG_BIG)
    frev = jnp.pad(table.T, ((0, 0), (1, 0)))
    frev = frev.reshape(N_HEADS_B // 2, 2, frev.shape[1])
    bias_max = jnp.max(jnp.where((count > 0)[:, None], table, -jnp.inf), axis=0)
    return frev, bias_max, table[reach_all]


def _tail_kernel(ya_ref, yb_ref, x_ref, p_ref, ga_ref, gb_ref, wo_ref, gpost_ref,
                 g1_ref, w1_ref, w2_ref, g2_ref, g3_ref, wg_ref, wp_ref, o_ref):
    def norm_t(y_t, g_col):
        ms = jnp.mean(y_t * y_t, axis=0, keepdims=True)
        return (y_t * lax.rsqrt(ms + EPS) * g_col).astype(jnp.bfloat16)

    y_t = jnp.concatenate([norm_t(ya_ref[...], ga_ref[...]),
                           norm_t(yb_ref[...], gb_ref[...])], axis=0)
    y = lax.dot_general(y_t, wo_ref[...], (((0,), (0,)), ((), ())),
                        preferred_element_type=jnp.float32)
    h = x_ref[...] + _rms_rows(y, gpost_ref[...])

    xn = _rms_rows(h, g1_ref[...]).astype(jnp.bfloat16)
    f = None
    for c in range(D_FF // FF_CHUNK):
        cols = slice(c * FF_CHUNK, (c + 1) * FF_CHUNK)
        u = jnp.dot(xn, w1_ref[:, cols], preferred_element_type=jnp.float32)
        u = jnp.square(jnp.maximum(u, 0.0)).astype(jnp.bfloat16)
        part = jnp.dot(u, w2_ref[cols, :], preferred_element_type=jnp.float32)
        f = part if f is None else f + part
    h = h + _rms_rows(f, g2_ref[...])
    gate_in = _rms_rows(h, g3_ref[...]).astype(jnp.bfloat16)
    gate = jax.nn.sigmoid(jnp.dot(gate_in, wg_ref[...], preferred_element_type=jnp.float32))
    emb = jnp.dot(p_ref[...].astype(jnp.bfloat16), wp_ref[...], preferred_element_type=jnp.float32)
    o_ref[...] = h + gate * emb


def _tail_call(ya_t, yb_t, x2, p2, ga, gb, w_out, g_post, g1, w1, w2, g2, g3, wg, wp):
    s_len = x2.shape[0]
    tm = TM_TAIL
    const = lambda i: (0, 0)
    resident = functools.partial(pl.BlockSpec, index_map=const, pipeline_mode=pl.Buffered(1))
    row_vec = pl.BlockSpec((1, D_MODEL), const)
    return pl.pallas_call(
        _tail_kernel,
        grid=(s_len // tm,),
        in_specs=[
            pl.BlockSpec((D_A, tm), lambda i: (0, i)),
            pl.BlockSpec((D_B, tm), lambda i: (0, i)),
            pl.BlockSpec((tm, D_MODEL), lambda i: (i, 0)),
            pl.BlockSpec((tm, D_PLE), lambda i: (i, 0)),
            pl.BlockSpec((D_A, 1), const),
            pl.BlockSpec((D_B, 1), const),
            resident((D_A + D_B, D_MODEL)),
            row_vec,
            row_vec,
            resident((D_MODEL, D_FF)),
            resident((D_FF, D_MODEL)),
            row_vec,
            row_vec,
            resident((D_MODEL, D_MODEL)),
            resident((D_PLE, D_MODEL)),
        ],
        out_specs=pl.BlockSpec((tm, D_MODEL), lambda i: (i, 0)),
        out_shape=jax.ShapeDtypeStruct((s_len, D_MODEL), jnp.float32),
        compiler_params=pltpu.CompilerParams(
            dimension_semantics=("arbitrary",), vmem_limit_bytes=VMEM_LIMIT),
        name="tail",
    )(ya_t, yb_t, x2, p2, ga, gb, w_out, g_post, g1, w1, w2, g2, g3, wg, wp)


def _rope_tables_t(n_tokens):
    tok = jnp.arange(n_tokens)
    row = (tok // GRID_W).astype(jnp.float32)
    col = (tok % GRID_W).astype(jnp.float32)
    n_axis = ROPE_HALF // 2
    inv_freq = ROPE_THETA ** (-jnp.arange(n_axis, dtype=jnp.float32) / n_axis)
    ang = jnp.concatenate([inv_freq[:, None] * row[None, :], inv_freq[:, None] * col[None, :]], axis=0)
    return jnp.cos(ang), jnp.sin(ang)


def _layer(h, p_i, w_in, g_attn_pre, g_q, g_k, g_out_a, g_out_b, w_out, g_attn_post, bias_t,
           g_mlp_pre, w_ff1, w_ff2, g_mlp_post, g_ple, w_ple_gate, w_ple_proj, cos_t, sin_t):
    bf = jnp.bfloat16
    row = lambda g: g.reshape(1, -1)
    col = lambda g: g.reshape(-1, 1)

    qa_t, ka, va_t, qb_t, kb, vb_t, len2 = _proj_call(
        h, row(g_attn_pre), w_in.T.astype(bf), col(g_q), col(g_k), cos_t, sin_t)
    ya_t = _attn_a(qa_t, ka, va_t, len2[0], len2[1])
    yb_t = _attn_b(qb_t, kb, vb_t, bias_t, len2[2], len2[3])
    return _tail_call(ya_t, yb_t, h, p_i, col(g_out_a), col(g_out_b), w_out.astype(bf),
                      row(g_attn_post), row(g_mlp_pre), w_ff1.astype(bf), w_ff2.astype(bf),
                      row(g_mlp_post), row(g_ple), w_ple_gate.astype(bf), w_ple_proj.astype(bf))


def kernel(x, p, w_in, g_attn_pre, g_q, g_k, g_out_a, g_out_b, w_out, g_attn_post, rel_bias,
           g_mlp_pre, w_ff1, w_ff2, g_mlp_post, g_ple, w_ple_gate, w_ple_proj):
    b, s_len, _ = x.shape
    cos_t, sin_t = _rope_tables_t(s_len)
    bias_t = _dilated_bias_table(rel_bias)
    outs = []
    for bi in range(b):
        h = x[bi]
        for i in range(w_in.shape[0]):
            h = _layer(h, p[i, bi], w_in[i], g_attn_pre[i], g_q[i], g_k[i], g_out_a[i], g_out_b[i],
                       w_out[i], g_attn_post[i], bias_t, g_mlp_pre[i], w_ff1[i], w_ff2[i],
                       g_mlp_post[i], g_ple[i], w_ple_gate[i], w_ple_proj[i], cos_t, sin_t)
        outs.append(h)
    return jnp.stack(outs, axis=0)
```

```python
import functools
import math

import jax
import jax.numpy as jnp
import numpy as np
from jax import lax
from jax.experimental import pallas as pl
from jax.experimental.pallas import tpu as pltpu

D_MODEL = 1024
HEAD_DIM = 64
N_HEADS_A = 8
N_KV_A = 2
N_HEADS_B = 8
D_A = N_HEADS_A * HEAD_DIM
D_KV_A = N_KV_A * HEAD_DIM
D_B = N_HEADS_B * HEAD_DIM
D_FF = 4 * D_MODEL
D_PLE = 256
GRID_W = 64
ROPE_THETA = 10000.0
ROPE_HALF = HEAD_DIM // 2
DILATED_PATTERNS = ((128, 1), (512, 4), (2048, 16))
N_BUCKETS = 32
MAX_DISTANCE = 1024
EPS = 1e-6
NEG_BIG = -1e30

LOG2E = math.log2(math.e)
Q_SCALE = HEAD_DIM ** -0.5 * LOG2E

MXU_TILE = 256
ONES_ROWS = 16
V_ROWS = HEAD_DIM + ONES_ROWS
VMEM_LIMIT = 56 * 1024 * 1024

TM_PROJ = 512
BQ_A = 128
NK_A = 256
UNROLL_A = 8
ROUNDING_SLACK = (1.0 + 2.0 ** -8) ** 2
SHIFT_SPAN_LIMIT = 96.0
BQ_B = 256
NK_B = 256
QB_B = 4
B_REACH = max(w for w, _ in DILATED_PATTERNS) // 2
TM_TAIL = 512
FF_CHUNK = 1024

_T_QA, _T_KA, _T_VA = 0, D_A, D_A + D_KV_A
_T_QB, _T_KB, _T_VB = D_A + 2 * D_KV_A, D_A + 2 * D_KV_A + D_B, D_A + 2 * D_KV_A + 2 * D_B
_T_ROWS = D_A + 2 * D_KV_A + 3 * D_B


def _rms_rows(x, g_row):
    ms = jnp.mean(x * x, axis=-1, keepdims=True)
    return x * lax.rsqrt(ms + EPS) * g_row


def _proj_kernel(x_ref, g_ref, wt_ref, gq_ref, gk_ref, cos_ref, sin_ref,
                 qa_ref, ka_ref, va_ref, qb_ref, kb_ref, vb_ref, len_ref):
    xn = _rms_rows(x_ref[...], g_ref[...]).astype(jnp.bfloat16)
    cuts = (0, _T_KA, _T_QB, _T_KB, _T_VB, _T_ROWS)
    groups = {}

    def rows(lo, hi):
        for g_lo, g_hi in zip(cuts[:-1], cuts[1:]):
            if g_lo <= lo and hi <= g_hi:
                if g_lo not in groups:
                    groups[g_lo] = lax.dot_general(wt_ref[g_lo:g_hi, :], xn, (((1,), (1,)), ((), ())),
                                                   preferred_element_type=jnp.float32)
                return groups[g_lo][lo - g_lo:hi - g_lo]
        raise ValueError((lo, hi))

    cos = cos_ref[...]
    sin = sin_ref[...]

    def norm_rope(blk, g_col):
        ms = jnp.mean(blk * blk, axis=0, keepdims=True)
        y = blk * lax.rsqrt(ms + EPS) * g_col
        y1, y2 = y[:ROPE_HALF], y[ROPE_HALF:]
        return jnp.concatenate([y1 * cos - y2 * sin, y2 * cos + y1 * sin], axis=0)

    def longest2(heads):
        out = None
        for v in heads:
            n2 = jnp.sum(v * v, axis=0, keepdims=True)
            out = n2 if out is None else jnp.maximum(out, n2)
        return out

    gq = gq_ref[...]
    gk = gk_ref[...]
    q_heads = []
    for h in range(N_HEADS_A):
        r0 = _T_QA + h * HEAD_DIM
        q_heads.append(norm_rope(rows(r0, r0 + HEAD_DIM), gq) * Q_SCALE)
        qa_ref[h * HEAD_DIM:(h + 1) * HEAD_DIM, :] = q_heads[-1].astype(jnp.bfloat16)
    k_heads = [norm_rope(rows(_T_KA + g * HEAD_DIM, _T_KA + (g + 1) * HEAD_DIM), gk)
               for g in range(N_KV_A)]
    ka_ref[...] = jnp.concatenate(k_heads, axis=0).T.astype(jnp.bfloat16)

    qb = rows(_T_QB, _T_QB + D_B) * Q_SCALE
    kb = rows(_T_KB, _T_KB + D_B)
    qb_ref[...] = qb.astype(jnp.bfloat16)
    kb_ref[...] = kb.T.astype(jnp.bfloat16)
    split = lambda v: [v[h * HEAD_DIM:(h + 1) * HEAD_DIM] for h in range(N_HEADS_B)]
    len_ref[0:1, :] = longest2(q_heads) * ROUNDING_SLACK
    len_ref[1:2, :] = longest2(k_heads) * ROUNDING_SLACK
    len_ref[2:3, :] = longest2(split(qb)) * ROUNDING_SLACK
    len_ref[3:4, :] = longest2(split(kb)) * ROUNDING_SLACK
    len_ref[4:8, :] = jnp.zeros((4, len_ref.shape[1]), jnp.float32)

    tm = x_ref.shape[0]
    def store_slabs(v_ref, v_t, n_heads, nk):
        for c in range(tm // nk):
            for h in range(n_heads):
                r0 = h * V_ROWS
                v_ref[c, r0:r0 + HEAD_DIM, :] = v_t[h * HEAD_DIM:(h + 1) * HEAD_DIM,
                                                    c * nk:(c + 1) * nk]
                v_ref[c, r0 + HEAD_DIM:r0 + V_ROWS, :] = jnp.ones((ONES_ROWS, nk), jnp.bfloat16)

    store_slabs(va_ref, rows(_T_VA, _T_VA + D_KV_A).astype(jnp.bfloat16), N_KV_A, NK_A)
    store_slabs(vb_ref, rows(_T_VB, _T_VB + D_B).astype(jnp.bfloat16), N_HEADS_B, NK_B)


def _proj_call(x2, g_pre, wt, gq, gk, cos_t, sin_t):
    s_len = x2.shape[0]
    tm = TM_PROJ
    const = lambda i: (0, 0)
    return pl.pallas_call(
        _proj_kernel,
        grid=(s_len // tm,),
        in_specs=[
            pl.BlockSpec((tm, D_MODEL), lambda i: (i, 0)),
            pl.BlockSpec((1, D_MODEL), const),
            pl.BlockSpec((_T_ROWS, D_MODEL), const),
            pl.BlockSpec((HEAD_DIM, 1), const),
            pl.BlockSpec((HEAD_DIM, 1), const),
            pl.BlockSpec((ROPE_HALF, tm), lambda i: (0, i)),
            pl.BlockSpec((ROPE_HALF, tm), lambda i: (0, i)),
        ],
        out_specs=[
            pl.BlockSpec((D_A, tm), lambda i: (0, i)),
            pl.BlockSpec((tm, D_KV_A), lambda i: (i, 0)),
            pl.BlockSpec((tm // NK_A, V_ROWS * N_KV_A, NK_A), lambda i: (i, 0, 0)),
            pl.BlockSpec((D_B, tm), lambda i: (0, i)),
            pl.BlockSpec((tm, D_B), lambda i: (i, 0)),
            pl.BlockSpec((tm // NK_B, V_ROWS * N_HEADS_B, NK_B), lambda i: (i, 0, 0)),
            pl.BlockSpec((8, tm), lambda i: (0, i)),
        ],
        out_shape=[
            jax.ShapeDtypeStruct((D_A, s_len), jnp.bfloat16),
            jax.ShapeDtypeStruct((s_len, D_KV_A), jnp.bfloat16),
            jax.ShapeDtypeStruct((s_len // NK_A, V_ROWS * N_KV_A, NK_A),
                                 jnp.bfloat16),
            jax.ShapeDtypeStruct((D_B, s_len), jnp.bfloat16),
            jax.ShapeDtypeStruct((s_len, D_B), jnp.bfloat16),
            jax.ShapeDtypeStruct((s_len // NK_B, V_ROWS * N_HEADS_B, NK_B),
                                 jnp.bfloat16),
            jax.ShapeDtypeStruct((8, s_len), jnp.float32),
        ],
        compiler_params=pltpu.CompilerParams(
            dimension_semantics=("arbitrary",), vmem_limit_bytes=VMEM_LIMIT),
        name="proj",
    )(x2, g_pre, wt, gq, gk, cos_t, sin_t)


def _probs(s, m):
    return jnp.exp2(s - m).astype(jnp.bfloat16)


def _pv_product(p, v_t):
    groups = v_t.shape[0] // V_ROWS
    width = p.shape[1] // groups
    pv = [jnp.dot(v_t[g * V_ROWS:(g + 1) * V_ROWS], p[:, g * width:(g + 1) * width],
                  preferred_element_type=jnp.float32) for g in range(groups)]
    return jnp.concatenate(pv, axis=1)


def _softmax_step(s, v_t, m, acc_ref):
    m_new = jnp.maximum(m, jnp.max(s, axis=0, keepdims=True))
    acc_ref[...] = acc_ref[...] * jnp.exp2(m - m_new) + _pv_product(_probs(s, m_new), v_t)
    return m_new


def _logits_tiles(k_rows, qt_ref, bias_ref, s_ref, n_ct):
    maxima = []
    for ct in range(n_ct):
        cols = slice(ct * MXU_TILE, (ct + 1) * MXU_TILE)
        s = jnp.dot(k_rows, qt_ref[:, cols], preferred_element_type=jnp.float32)
        if bias_ref is not None:
            s = s + bias_ref[:, cols]
        s_ref[:, cols] = s
        maxima.append(jnp.max(s_ref[:, cols], axis=0, keepdims=True))
    return maxima


def _softmax_tiles(s_ref, mx, v_tiles, m, acc_ref):
    m_out = []
    for ct in range(len(mx)):
        cols = slice(ct * MXU_TILE, (ct + 1) * MXU_TILE)
        pv = None
        m_new = mx[ct] if m is None else jnp.maximum(m[ct], mx[ct])
        for r, v_t in enumerate(v_tiles):
            p = _probs(s_ref[r * MXU_TILE:(r + 1) * MXU_TILE, cols], m_new)
            part = jnp.dot(v_t[ct], p, preferred_element_type=jnp.float32)
            pv = part if pv is None else pv + part
        if m is None:
            acc_ref[:, cols] = pv
        else:
            acc_ref[:, cols] = acc_ref[:, cols] * jnp.exp2(m[ct] - m_new) + pv
        m_out.append(m_new)
    return m_out


def _attn_a_kernel(q_ref, k_ref, v_ref, *refs, bounded):
    if bounded:
        klen_ref, o_ref, qt_ref, acc_ref, *s_refs = refs
    else:
        o_ref, qt_ref, acc_ref, *s_refs = refs
    bq = q_ref.shape[1]
    m_cols = N_HEADS_A * bq
    grp = N_HEADS_A // N_KV_A
    zeros = jnp.zeros((HEAD_DIM, bq), jnp.bfloat16)
    q_len = []
    for h in range(N_HEADS_A):
        qh = q_ref[h * HEAD_DIM:(h + 1) * HEAD_DIM, :]
        parts = [qh if g == h // grp else zeros for g in range(N_KV_A)]
        qt_ref[:, h * bq:(h + 1) * bq] = jnp.concatenate(parts, axis=0)
        qf = qh.astype(jnp.float32)
        q_len.append(jnp.sqrt(jnp.sum(qf * qf, axis=0, keepdims=True)))
    acc_ref[...] = jnp.zeros_like(acc_ref)
    n_chunks = k_ref.shape[0] // NK_A
    unroll = n_chunks if bounded else UNROLL_A

    def logits(c):
        start = pl.multiple_of(c * NK_A, NK_A)
        return jnp.dot(k_ref[pl.ds(start, NK_A), :], qt_ref[...],
                       preferred_element_type=jnp.float32)

    s_refs[0][...] = logits(0)

    if bounded:
        shift = jnp.concatenate(q_len, axis=1) * klen_ref[...]

        def body(t, carry):
            c0 = t * unroll
            total = None
            for u in range(unroll):
                nxt = jnp.minimum(c0 + u + 1, n_chunks - 1)
                s_refs[(u + 1) % 2][...] = logits(nxt)
                pv = _pv_product(_probs(s_refs[u % 2][...], shift), v_ref[c0 + u])
                total = pv if total is None else total + pv
            acc_ref[...] += total
            return carry

        lax.fori_loop(0, n_chunks // unroll, body, 0)
    else:
        def body(t, m):
            c0 = t * unroll
            for u in range(unroll):
                nxt = jnp.minimum(c0 + u + 1, n_chunks - 1)
                s_refs[(u + 1) % 2][...] = logits(nxt)
                m = _softmax_step(s_refs[u % 2][...], v_ref[c0 + u], m, acc_ref)
            return m

        lax.fori_loop(0, n_chunks // unroll, body, jnp.full((1, m_cols), NEG_BIG, jnp.float32))
    inv = 1.0 / acc_ref[HEAD_DIM:HEAD_DIM + 1, :]
    for h in range(N_HEADS_A):
        cols = slice(h * bq, (h + 1) * bq)
        o_ref[h * HEAD_DIM:(h + 1) * HEAD_DIM, :] = acc_ref[:HEAD_DIM, cols] * inv[:, cols]


def _attn_a_call(qa_t, ka, va_t, k_len, bounded):
    s_len = ka.shape[0]
    bq = BQ_A
    operands = (qa_t, ka, va_t) + ((k_len,) if bounded else ())
    return pl.pallas_call(
        functools.partial(_attn_a_kernel, bounded=bounded),
        grid=(s_len // bq,),
        in_specs=[
            pl.BlockSpec((D_A, bq), lambda i: (0, i)),
            pl.BlockSpec((s_len, D_KV_A), lambda i: (0, 0)),
            pl.BlockSpec((s_len // NK_A, V_ROWS * N_KV_A, NK_A), lambda i: (0, 0, 0)),
        ] + ([pl.BlockSpec((1, 1), lambda i: (0, 0))] if bounded else []),
        out_specs=pl.BlockSpec((D_A, bq), lambda i: (0, i)),
        out_shape=jax.ShapeDtypeStruct((D_A, s_len), jnp.float32),
        scratch_shapes=[
            pltpu.VMEM((D_KV_A, N_HEADS_A * bq), jnp.bfloat16),
            pltpu.VMEM((V_ROWS, N_HEADS_A * bq), jnp.float32),
        ] + [pltpu.VMEM((NK_A, N_HEADS_A * bq), jnp.float32)] * 2,
        compiler_params=pltpu.CompilerParams(
            dimension_semantics=("arbitrary",), vmem_limit_bytes=VMEM_LIMIT),
        name="attn_a_bounded" if bounded else "attn_a_online",
    )(*operands)


def _attn_b_kernel(q_ref, k_ref, v_ref, frev_ref, *refs, bounded):
    if bounded:
        stats_ref, o_ref, qt_ref, acc_ref, bias_ref, *s_refs = refs
    else:
        o_ref, qt_ref, acc_ref, bias_ref, s_ref = refs
    step = pl.program_id(1)
    bq = BQ_B
    n_chunks = k_ref.shape[0] // NK_B
    back = B_REACH // NK_B
    span = (bq + 2 * B_REACH) // NK_B

    def first_chunk(sb):
        return (step * QB_B + sb) * (bq // NK_B) - back

    def is_interior(sb):
        return jnp.logical_and(first_chunk(sb) >= 0, first_chunk(sb) + span <= n_chunks)

    @pl.when(step == 0)
    def _build_bias():
        for hh in range(2):
            for c in range(span):
                seg = (span - 1 - c) * NK_B
                row = frev_ref[hh:hh + 1, seg:seg + 2 * bq]
                rolled = pltpu.roll(jnp.broadcast_to(row, (NK_B, 2 * bq)), 0, 1,
                                    stride=1, stride_axis=0)
                bias_ref[c * NK_B:(c + 1) * NK_B, hh * bq:(hh + 1) * bq] = rolled[:, bq:]

    def stage_queries(sb):
        zeros = jnp.zeros((HEAD_DIM, bq), jnp.bfloat16)
        q0 = q_ref[:HEAD_DIM, sb * bq:(sb + 1) * bq]
        q1 = q_ref[HEAD_DIM:, sb * bq:(sb + 1) * bq]
        qt_ref[:, 2 * sb * bq:(2 * sb + 1) * bq] = jnp.concatenate([q0, zeros], axis=0)
        qt_ref[:, (2 * sb + 1) * bq:(2 * sb + 2) * bq] = jnp.concatenate([zeros, q1], axis=0)
        return q0, q1

    def write(sb, acc):
        inv = 1.0 / acc[HEAD_DIM:HEAD_DIM + 1, :]
        o_ref[:HEAD_DIM, sb * bq:(sb + 1) * bq] = acc[:HEAD_DIM, :bq] * inv[:, :bq]
        o_ref[HEAD_DIM:, sb * bq:(sb + 1) * bq] = acc[:HEAD_DIM, bq:] * inv[:, bq:]

    def logits(sb, c):
        k_start = pl.multiple_of((first_chunk(sb) + c) * NK_B, NK_B)
        b_start = c * NK_B if isinstance(c, int) else pl.multiple_of(c * NK_B, NK_B)
        s = jnp.dot(k_ref[pl.ds(k_start, NK_B), :], qt_ref[:, 2 * sb * bq:(2 * sb + 2) * bq],
                    preferred_element_type=jnp.float32)
        return s + bias_ref[pl.ds(b_start, NK_B), :]

    def clipped_bounds(sb):
        return jnp.maximum(0, -first_chunk(sb)), jnp.minimum(span, n_chunks - first_chunk(sb))

    if bounded:
        def stage_and_shift(sb):
            stats = stats_ref[...]
            widen = lambda r: jnp.concatenate([stats[r:r + 1, :]] * (bq // stats.shape[1]), axis=1)
            shift = []
            for hh, q in enumerate(stage_queries(sb)):
                qf = q.astype(jnp.float32)
                shift.append(jnp.sqrt(jnp.sum(qf * qf, axis=0, keepdims=True)) * widen(2) + widen(hh))
            return jnp.concatenate(shift, axis=1)

        all_interior = jnp.logical_and(is_interior(0), is_interior(QB_B - 1))

        @pl.when(all_interior)
        def _whole_windows():
            shifts = [stage_and_shift(sb) for sb in range(QB_B)]
            order = [(sb, c) for sb in range(QB_B) for c in range(span)]
            s_refs[0][...] = logits(*order[0])
            total = None
            for n, (sb, c) in enumerate(order):
                if n + 1 < len(order):
                    s_refs[(n + 1) % 2][...] = logits(*order[n + 1])
                pv = _pv_product(_probs(s_refs[n % 2][...], shifts[sb]), v_ref[first_chunk(sb) + c])
                total = pv if c == 0 else total + pv
                if c == span - 1:
                    write(sb, total)

        @pl.when(jnp.logical_not(all_interior))
        def _clipped_windows():
            for sb in range(QB_B):
                acc_ref[...] = jnp.zeros_like(acc_ref)
                shift = stage_and_shift(sb)

                def body(c, carry, sb=sb, shift=shift):
                    acc_ref[...] += _pv_product(_probs(logits(sb, c), shift),
                                                v_ref[first_chunk(sb) + c])
                    return carry

                lax.fori_loop(*clipped_bounds(sb), body, 0)
                write(sb, acc_ref[...])
    else:
        for sb in range(QB_B):
            @pl.when(is_interior(sb))
            def _whole_window(sb=sb):
                stage_queries(sb)
                n_ct = 2 * bq // MXU_TILE
                k_start = pl.multiple_of(first_chunk(sb) * NK_B, NK_B)
                mx = _logits_tiles(k_ref[pl.ds(k_start, span * NK_B), :],
                                   qt_ref.at[:, 2 * sb * bq:(2 * sb + 2) * bq], bias_ref, s_ref, n_ct)
                v_tiles = [[v_ref[first_chunk(sb) + c, hh * V_ROWS:(hh + 1) * V_ROWS, :]
                            for hh in range(n_ct)] for c in range(span)]
                _softmax_tiles(s_ref, mx, v_tiles, None, acc_ref)
                write(sb, acc_ref[...])

            @pl.when(jnp.logical_not(is_interior(sb)))
            def _clipped_window(sb=sb):
                acc_ref[...] = jnp.zeros_like(acc_ref)
                stage_queries(sb)

                def body(c, m):
                    return _softmax_step(logits(sb, c), v_ref[first_chunk(sb) + c], m, acc_ref)

                lax.fori_loop(*clipped_bounds(sb), body,
                              jnp.full((1, 2 * bq), NEG_BIG, jnp.float32))
                write(sb, acc_ref[...])


def _attn_b_call(qb_t, kb, vb_t, frev, stats, bounded):
    s_len = kb.shape[0]
    bq = BQ_B
    pairs = N_HEADS_B // 2
    span_rows = bq + 2 * B_REACH
    assert bq == NK_B == MXU_TILE and frev.shape[-1] == span_rows + bq
    assert B_REACH // bq % QB_B == 0 and (s_len // bq) % QB_B == 0
    operands = (qb_t, kb, vb_t, frev) + ((stats,) if bounded else ())
    logits_scratch = ([pltpu.VMEM((NK_B, 2 * bq), jnp.float32)] * 2 if bounded
                      else [pltpu.VMEM((span_rows, 2 * bq), jnp.float32)])
    return pl.pallas_call(
        functools.partial(_attn_b_kernel, bounded=bounded),
        grid=(pairs, s_len // (bq * QB_B)),
        in_specs=[
            pl.BlockSpec((2 * HEAD_DIM, bq * QB_B), lambda j, i: (j, i)),
            pl.BlockSpec((s_len, 2 * HEAD_DIM), lambda j, i: (0, j)),
            pl.BlockSpec((s_len // NK_B, 2 * V_ROWS, NK_B), lambda j, i: (0, j, 0)),
            pl.BlockSpec((None, 2, span_rows + bq), lambda j, i: (j, 0, 0)),
        ] + ([pl.BlockSpec((None, 8, 128), lambda j, i: (j, 0, 0))] if bounded else []),
        out_specs=pl.BlockSpec((2 * HEAD_DIM, bq * QB_B), lambda j, i: (j, i)),
        out_shape=jax.ShapeDtypeStruct((D_B, s_len), jnp.float32),
        scratch_shapes=[
            pltpu.VMEM((2 * HEAD_DIM, 2 * bq * QB_B), jnp.bfloat16),
            pltpu.VMEM((V_ROWS, 2 * bq), jnp.float32),
            pltpu.VMEM((span_rows, 2 * bq), jnp.float32),
        ] + logits_scratch,
        compiler_params=pltpu.CompilerParams(
            dimension_semantics=("arbitrary", "arbitrary"), vmem_limit_bytes=VMEM_LIMIT),
        name="attn_b_bounded" if bounded else "attn_b_online",
    )(*operands)


def _mixers(qa_t, ka, va_t, qb_t, kb, vb_t, bias, len2):
    frev, bias_max, bias_self = bias
    qa_len, ka_len, qb_len, kb_len = jnp.sqrt(jnp.max(len2[:4], axis=1))
    safe_a = 2.0 * qa_len * ka_len <= SHIFT_SPAN_LIMIT
    safe_b = 2.0 * qb_len * kb_len + jnp.max(bias_max - bias_self) <= SHIFT_SPAN_LIMIT
    pairs = N_HEADS_B // 2
    stats = jnp.concatenate([bias_max.reshape(pairs, 2), jnp.broadcast_to(kb_len, (pairs, 1)),
                             jnp.zeros((pairs, 5), jnp.float32)], axis=1)
    stats = jnp.broadcast_to(stats[:, :, None], (pairs, 8, 128))
    ops_a = (qa_t, ka, va_t, ka_len.reshape(1, 1))
    ops_b = (qb_t, kb, vb_t, frev, stats)
    call_a = lambda bounded: functools.partial(_attn_a_call, bounded=bounded)
    call_b = lambda bounded: functools.partial(_attn_b_call, bounded=bounded)

    def both_bounded(ops_a, ops_b):
        return call_a(True)(*ops_a), call_b(True)(*ops_b)

    def each_as_safe(ops_a, ops_b):
        return (lax.cond(safe_a, call_a(True), call_a(False), *ops_a),
                lax.cond(safe_b, call_b(True), call_b(False), *ops_b))

    return lax.cond(jnp.logical_and(safe_a, safe_b), both_bounded, each_as_safe, ops_a, ops_b)


def _t5_bucket_index(rel):
    nb = N_BUCKETS // 2
    max_exact = nb // 2
    side = jnp.where(rel > 0, nb, 0)
    n = jnp.abs(rel)
    large = max_exact + (jnp.log(jnp.maximum(n, max_exact).astype(jnp.float32) / max_exact)
                         / math.log(MAX_DISTANCE / max_exact) * (nb - max_exact)).astype(jnp.int32)
    large = jnp.minimum(large, nb - 1)
    return side + jnp.where(n < max_exact, n, large)


def _dilated_bias_table(rel_bias):
    bq = BQ_B
    reach_all = B_REACH + bq - 1
    delta = jnp.arange(reach_all, -reach_all - 1, -1)
    count = jnp.zeros(delta.shape, jnp.float32)
    for window, dilation in DILATED_PATTERNS:
        inside = (delta % dilation == 0) & (jnp.abs(delta) <= window // 2)
        count = count + inside.astype(jnp.float32)
    table = rel_bias[_t5_bucket_index(delta)].astype(jnp.float32)
    table = jnp.where((count > 0)[:, None],
                      (table + jnp.log(jnp.maximum(count, 1.0))[:, None]) * LOG2E, NEG_BIG)
    frev = jnp.pad(table.T, ((0, 0), (1, 0)))
    frev = frev.reshape(N_HEADS_B // 2, 2, frev.shape[1])
    bias_max = jnp.max(jnp.where((count > 0)[:, None], table, -jnp.inf), axis=0)
    return frev, bias_max, table[reach_all]


def _tail_kernel(ya_ref, yb_ref, x_ref, p_ref, ga_ref, gb_ref, wo_ref, gpost_ref,
                 g1_ref, w1_ref, w2_ref, g2_ref, g3_ref, wg_ref, wp_ref, o_ref):
    def norm_t(y_t, g_col):
        ms = jnp.mean(y_t * y_t, axis=0, keepdims=True)
        return (y_t * lax.rsqrt(ms + EPS) * g_col).astype(jnp.bfloat16)

    y_t = jnp.concatenate([norm_t(ya_ref[...], ga_ref[...]),
                           norm_t(yb_ref[...], gb_ref[...])], axis=0)
    y = lax.dot_general(y_t, wo_ref[...], (((0,), (0,)), ((), ())),
                        preferred_element_type=jnp.float32)
    h = x_ref[...] + _rms_rows(y, gpost_ref[...])

    xn = _rms_rows(h, g1_ref[...]).astype(jnp.bfloat16)
    f = None
    for c in range(D_FF // FF_CHUNK):
        cols = slice(c * FF_CHUNK, (c + 1) * FF_CHUNK)
        u = jnp.dot(xn, w1_ref[:, cols], preferred_element_type=jnp.float32)
        u = jnp.square(jnp.maximum(u, 0.0)).astype(jnp.bfloat16)
        part = jnp.dot(u, w2_ref[cols, :], preferred_element_type=jnp.float32)
        f = part if f is None else f + part
    h = h + _rms_rows(f, g2_ref[...])
    gate_in = _rms_rows(h, g3_ref[...]).astype(jnp.bfloat16)
    gate = jax.nn.sigmoid(jnp.dot(gate_in, wg_ref[...], preferred_element_type=jnp.float32))
    emb = jnp.dot(p_ref[...].astype(jnp.bfloat16), wp_ref[...], preferred_element_type=jnp.float32)
    o_ref[...] = h + gate * emb


def _tail_call(ya_t, yb_t, x2, p2, ga, gb, w_out, g_post, g1, w1, w2, g2, g3, wg, wp):
    s_len = x2.shape[0]
    tm = TM_TAIL
    const = lambda i: (0, 0)
    resident = functools.partial(pl.BlockSpec, index_map=const, pipeline_mode=pl.Buffered(1))
    row_vec = pl.BlockSpec((1, D_MODEL), const)
    return pl.pallas_call(
        _tail_kernel,
        grid=(s_len // tm,),
        in_specs=[
            pl.BlockSpec((D_A, tm), lambda i: (0, i)),
            pl.BlockSpec((D_B, tm), lambda i: (0, i)),
            pl.BlockSpec((tm, D_MODEL), lambda i: (i, 0)),
            pl.BlockSpec((tm, D_PLE), lambda i: (i, 0)),
            pl.BlockSpec((D_A, 1), const),
            pl.BlockSpec((D_B, 1), const),
            resident((D_A + D_B, D_MODEL)),
            row_vec,
            row_vec,
            resident((D_MODEL, D_FF)),
            resident((D_FF, D_MODEL)),
            row_vec,
            row_vec,
            resident((D_MODEL, D_MODEL)),
            resident((D_PLE, D_MODEL)),
        ],
        out_specs=pl.BlockSpec((tm, D_MODEL), lambda i: (i, 0)),
        out_shape=jax.ShapeDtypeStruct((s_len, D_MODEL), jnp.float32),
        compiler_params=pltpu.CompilerParams(
            dimension_semantics=("arbitrary",), vmem_limit_bytes=VMEM_LIMIT),
        name="tail",
    )(ya_t, yb_t, x2, p2, ga, gb, w_out, g_post, g1, w1, w2, g2, g3, wg, wp)


def _rope_tables_t(n_tokens):
    tok = np.arange(n_tokens)
    n_axis = ROPE_HALF // 2
    inv_freq = ROPE_THETA ** (-np.arange(n_axis, dtype=np.float64) / n_axis)
    ang = np.concatenate([inv_freq[:, None] * (tok // GRID_W)[None, :],
                          inv_freq[:, None] * (tok % GRID_W)[None, :]], axis=0)
    return jnp.asarray(np.cos(ang), jnp.float32), jnp.asarray(np.sin(ang), jnp.float32)


def _layer(h, p_i, w_in, g_attn_pre, g_q, g_k, g_out_a, g_out_b, w_out, g_attn_post, bias_t,
           g_mlp_pre, w_ff1, w_ff2, g_mlp_post, g_ple, w_ple_gate, w_ple_proj, cos_t, sin_t):
    bf = jnp.bfloat16
    row = lambda g: g.reshape(1, -1)
    col = lambda g: g.reshape(-1, 1)

    qa_t, ka, va_t, qb_t, kb, vb_t, len2 = _proj_call(
        h, row(g_attn_pre), w_in.T.astype(bf), col(g_q), col(g_k), cos_t, sin_t)
    ya_t, yb_t = _mixers(qa_t, ka, va_t, qb_t, kb, vb_t, bias_t, len2)
    return _tail_call(ya_t, yb_t, h, p_i, col(g_out_a), col(g_out_b), w_out.astype(bf),
                      row(g_attn_post), row(g_mlp_pre), w_ff1.astype(bf), w_ff2.astype(bf),
                      row(g_mlp_post), row(g_ple), w_ple_gate.astype(bf), w_ple_proj.astype(bf))


def kernel(x, p, w_in, g_attn_pre, g_q, g_k, g_out_a, g_out_b, w_out, g_attn_post, rel_bias,
           g_mlp_pre, w_ff1, w_ff2, g_mlp_post, g_ple, w_ple_gate, w_ple_proj):
    b, s_len, _ = x.shape
    cos_t, sin_t = _rope_tables_t(s_len)
    bias_t = _dilated_bias_table(rel_bias)
    outs = []
    for bi in range(b):
        h = x[bi]
        for i in range(w_in.shape[0]):
            h = _layer(h, p[i, bi], w_in[i], g_attn_pre[i], g_q[i], g_k[i], g_out_a[i], g_out_b[i],
                       w_out[i], g_attn_post[i], bias_t, g_mlp_pre[i], w_ff1[i], w_ff2[i],
                       g_mlp_post[i], g_ple[i], w_ple_gate[i], w_ple_proj[i], cos_t, sin_t)
        outs.append(h)
    return jnp.stack(outs, axis=0)
```

```python
import functools
import math

import jax
import jax.numpy as jnp
import numpy as np
from jax import lax
from jax.experimental import pallas as pl
from jax.experimental.pallas import tpu as pltpu

D_MODEL = 1024
HEAD_DIM = 64
N_HEADS_A = 8
N_KV_A = 2
N_HEADS_B = 8
D_A = N_HEADS_A * HEAD_DIM
D_KV_A = N_KV_A * HEAD_DIM
D_B = N_HEADS_B * HEAD_DIM
D_FF = 4 * D_MODEL
D_PLE = 256
GRID_W = 64
ROPE_THETA = 10000.0
ROPE_HALF = HEAD_DIM // 2
DILATED_PATTERNS = ((128, 1), (512, 4), (2048, 16))
N_BUCKETS = 32
MAX_DISTANCE = 1024
EPS = 1e-6
NEG_BIG = -1e30

LOG2E = math.log2(math.e)
Q_SCALE = HEAD_DIM ** -0.5 * LOG2E

MXU_TILE = 256
ONES_ROWS = 16
V_ROWS = HEAD_DIM + ONES_ROWS
VMEM_LIMIT = 56 * 1024 * 1024

TM_PROJ = 512
BQ_A = 128
NK_A = 256
UNROLL_A = 8
ROUNDING_SLACK = (1.0 + 2.0 ** -8) ** 2
SHIFT_SPAN_LIMIT = 96.0
BQ_B = 256
NK_B = 256
QB_B = 4
B_REACH = max(w for w, _ in DILATED_PATTERNS) // 2
TM_TAIL = 512
FF_CHUNK = 1024

_T_QA, _T_KA, _T_VA = 0, D_A, D_A + D_KV_A
_T_QB, _T_KB, _T_VB = D_A + 2 * D_KV_A, D_A + 2 * D_KV_A + D_B, D_A + 2 * D_KV_A + 2 * D_B
_T_ROWS = D_A + 2 * D_KV_A + 3 * D_B


def _rms_rows(x, g_row):
    ms = jnp.mean(x * x, axis=-1, keepdims=True)
    return x * lax.rsqrt(ms + EPS) * g_row


def _proj_kernel(x_ref, g_ref, wt_ref, gq_ref, gk_ref, cos_ref, sin_ref,
                 qa_ref, ka_ref, va_ref, qb_ref, kb_ref, vb_ref, len_ref):
    xn = _rms_rows(x_ref[...], g_ref[...]).astype(jnp.bfloat16)
    cuts = (0, _T_KA, _T_QB, _T_KB, _T_VB, _T_ROWS)
    groups = {}

    def rows(lo, hi):
        for g_lo, g_hi in zip(cuts[:-1], cuts[1:]):
            if g_lo <= lo and hi <= g_hi:
                if g_lo not in groups:
                    groups[g_lo] = lax.dot_general(wt_ref[g_lo:g_hi, :], xn, (((1,), (1,)), ((), ())),
                                                   preferred_element_type=jnp.float32)
                return groups[g_lo][lo - g_lo:hi - g_lo]
        raise ValueError((lo, hi))

    cos = cos_ref[...]
    sin = sin_ref[...]

    def norm_rope(blk, g_col):
        ms = jnp.mean(blk * blk, axis=0, keepdims=True)
        y = blk * lax.rsqrt(ms + EPS) * g_col
        y1, y2 = y[:ROPE_HALF], y[ROPE_HALF:]
        return jnp.concatenate([y1 * cos - y2 * sin, y2 * cos + y1 * sin], axis=0)

    def longest2(heads):
        out = None
        for v in heads:
            n2 = jnp.sum(v * v, axis=0, keepdims=True)
            out = n2 if out is None else jnp.maximum(out, n2)
        return out

    gq = gq_ref[...]
    gk = gk_ref[...]
    q_heads = []
    for h in range(N_HEADS_A):
        r0 = _T_QA + h * HEAD_DIM
        q_heads.append(norm_rope(rows(r0, r0 + HEAD_DIM), gq) * Q_SCALE)
        qa_ref[h * HEAD_DIM:(h + 1) * HEAD_DIM, :] = q_heads[-1].astype(jnp.bfloat16)
    k_heads = [norm_rope(rows(_T_KA + g * HEAD_DIM, _T_KA + (g + 1) * HEAD_DIM), gk)
               for g in range(N_KV_A)]
    ka_ref[...] = jnp.concatenate(k_heads, axis=0).T.astype(jnp.bfloat16)

    qb = rows(_T_QB, _T_QB + D_B) * Q_SCALE
    kb = rows(_T_KB, _T_KB + D_B)
    qb_ref[...] = qb.astype(jnp.bfloat16)
    kb_ref[...] = kb.T.astype(jnp.bfloat16)
    split = lambda v: [v[h * HEAD_DIM:(h + 1) * HEAD_DIM] for h in range(N_HEADS_B)]
    len_ref[0:1, :] = longest2(q_heads) * ROUNDING_SLACK
    len_ref[1:2, :] = longest2(k_heads) * ROUNDING_SLACK
    len_ref[2:3, :] = longest2(split(qb)) * ROUNDING_SLACK
    len_ref[3:4, :] = longest2(split(kb)) * ROUNDING_SLACK
    len_ref[4:8, :] = jnp.zeros((4, len_ref.shape[1]), jnp.float32)

    tm = x_ref.shape[0]
    def store_slabs(v_ref, v_t, n_heads, nk):
        for c in range(tm // nk):
            for h in range(n_heads):
                r0 = h * V_ROWS
                v_ref[c, r0:r0 + HEAD_DIM, :] = v_t[h * HEAD_DIM:(h + 1) * HEAD_DIM,
                                                    c * nk:(c + 1) * nk]
                v_ref[c, r0 + HEAD_DIM:r0 + V_ROWS, :] = jnp.ones((ONES_ROWS, nk), jnp.bfloat16)

    store_slabs(va_ref, rows(_T_VA, _T_VA + D_KV_A).astype(jnp.bfloat16), N_KV_A, NK_A)
    store_slabs(vb_ref, rows(_T_VB, _T_VB + D_B).astype(jnp.bfloat16), N_HEADS_B, NK_B)


def _proj_call(x2, g_pre, wt, gq, gk, cos_t, sin_t):
    s_len = x2.shape[0]
    tm = TM_PROJ
    const = lambda i: (0, 0)
    return pl.pallas_call(
        _proj_kernel,
        grid=(s_len // tm,),
        in_specs=[
            pl.BlockSpec((tm, D_MODEL), lambda i: (i, 0)),
            pl.BlockSpec((1, D_MODEL), const),
            pl.BlockSpec((_T_ROWS, D_MODEL), const),
            pl.BlockSpec((HEAD_DIM, 1), const),
            pl.BlockSpec((HEAD_DIM, 1), const),
            pl.BlockSpec((ROPE_HALF, tm), lambda i: (0, i)),
            pl.BlockSpec((ROPE_HALF, tm), lambda i: (0, i)),
        ],
        out_specs=[
            pl.BlockSpec((D_A, tm), lambda i: (0, i)),
            pl.BlockSpec((tm, D_KV_A), lambda i: (i, 0)),
            pl.BlockSpec((tm // NK_A, V_ROWS * N_KV_A, NK_A), lambda i: (i, 0, 0)),
            pl.BlockSpec((D_B, tm), lambda i: (0, i)),
            pl.BlockSpec((tm, D_B), lambda i: (i, 0)),
            pl.BlockSpec((tm // NK_B, V_ROWS * N_HEADS_B, NK_B), lambda i: (i, 0, 0)),
            pl.BlockSpec((8, tm), lambda i: (0, i)),
        ],
        out_shape=[
            jax.ShapeDtypeStruct((D_A, s_len), jnp.bfloat16),
            jax.ShapeDtypeStruct((s_len, D_KV_A), jnp.bfloat16),
            jax.ShapeDtypeStruct((s_len // NK_A, V_ROWS * N_KV_A, NK_A),
                                 jnp.bfloat16),
            jax.ShapeDtypeStruct((D_B, s_len), jnp.bfloat16),
            jax.ShapeDtypeStruct((s_len, D_B), jnp.bfloat16),
            jax.ShapeDtypeStruct((s_len // NK_B, V_ROWS * N_HEADS_B, NK_B),
                                 jnp.bfloat16),
            jax.ShapeDtypeStruct((8, s_len), jnp.float32),
        ],
        compiler_params=pltpu.CompilerParams(
            dimension_semantics=("arbitrary",), vmem_limit_bytes=VMEM_LIMIT),
        name="proj",
    )(x2, g_pre, wt, gq, gk, cos_t, sin_t)


def _probs(s, m):
    return jnp.exp2(s - m).astype(jnp.bfloat16)


def _pv_product(p, v_t):
    groups = v_t.shape[0] // V_ROWS
    width = p.shape[1] // groups
    pv = [jnp.dot(v_t[g * V_ROWS:(g + 1) * V_ROWS], p[:, g * width:(g + 1) * width],
                  preferred_element_type=jnp.float32) for g in range(groups)]
    return jnp.concatenate(pv, axis=1)


def _softmax_step(s, v_t, m, acc_ref):
    m_new = jnp.maximum(m, jnp.max(s, axis=0, keepdims=True))
    acc_ref[...] = acc_ref[...] * jnp.exp2(m - m_new) + _pv_product(_probs(s, m_new), v_t)
    return m_new


def _logits_tiles(k_rows, qt_ref, bias_ref, s_ref, n_ct):
    maxima = []
    for ct in range(n_ct):
        cols = slice(ct * MXU_TILE, (ct + 1) * MXU_TILE)
        s = jnp.dot(k_rows, qt_ref[:, cols], preferred_element_type=jnp.float32)
        if bias_ref is not None:
            s = s + bias_ref[:, cols]
        s_ref[:, cols] = s
        maxima.append(jnp.max(s_ref[:, cols], axis=0, keepdims=True))
    return maxima


def _softmax_tiles(s_ref, mx, v_tiles, m, acc_ref):
    m_out = []
    for ct in range(len(mx)):
        cols = slice(ct * MXU_TILE, (ct + 1) * MXU_TILE)
        pv = None
        m_new = mx[ct] if m is None else jnp.maximum(m[ct], mx[ct])
        for r, v_t in enumerate(v_tiles):
            p = _probs(s_ref[r * MXU_TILE:(r + 1) * MXU_TILE, cols], m_new)
            part = jnp.dot(v_t[ct], p, preferred_element_type=jnp.float32)
            pv = part if pv is None else pv + part
        if m is None:
            acc_ref[:, cols] = pv
        else:
            acc_ref[:, cols] = acc_ref[:, cols] * jnp.exp2(m[ct] - m_new) + pv
        m_out.append(m_new)
    return m_out


def _attn_a_kernel(q_ref, k_ref, v_ref, *refs, bounded):
    if bounded:
        klen_ref, o_ref, qt_ref, acc_ref, *s_refs = refs
    else:
        o_ref, qt_ref, acc_ref, *s_refs = refs
    bq = q_ref.shape[1]
    m_cols = N_HEADS_A * bq
    grp = N_HEADS_A // N_KV_A
    zeros = jnp.zeros((HEAD_DIM, bq), jnp.bfloat16)
    q_len = []
    for h in range(N_HEADS_A):
        qh = q_ref[h * HEAD_DIM:(h + 1) * HEAD_DIM, :]
        parts = [qh if g == h // grp else zeros for g in range(N_KV_A)]
        qt_ref[:, h * bq:(h + 1) * bq] = jnp.concatenate(parts, axis=0)
        qf = qh.astype(jnp.float32)
        q_len.append(jnp.sqrt(jnp.sum(qf * qf, axis=0, keepdims=True)))
    acc_ref[...] = jnp.zeros_like(acc_ref)
    n_chunks = k_ref.shape[0] // NK_A
    unroll = n_chunks if bounded else UNROLL_A

    def logits(c):
        start = pl.multiple_of(c * NK_A, NK_A)
        return jnp.dot(k_ref[pl.ds(start, NK_A), :], qt_ref[...],
                       preferred_element_type=jnp.float32)

    s_refs[0][...] = logits(0)

    if bounded:
        shift = jnp.concatenate(q_len, axis=1) * klen_ref[...]

        def body(t, carry):
            c0 = t * unroll
            total = None
            for u in range(unroll):
                nxt = jnp.minimum(c0 + u + 1, n_chunks - 1)
                s_refs[(u + 1) % 2][...] = logits(nxt)
                pv = _pv_product(_probs(s_refs[u % 2][...], shift), v_ref[c0 + u])
                total = pv if total is None else total + pv
            acc_ref[...] += total
            return carry

        lax.fori_loop(0, n_chunks // unroll, body, 0)
    else:
        def body(t, m):
            c0 = t * unroll
            for u in range(unroll):
                nxt = jnp.minimum(c0 + u + 1, n_chunks - 1)
                s_refs[(u + 1) % 2][...] = logits(nxt)
                m = _softmax_step(s_refs[u % 2][...], v_ref[c0 + u], m, acc_ref)
            return m

        lax.fori_loop(0, n_chunks // unroll, body, jnp.full((1, m_cols), NEG_BIG, jnp.float32))
    inv = 1.0 / acc_ref[HEAD_DIM:HEAD_DIM + 1, :]
    for h in range(N_HEADS_A):
        cols = slice(h * bq, (h + 1) * bq)
        o_ref[h * HEAD_DIM:(h + 1) * HEAD_DIM, :] = acc_ref[:HEAD_DIM, cols] * inv[:, cols]


def _attn_a_call(qa_t, ka, va_t, k_len, bounded):
    s_len = ka.shape[0]
    bq = BQ_A
    operands = (qa_t, ka, va_t) + ((k_len,) if bounded else ())
    return pl.pallas_call(
        functools.partial(_attn_a_kernel, bounded=bounded),
        grid=(s_len // bq,),
        in_specs=[
            pl.BlockSpec((D_A, bq), lambda i: (0, i)),
            pl.BlockSpec((s_len, D_KV_A), lambda i: (0, 0)),
            pl.BlockSpec((s_len // NK_A, V_ROWS * N_KV_A, NK_A), lambda i: (0, 0, 0)),
        ] + ([pl.BlockSpec((1, 1), lambda i: (0, 0))] if bounded else []),
        out_specs=pl.BlockSpec((D_A, bq), lambda i: (0, i)),
        out_shape=jax.ShapeDtypeStruct((D_A, s_len), jnp.float32),
        scratch_shapes=[
            pltpu.VMEM((D_KV_A, N_HEADS_A * bq), jnp.bfloat16),
            pltpu.VMEM((V_ROWS, N_HEADS_A * bq), jnp.float32),
        ] + [pltpu.VMEM((NK_A, N_HEADS_A * bq), jnp.float32)] * 2,
        compiler_params=pltpu.CompilerParams(
            dimension_semantics=("arbitrary",), vmem_limit_bytes=VMEM_LIMIT),
        name="attn_a_bounded" if bounded else "attn_a_online",
    )(*operands)


def _attn_b_kernel(q_ref, k_ref, v_ref, frev_ref, *refs, bounded):
    if bounded:
        stats_ref, o_ref, qt_ref, acc_ref, bias_ref, *s_refs = refs
    else:
        o_ref, qt_ref, acc_ref, bias_ref, s_ref = refs
    step = pl.program_id(1)
    bq = BQ_B
    n_chunks = k_ref.shape[0] // NK_B
    back = B_REACH // NK_B
    span = (bq + 2 * B_REACH) // NK_B

    def first_chunk(sb):
        return (step * QB_B + sb) * (bq // NK_B) - back

    def is_interior(sb):
        return jnp.logical_and(first_chunk(sb) >= 0, first_chunk(sb) + span <= n_chunks)

    @pl.when(step == 0)
    def _build_bias():
        for hh in range(2):
            for c in range(span):
                seg = (span - 1 - c) * NK_B
                row = frev_ref[hh:hh + 1, seg:seg + 2 * bq]
                rolled = pltpu.roll(jnp.broadcast_to(row, (NK_B, 2 * bq)), 0, 1,
                                    stride=1, stride_axis=0)
                bias_ref[c * NK_B:(c + 1) * NK_B, hh * bq:(hh + 1) * bq] = rolled[:, bq:]

    def stage_queries(sb):
        zeros = jnp.zeros((HEAD_DIM, bq), jnp.bfloat16)
        q0 = q_ref[:HEAD_DIM, sb * bq:(sb + 1) * bq]
        q1 = q_ref[HEAD_DIM:, sb * bq:(sb + 1) * bq]
        qt_ref[:, 2 * sb * bq:(2 * sb + 1) * bq] = jnp.concatenate([q0, zeros], axis=0)
        qt_ref[:, (2 * sb + 1) * bq:(2 * sb + 2) * bq] = jnp.concatenate([zeros, q1], axis=0)
        return q0, q1

    def write(sb, acc):
        inv = 1.0 / acc[HEAD_DIM:HEAD_DIM + 1, :]
        o_ref[:HEAD_DIM, sb * bq:(sb + 1) * bq] = acc[:HEAD_DIM, :bq] * inv[:, :bq]
        o_ref[HEAD_DIM:, sb * bq:(sb + 1) * bq] = acc[:HEAD_DIM, bq:] * inv[:, bq:]

    def logits(sb, c):
        k_start = pl.multiple_of((first_chunk(sb) + c) * NK_B, NK_B)
        b_start = c * NK_B if isinstance(c, int) else pl.multiple_of(c * NK_B, NK_B)
        s = jnp.dot(k_ref[pl.ds(k_start, NK_B), :], qt_ref[:, 2 * sb * bq:(2 * sb + 2) * bq],
                    preferred_element_type=jnp.float32)
        return s + bias_ref[pl.ds(b_start, NK_B), :]

    def clipped_bounds(sb):
        return jnp.maximum(0, -first_chunk(sb)), jnp.minimum(span, n_chunks - first_chunk(sb))

    if bounded:
        def stage_and_shift(sb):
            stats = stats_ref[...]
            widen = lambda r: jnp.concatenate([stats[r:r + 1, :]] * (bq // stats.shape[1]), axis=1)
            shift = []
            for hh, q in enumerate(stage_queries(sb)):
                qf = q.astype(jnp.float32)
                shift.append(jnp.sqrt(jnp.sum(qf * qf, axis=0, keepdims=True)) * widen(2) + widen(hh))
            return jnp.concatenate(shift, axis=1)

        def pipeline(chunks_of):
            shifts = [stage_and_shift(sb) for sb in range(QB_B)]
            order = [(sb, c) for sb in range(QB_B) for c in chunks_of(sb)]
            s_refs[0][...] = logits(*order[0])
            total = None
            for n, (sb, c) in enumerate(order):
                if n + 1 < len(order):
                    s_refs[(n + 1) % 2][...] = logits(*order[n + 1])
                pv = _pv_product(_probs(s_refs[n % 2][...], shifts[sb]), v_ref[first_chunk(sb) + c])
                total = pv if c == chunks_of(sb)[0] else total + pv
                if c == chunks_of(sb)[-1]:
                    write(sb, total)

        last = pl.num_programs(1) - 1
        blocks_back = back * NK_B // bq

        @pl.when(step == 0)
        def _sequence_start():
            pipeline(lambda sb: range((blocks_back - sb) * (bq // NK_B), span))

        @pl.when(step == last)
        def _sequence_end():
            pipeline(lambda sb: range(min(span, (QB_B - sb) * (bq // NK_B) + back)))

        @pl.when(jnp.logical_and(step > 0, step < last))
        def _whole_windows():
            pipeline(lambda sb: range(span))
    else:
        for sb in range(QB_B):
            @pl.when(is_interior(sb))
            def _whole_window(sb=sb):
                stage_queries(sb)
                n_ct = 2 * bq // MXU_TILE
                k_start = pl.multiple_of(first_chunk(sb) * NK_B, NK_B)
                mx = _logits_tiles(k_ref[pl.ds(k_start, span * NK_B), :],
                                   qt_ref.at[:, 2 * sb * bq:(2 * sb + 2) * bq], bias_ref, s_ref, n_ct)
                v_tiles = [[v_ref[first_chunk(sb) + c, hh * V_ROWS:(hh + 1) * V_ROWS, :]
                            for hh in range(n_ct)] for c in range(span)]
                _softmax_tiles(s_ref, mx, v_tiles, None, acc_ref)
                write(sb, acc_ref[...])

            @pl.when(jnp.logical_not(is_interior(sb)))
            def _clipped_window(sb=sb):
                acc_ref[...] = jnp.zeros_like(acc_ref)
                stage_queries(sb)

                def body(c, m):
                    return _softmax_step(logits(sb, c), v_ref[first_chunk(sb) + c], m, acc_ref)

                lax.fori_loop(*clipped_bounds(sb), body,
                              jnp.full((1, 2 * bq), NEG_BIG, jnp.float32))
                write(sb, acc_ref[...])


def _attn_b_call(qb_t, kb, vb_t, frev, stats, bounded):
    s_len = kb.shape[0]
    bq = BQ_B
    pairs = N_HEADS_B // 2
    span_rows = bq + 2 * B_REACH
    assert bq == NK_B == MXU_TILE and frev.shape[-1] == span_rows + bq
    assert B_REACH == bq * QB_B and s_len % (bq * QB_B) == 0 and s_len // (bq * QB_B) >= 2
    operands = (qb_t, kb, vb_t, frev) + ((stats,) if bounded else ())
    logits_scratch = ([pltpu.VMEM((NK_B, 2 * bq), jnp.float32)] * 2 if bounded
                      else [pltpu.VMEM((span_rows, 2 * bq), jnp.float32)])
    return pl.pallas_call(
        functools.partial(_attn_b_kernel, bounded=bounded),
        grid=(pairs, s_len // (bq * QB_B)),
        in_specs=[
            pl.BlockSpec((2 * HEAD_DIM, bq * QB_B), lambda j, i: (j, i)),
            pl.BlockSpec((s_len, 2 * HEAD_DIM), lambda j, i: (0, j)),
            pl.BlockSpec((s_len // NK_B, 2 * V_ROWS, NK_B), lambda j, i: (0, j, 0)),
            pl.BlockSpec((None, 2, span_rows + bq), lambda j, i: (j, 0, 0)),
        ] + ([pl.BlockSpec((None, 8, 128), lambda j, i: (j, 0, 0))] if bounded else []),
        out_specs=pl.BlockSpec((2 * HEAD_DIM, bq * QB_B), lambda j, i: (j, i)),
        out_shape=jax.ShapeDtypeStruct((D_B, s_len), jnp.float32),
        scratch_shapes=[
            pltpu.VMEM((2 * HEAD_DIM, 2 * bq * QB_B), jnp.bfloat16),
            pltpu.VMEM((V_ROWS, 2 * bq), jnp.float32),
            pltpu.VMEM((span_rows, 2 * bq), jnp.float32),
        ] + logits_scratch,
        compiler_params=pltpu.CompilerParams(
            dimension_semantics=("arbitrary", "arbitrary"), vmem_limit_bytes=VMEM_LIMIT),
        name="attn_b_bounded" if bounded else "attn_b_online",
    )(*operands)


def _mixers(qa_t, ka, va_t, qb_t, kb, vb_t, bias, len2):
    frev, bias_max, bias_self = bias
    qa_len, ka_len, qb_len, kb_len = jnp.sqrt(jnp.max(len2[:4], axis=1))
    safe_a = 2.0 * qa_len * ka_len <= SHIFT_SPAN_LIMIT
    safe_b = 2.0 * qb_len * kb_len + jnp.max(bias_max - bias_self) <= SHIFT_SPAN_LIMIT
    pairs = N_HEADS_B // 2
    stats = jnp.concatenate([bias_max.reshape(pairs, 2), jnp.broadcast_to(kb_len, (pairs, 1)),
                             jnp.zeros((pairs, 5), jnp.float32)], axis=1)
    stats = jnp.broadcast_to(stats[:, :, None], (pairs, 8, 128))
    ops_a = (qa_t, ka, va_t, ka_len.reshape(1, 1))
    ops_b = (qb_t, kb, vb_t, frev, stats)
    call_a = lambda bounded: functools.partial(_attn_a_call, bounded=bounded)
    call_b = lambda bounded: functools.partial(_attn_b_call, bounded=bounded)

    def both_bounded(ops_a, ops_b):
        return call_a(True)(*ops_a), call_b(True)(*ops_b)

    def each_as_safe(ops_a, ops_b):
        return (lax.cond(safe_a, call_a(True), call_a(False), *ops_a),
                lax.cond(safe_b, call_b(True), call_b(False), *ops_b))

    return lax.cond(jnp.logical_and(safe_a, safe_b), both_bounded, each_as_safe, ops_a, ops_b)


def _t5_bucket_index(rel):
    nb = N_BUCKETS // 2
    max_exact = nb // 2
    side = jnp.where(rel > 0, nb, 0)
    n = jnp.abs(rel)
    large = max_exact + (jnp.log(jnp.maximum(n, max_exact).astype(jnp.float32) / max_exact)
                         / math.log(MAX_DISTANCE / max_exact) * (nb - max_exact)).astype(jnp.int32)
    large = jnp.minimum(large, nb - 1)
    return side + jnp.where(n < max_exact, n, large)


def _dilated_bias_table(rel_bias):
    bq = BQ_B
    reach_all = B_REACH + bq - 1
    delta = jnp.arange(reach_all, -reach_all - 1, -1)
    count = jnp.zeros(delta.shape, jnp.float32)
    for window, dilation in DILATED_PATTERNS:
        inside = (delta % dilation == 0) & (jnp.abs(delta) <= window // 2)
        count = count + inside.astype(jnp.float32)
    table = rel_bias[_t5_bucket_index(delta)].astype(jnp.float32)
    table = jnp.where((count > 0)[:, None],
                      (table + jnp.log(jnp.maximum(count, 1.0))[:, None]) * LOG2E, NEG_BIG)
    frev = jnp.pad(table.T, ((0, 0), (1, 0)))
    frev = frev.reshape(N_HEADS_B // 2, 2, frev.shape[1])
    bias_max = jnp.max(jnp.where((count > 0)[:, None], table, -jnp.inf), axis=0)
    return frev, bias_max, table[reach_all]


def _tail_kernel(ya_ref, yb_ref, x_ref, p_ref, ga_ref, gb_ref, wo_ref, gpost_ref,
                 g1_ref, w1_ref, w2_ref, g2_ref, g3_ref, wg_ref, wp_ref, o_ref):
    def norm_t(y_t, g_col):
        ms = jnp.mean(y_t * y_t, axis=0, keepdims=True)
        return (y_t * lax.rsqrt(ms + EPS) * g_col).astype(jnp.bfloat16)

    y_t = jnp.concatenate([norm_t(ya_ref[...], ga_ref[...]),
                           norm_t(yb_ref[...], gb_ref[...])], axis=0)
    y = lax.dot_general(y_t, wo_ref[...], (((0,), (0,)), ((), ())),
                        preferred_element_type=jnp.float32)
    h = x_ref[...] + _rms_rows(y, gpost_ref[...])

    xn = _rms_rows(h, g1_ref[...]).astype(jnp.bfloat16)
    f = None
    for c in range(D_FF // FF_CHUNK):
        cols = slice(c * FF_CHUNK, (c + 1) * FF_CHUNK)
        u = jnp.dot(xn, w1_ref[:, cols], preferred_element_type=jnp.float32)
        u = jnp.square(jnp.maximum(u, 0.0)).astype(jnp.bfloat16)
        part = jnp.dot(u, w2_ref[cols, :], preferred_element_type=jnp.float32)
        f = part if f is None else f + part
    h = h + _rms_rows(f, g2_ref[...])
    gate_in = _rms_rows(h, g3_ref[...]).astype(jnp.bfloat16)
    gate = jax.nn.sigmoid(jnp.dot(gate_in, wg_ref[...], preferred_element_type=jnp.float32))
    emb = jnp.dot(p_ref[...].astype(jnp.bfloat16), wp_ref[...], preferred_element_type=jnp.float32)
    o_ref[...] = h + gate * emb


def _tail_call(ya_t, yb_t, x2, p2, ga, gb, w_out, g_post, g1, w1, w2, g2, g3, wg, wp):
    s_len = x2.shape[0]
    tm = TM_TAIL
    const = lambda i: (0, 0)
    resident = functools.partial(pl.BlockSpec, index_map=const, pipeline_mode=pl.Buffered(1))
    row_vec = pl.BlockSpec((1, D_MODEL), const)
    return pl.pallas_call(
        _tail_kernel,
        grid=(s_len // tm,),
        in_specs=[
            pl.BlockSpec((D_A, tm), lambda i: (0, i)),
            pl.BlockSpec((D_B, tm), lambda i: (0, i)),
            pl.BlockSpec((tm, D_MODEL), lambda i: (i, 0)),
            pl.BlockSpec((tm, D_PLE), lambda i: (i, 0)),
            pl.BlockSpec((D_A, 1), const),
            pl.BlockSpec((D_B, 1), const),
            resident((D_A + D_B, D_MODEL)),
            row_vec,
            row_vec,
            resident((D_MODEL, D_FF)),
            resident((D_FF, D_MODEL)),
            row_vec,
            row_vec,
            resident((D_MODEL, D_MODEL)),
            resident((D_PLE, D_MODEL)),
        ],
        out_specs=pl.BlockSpec((tm, D_MODEL), lambda i: (i, 0)),
        out_shape=jax.ShapeDtypeStruct((s_len, D_MODEL), jnp.float32),
        compiler_params=pltpu.CompilerParams(
            dimension_semantics=("arbitrary",), vmem_limit_bytes=VMEM_LIMIT),
        name="tail",
    )(ya_t, yb_t, x2, p2, ga, gb, w_out, g_post, g1, w1, w2, g2, g3, wg, wp)


def _rope_tables_t(n_tokens):
    tok = np.arange(n_tokens)
    n_axis = ROPE_HALF // 2
    inv_freq = ROPE_THETA ** (-np.arange(n_axis, dtype=np.float64) / n_axis)
    ang = np.concatenate([inv_freq[:, None] * (tok // GRID_W)[None, :],
                          inv_freq[:, None] * (tok % GRID_W)[None, :]], axis=0)
    return jnp.asarray(np.cos(ang), jnp.float32), jnp.asarray(np.sin(ang), jnp.float32)


def _layer(h, p_i, w_in, g_attn_pre, g_q, g_k, g_out_a, g_out_b, w_out, g_attn_post, bias_t,
           g_mlp_pre, w_ff1, w_ff2, g_mlp_post, g_ple, w_ple_gate, w_ple_proj, cos_t, sin_t):
    bf = jnp.bfloat16
    row = lambda g: g.reshape(1, -1)
    col = lambda g: g.reshape(-1, 1)

    qa_t, ka, va_t, qb_t, kb, vb_t, len2 = _proj_call(
        h, row(g_attn_pre), w_in.T.astype(bf), col(g_q), col(g_k), cos_t, sin_t)
    ya_t, yb_t = _mixers(qa_t, ka, va_t, qb_t, kb, vb_t, bias_t, len2)
    return _tail_call(ya_t, yb_t, h, p_i, col(g_out_a), col(g_out_b), w_out.astype(bf),
                      row(g_attn_post), row(g_mlp_pre), w_ff1.astype(bf), w_ff2.astype(bf),
                      row(g_mlp_post), row(g_ple), w_ple_gate.astype(bf), w_ple_proj.astype(bf))


def kernel(x, p, w_in, g_attn_pre, g_q, g_k, g_out_a, g_out_b, w_out, g_attn_post, rel_bias,
           g_mlp_pre, w_ff1, w_ff2, g_mlp_post, g_ple, w_ple_gate, w_ple_proj):
    b, s_len, _ = x.shape
    cos_t, sin_t = _rope_tables_t(s_len)
    bias_t = _dilated_bias_table(rel_bias)
    outs = []
    for bi in range(b):
        h = x[bi]
        for i in range(w_in.shape[0]):
            h = _layer(h, p[i, bi], w_in[i], g_attn_pre[i], g_q[i], g_k[i], g_out_a[i], g_out_b[i],
                       w_out[i], g_attn_post[i], bias_t, g_mlp_pre[i], w_ff1[i], w_ff2[i],
                       g_mlp_post[i], g_ple[i], w_ple_gate[i], w_ple_proj[i], cos_t, sin_t)
        outs.append(h)
    return jnp.stack(outs, axis=0)
```

```python
import functools
import math

import jax
import jax.numpy as jnp
import numpy as np
from jax import lax
from jax.experimental import pallas as pl
from jax.experimental.pallas import tpu as pltpu

D_MODEL = 1024
HEAD_DIM = 64
N_HEADS_A = 8
N_KV_A = 2
N_HEADS_B = 8
D_A = N_HEADS_A * HEAD_DIM
D_KV_A = N_KV_A * HEAD_DIM
D_B = N_HEADS_B * HEAD_DIM
D_FF = 4 * D_MODEL
D_PLE = 256
GRID_W = 64
ROPE_THETA = 10000.0
ROPE_HALF = HEAD_DIM // 2
DILATED_PATTERNS = ((128, 1), (512, 4), (2048, 16))
N_BUCKETS = 32
MAX_DISTANCE = 1024
EPS = 1e-6
NEG_BIG = -1e30

LOG2E = math.log2(math.e)
Q_SCALE = HEAD_DIM ** -0.5 * LOG2E

MXU_TILE = 256
ONES_ROWS = 16
V_ROWS = HEAD_DIM + ONES_ROWS
VMEM_LIMIT = 56 * 1024 * 1024

TM_PROJ = 512
BQ_A = 128
NK_A = 256
UNROLL_A = 8
ROUNDING_SLACK = (1.0 + 2.0 ** -8) ** 2
SHIFT_SPAN_LIMIT = 96.0
BQ_B = 256
NK_B = 256
QB_B = 4
B_REACH = max(w for w, _ in DILATED_PATTERNS) // 2
TM_TAIL = 512
FF_CHUNK = 1024

_T_QA, _T_KA, _T_VA = 0, D_A, D_A + D_KV_A
_T_QB, _T_KB, _T_VB = D_A + 2 * D_KV_A, D_A + 2 * D_KV_A + D_B, D_A + 2 * D_KV_A + 2 * D_B
_T_ROWS = D_A + 2 * D_KV_A + 3 * D_B


def _rms_rows(x, g_row):
    ms = jnp.mean(x * x, axis=-1, keepdims=True)
    return x * lax.rsqrt(ms + EPS) * g_row


def _proj_kernel(x_ref, g_ref, wt_ref, gq_ref, gk_ref, cos_ref, sin_ref,
                 qa_ref, ka_ref, va_ref, qb_ref, kb_ref, vb_ref, len_ref):
    xn = _rms_rows(x_ref[...], g_ref[...]).astype(jnp.bfloat16)
    cuts = (0, _T_KA, _T_QB, _T_KB, _T_VB, _T_ROWS)
    groups = {}

    def rows(lo, hi):
        for g_lo, g_hi in zip(cuts[:-1], cuts[1:]):
            if g_lo <= lo and hi <= g_hi:
                if g_lo not in groups:
                    groups[g_lo] = lax.dot_general(wt_ref[g_lo:g_hi, :], xn, (((1,), (1,)), ((), ())),
                                                   preferred_element_type=jnp.float32)
                return groups[g_lo][lo - g_lo:hi - g_lo]
        raise ValueError((lo, hi))

    cos = cos_ref[...]
    sin = sin_ref[...]

    def norm_rope(blk, g_col):
        ms = jnp.mean(blk * blk, axis=0, keepdims=True)
        y = blk * lax.rsqrt(ms + EPS) * g_col
        y1, y2 = y[:ROPE_HALF], y[ROPE_HALF:]
        return jnp.concatenate([y1 * cos - y2 * sin, y2 * cos + y1 * sin], axis=0)

    def longest2(heads):
        out = None
        for v in heads:
            n2 = jnp.sum(v * v, axis=0, keepdims=True)
            out = n2 if out is None else jnp.maximum(out, n2)
        return out

    gq = gq_ref[...]
    gk = gk_ref[...]
    q_heads = []
    for h in range(N_HEADS_A):
        r0 = _T_QA + h * HEAD_DIM
        q_heads.append(norm_rope(rows(r0, r0 + HEAD_DIM), gq) * Q_SCALE)
        qa_ref[h * HEAD_DIM:(h + 1) * HEAD_DIM, :] = q_heads[-1].astype(jnp.bfloat16)
    k_heads = [norm_rope(rows(_T_KA + g * HEAD_DIM, _T_KA + (g + 1) * HEAD_DIM), gk)
               for g in range(N_KV_A)]
    ka_ref[...] = jnp.concatenate(k_heads, axis=0).T.astype(jnp.bfloat16)

    qb = rows(_T_QB, _T_QB + D_B) * Q_SCALE
    kb = rows(_T_KB, _T_KB + D_B)
    qb_ref[...] = qb.astype(jnp.bfloat16)
    kb_ref[...] = kb.T.astype(jnp.bfloat16)
    split = lambda v: [v[h * HEAD_DIM:(h + 1) * HEAD_DIM] for h in range(N_HEADS_B)]
    len_ref[0:1, :] = longest2(q_heads) * ROUNDING_SLACK
    len_ref[1:2, :] = longest2(k_heads) * ROUNDING_SLACK
    len_ref[2:3, :] = longest2(split(qb)) * ROUNDING_SLACK
    len_ref[3:4, :] = longest2(split(kb)) * ROUNDING_SLACK
    len_ref[4:8, :] = jnp.zeros((4, len_ref.shape[1]), jnp.float32)

    tm = x_ref.shape[0]
    def store_slabs(v_ref, v_t, n_heads, nk):
        for c in range(tm // nk):
            for h in range(n_heads):
                r0 = h * V_ROWS
                v_ref[c, r0:r0 + HEAD_DIM, :] = v_t[h * HEAD_DIM:(h + 1) * HEAD_DIM,
                                                    c * nk:(c + 1) * nk]
                v_ref[c, r0 + HEAD_DIM:r0 + V_ROWS, :] = jnp.ones((ONES_ROWS, nk), jnp.bfloat16)

    store_slabs(va_ref, rows(_T_VA, _T_VA + D_KV_A).astype(jnp.bfloat16), N_KV_A, NK_A)
    store_slabs(vb_ref, rows(_T_VB, _T_VB + D_B).astype(jnp.bfloat16), N_HEADS_B, NK_B)


def _proj_call(x2, g_pre, wt, gq, gk, cos_t, sin_t):
    s_len = x2.shape[0]
    tm = TM_PROJ
    const = lambda i: (0, 0)
    return pl.pallas_call(
        _proj_kernel,
        grid=(s_len // tm,),
        in_specs=[
            pl.BlockSpec((tm, D_MODEL), lambda i: (i, 0)),
            pl.BlockSpec((1, D_MODEL), const),
            pl.BlockSpec((_T_ROWS, D_MODEL), const),
            pl.BlockSpec((HEAD_DIM, 1), const),
            pl.BlockSpec((HEAD_DIM, 1), const),
            pl.BlockSpec((ROPE_HALF, tm), lambda i: (0, i)),
            pl.BlockSpec((ROPE_HALF, tm), lambda i: (0, i)),
        ],
        out_specs=[
            pl.BlockSpec((D_A, tm), lambda i: (0, i)),
            pl.BlockSpec((tm, D_KV_A), lambda i: (i, 0)),
            pl.BlockSpec((tm // NK_A, V_ROWS * N_KV_A, NK_A), lambda i: (i, 0, 0)),
            pl.BlockSpec((D_B, tm), lambda i: (0, i)),
            pl.BlockSpec((tm, D_B), lambda i: (i, 0)),
            pl.BlockSpec((tm // NK_B, V_ROWS * N_HEADS_B, NK_B), lambda i: (i, 0, 0)),
            pl.BlockSpec((8, tm), lambda i: (0, i)),
        ],
        out_shape=[
            jax.ShapeDtypeStruct((D_A, s_len), jnp.bfloat16),
            jax.ShapeDtypeStruct((s_len, D_KV_A), jnp.bfloat16),
            jax.ShapeDtypeStruct((s_len // NK_A, V_ROWS * N_KV_A, NK_A),
                                 jnp.bfloat16),
            jax.ShapeDtypeStruct((D_B, s_len), jnp.bfloat16),
            jax.ShapeDtypeStruct((s_len, D_B), jnp.bfloat16),
            jax.ShapeDtypeStruct((s_len // NK_B, V_ROWS * N_HEADS_B, NK_B),
                                 jnp.bfloat16),
            jax.ShapeDtypeStruct((8, s_len), jnp.float32),
        ],
        compiler_params=pltpu.CompilerParams(
            dimension_semantics=("arbitrary",), vmem_limit_bytes=VMEM_LIMIT),
        name="proj",
    )(x2, g_pre, wt, gq, gk, cos_t, sin_t)


def _probs(s, m):
    return jnp.exp2(s - m).astype(jnp.bfloat16)


def _pv_product(p, v_t):
    groups = v_t.shape[0] // V_ROWS
    width = p.shape[1] // groups
    pv = [jnp.dot(v_t[g * V_ROWS:(g + 1) * V_ROWS], p[:, g * width:(g + 1) * width],
                  preferred_element_type=jnp.float32) for g in range(groups)]
    return jnp.concatenate(pv, axis=1)


def _softmax_step(s, v_t, m, acc_ref):
    m_new = jnp.maximum(m, jnp.max(s, axis=0, keepdims=True))
    acc_ref[...] = acc_ref[...] * jnp.exp2(m - m_new) + _pv_product(_probs(s, m_new), v_t)
    return m_new


def _logits_tiles(k_rows, qt_ref, bias_ref, s_ref, n_ct):
    maxima = []
    for ct in range(n_ct):
        cols = slice(ct * MXU_TILE, (ct + 1) * MXU_TILE)
        s = jnp.dot(k_rows, qt_ref[:, cols], preferred_element_type=jnp.float32)
        if bias_ref is not None:
            s = s + bias_ref[:, cols]
        s_ref[:, cols] = s
        maxima.append(jnp.max(s_ref[:, cols], axis=0, keepdims=True))
    return maxima


def _softmax_tiles(s_ref, mx, v_tiles, m, acc_ref):
    m_out = []
    for ct in range(len(mx)):
        cols = slice(ct * MXU_TILE, (ct + 1) * MXU_TILE)
        pv = None
        m_new = mx[ct] if m is None else jnp.maximum(m[ct], mx[ct])
        for r, v_t in enumerate(v_tiles):
            p = _probs(s_ref[r * MXU_TILE:(r + 1) * MXU_TILE, cols], m_new)
            part = jnp.dot(v_t[ct], p, preferred_element_type=jnp.float32)
            pv = part if pv is None else pv + part
        if m is None:
            acc_ref[:, cols] = pv
        else:
            acc_ref[:, cols] = acc_ref[:, cols] * jnp.exp2(m[ct] - m_new) + pv
        m_out.append(m_new)
    return m_out


def _attn_a_kernel(q_ref, k_ref, v_ref, *refs, bounded):
    if bounded:
        klen_ref, o_ref, qt_ref, acc_ref, *s_refs = refs
    else:
        o_ref, qt_ref, acc_ref, *s_refs = refs
    bq = q_ref.shape[1]
    m_cols = N_HEADS_A * bq
    grp = N_HEADS_A // N_KV_A
    zeros = jnp.zeros((HEAD_DIM, bq), jnp.bfloat16)
    q_len = []
    for h in range(N_HEADS_A):
        qh = q_ref[h * HEAD_DIM:(h + 1) * HEAD_DIM, :]
        parts = [qh if g == h // grp else zeros for g in range(N_KV_A)]
        qt_ref[:, h * bq:(h + 1) * bq] = jnp.concatenate(parts, axis=0)
        qf = qh.astype(jnp.float32)
        q_len.append(jnp.sqrt(jnp.sum(qf * qf, axis=0, keepdims=True)))
    n_chunks = k_ref.shape[0] // NK_A

    def logits(c):
        start = c * NK_A if isinstance(c, int) else pl.multiple_of(c * NK_A, NK_A)
        return jnp.dot(k_ref[pl.ds(start, NK_A), :], qt_ref[...],
                       preferred_element_type=jnp.float32)

    s_refs[0][...] = logits(0)

    if bounded:
        shift = jnp.concatenate(q_len, axis=1) * klen_ref[...]
        acc = None
        for c in range(n_chunks):
            if c + 1 < n_chunks:
                s_refs[(c + 1) % 2][...] = logits(c + 1)
            pv = _pv_product(_probs(s_refs[c % 2][...], shift), v_ref[c])
            acc = pv if acc is None else acc + pv
    else:
        acc_ref[...] = jnp.zeros_like(acc_ref)

        def body(t, m):
            c0 = t * UNROLL_A
            for u in range(UNROLL_A):
                nxt = jnp.minimum(c0 + u + 1, n_chunks - 1)
                s_refs[(u + 1) % 2][...] = logits(nxt)
                m = _softmax_step(s_refs[u % 2][...], v_ref[c0 + u], m, acc_ref)
            return m

        lax.fori_loop(0, n_chunks // UNROLL_A, body, jnp.full((1, m_cols), NEG_BIG, jnp.float32))
        acc = acc_ref[...]
    inv = 1.0 / acc[HEAD_DIM:HEAD_DIM + 1, :]
    for h in range(N_HEADS_A):
        cols = slice(h * bq, (h + 1) * bq)
        o_ref[h * HEAD_DIM:(h + 1) * HEAD_DIM, :] = acc[:HEAD_DIM, cols] * inv[:, cols]


def _attn_a_call(qa_t, ka, va_t, k_len, bounded):
    s_len = ka.shape[0]
    bq = BQ_A
    operands = (qa_t, ka, va_t) + ((k_len,) if bounded else ())
    return pl.pallas_call(
        functools.partial(_attn_a_kernel, bounded=bounded),
        grid=(s_len // bq,),
        in_specs=[
            pl.BlockSpec((D_A, bq), lambda i: (0, i)),
            pl.BlockSpec((s_len, D_KV_A), lambda i: (0, 0)),
            pl.BlockSpec((s_len // NK_A, V_ROWS * N_KV_A, NK_A), lambda i: (0, 0, 0)),
        ] + ([pl.BlockSpec((1, 1), lambda i: (0, 0))] if bounded else []),
        out_specs=pl.BlockSpec((D_A, bq), lambda i: (0, i)),
        out_shape=jax.ShapeDtypeStruct((D_A, s_len), jnp.float32),
        scratch_shapes=[
            pltpu.VMEM((D_KV_A, N_HEADS_A * bq), jnp.bfloat16),
            pltpu.VMEM((V_ROWS, N_HEADS_A * bq), jnp.float32),
        ] + [pltpu.VMEM((NK_A, N_HEADS_A * bq), jnp.float32)] * 2,
        compiler_params=pltpu.CompilerParams(
            dimension_semantics=("arbitrary",), vmem_limit_bytes=VMEM_LIMIT),
        name="attn_a_bounded" if bounded else "attn_a_online",
    )(*operands)


def _attn_b_kernel(q_ref, k_ref, v_ref, frev_ref, *refs, bounded):
    if bounded:
        stats_ref, o_ref, qt_ref, acc_ref, bias_ref, *s_refs = refs
    else:
        o_ref, qt_ref, acc_ref, bias_ref, s_ref = refs
    step = pl.program_id(1)
    bq = BQ_B
    n_chunks = k_ref.shape[0] // NK_B
    back = B_REACH // NK_B
    span = (bq + 2 * B_REACH) // NK_B

    def first_chunk(sb):
        return (step * QB_B + sb) * (bq // NK_B) - back

    def is_interior(sb):
        return jnp.logical_and(first_chunk(sb) >= 0, first_chunk(sb) + span <= n_chunks)

    @pl.when(step == 0)
    def _build_bias():
        for hh in range(2):
            for c in range(span):
                seg = (span - 1 - c) * NK_B
                row = frev_ref[hh:hh + 1, seg:seg + 2 * bq]
                rolled = pltpu.roll(jnp.broadcast_to(row, (NK_B, 2 * bq)), 0, 1,
                                    stride=1, stride_axis=0)
                bias_ref[c * NK_B:(c + 1) * NK_B, hh * bq:(hh + 1) * bq] = rolled[:, bq:]

    def stage_queries(sb):
        zeros = jnp.zeros((HEAD_DIM, bq), jnp.bfloat16)
        q0 = q_ref[:HEAD_DIM, sb * bq:(sb + 1) * bq]
        q1 = q_ref[HEAD_DIM:, sb * bq:(sb + 1) * bq]
        qt_ref[:, 2 * sb * bq:(2 * sb + 1) * bq] = jnp.concatenate([q0, zeros], axis=0)
        qt_ref[:, (2 * sb + 1) * bq:(2 * sb + 2) * bq] = jnp.concatenate([zeros, q1], axis=0)
        return q0, q1

    def write(sb, acc):
        inv = 1.0 / acc[HEAD_DIM:HEAD_DIM + 1, :]
        o_ref[:HEAD_DIM, sb * bq:(sb + 1) * bq] = acc[:HEAD_DIM, :bq] * inv[:, :bq]
        o_ref[HEAD_DIM:, sb * bq:(sb + 1) * bq] = acc[:HEAD_DIM, bq:] * inv[:, bq:]

    def logits(sb, c):
        k_start = pl.multiple_of((first_chunk(sb) + c) * NK_B, NK_B)
        b_start = c * NK_B if isinstance(c, int) else pl.multiple_of(c * NK_B, NK_B)
        s = jnp.dot(k_ref[pl.ds(k_start, NK_B), :], qt_ref[:, 2 * sb * bq:(2 * sb + 2) * bq],
                    preferred_element_type=jnp.float32)
        return s + bias_ref[pl.ds(b_start, NK_B), :]

    def clipped_bounds(sb):
        return jnp.maximum(0, -first_chunk(sb)), jnp.minimum(span, n_chunks - first_chunk(sb))

    if bounded:
        def stage_and_shift(sb):
            stats = stats_ref[...]
            widen = lambda r: jnp.concatenate([stats[r:r + 1, :]] * (bq // stats.shape[1]), axis=1)
            shift = []
            for hh, q in enumerate(stage_queries(sb)):
                qf = q.astype(jnp.float32)
                shift.append(jnp.sqrt(jnp.sum(qf * qf, axis=0, keepdims=True)) * widen(2) + widen(hh))
            return jnp.concatenate(shift, axis=1)

        def pipeline(chunks_of):
            shifts = [stage_and_shift(sb) for sb in range(QB_B)]
            order = [(sb, c) for sb in range(QB_B) for c in chunks_of(sb)]
            s_refs[0][...] = logits(*order[0])
            total = None
            for n, (sb, c) in enumerate(order):
                if n + 1 < len(order):
                    s_refs[(n + 1) % 2][...] = logits(*order[n + 1])
                pv = _pv_product(_probs(s_refs[n % 2][...], shifts[sb]), v_ref[first_chunk(sb) + c])
                total = pv if c == chunks_of(sb)[0] else total + pv
                if c == chunks_of(sb)[-1]:
                    write(sb, total)

        last = pl.num_programs(1) - 1
        blocks_back = back * NK_B // bq

        @pl.when(step == 0)
        def _sequence_start():
            pipeline(lambda sb: range((blocks_back - sb) * (bq // NK_B), span))

        @pl.when(step == last)
        def _sequence_end():
            pipeline(lambda sb: range(min(span, (QB_B - sb) * (bq // NK_B) + back)))

        @pl.when(jnp.logical_and(step > 0, step < last))
        def _whole_windows():
            pipeline(lambda sb: range(span))
    else:
        for sb in range(QB_B):
            @pl.when(is_interior(sb))
            def _whole_window(sb=sb):
                stage_queries(sb)
                n_ct = 2 * bq // MXU_TILE
                k_start = pl.multiple_of(first_chunk(sb) * NK_B, NK_B)
                mx = _logits_tiles(k_ref[pl.ds(k_start, span * NK_B), :],
                                   qt_ref.at[:, 2 * sb * bq:(2 * sb + 2) * bq], bias_ref, s_ref, n_ct)
                v_tiles = [[v_ref[first_chunk(sb) + c, hh * V_ROWS:(hh + 1) * V_ROWS, :]
                            for hh in range(n_ct)] for c in range(span)]
                _softmax_tiles(s_ref, mx, v_tiles, None, acc_ref)
                write(sb, acc_ref[...])

            @pl.when(jnp.logical_not(is_interior(sb)))
            def _clipped_window(sb=sb):
                acc_ref[...] = jnp.zeros_like(acc_ref)
                stage_queries(sb)

                def body(c, m):
                    return _softmax_step(logits(sb, c), v_ref[first_chunk(sb) + c], m, acc_ref)

                lax.fori_loop(*clipped_bounds(sb), body,
                              jnp.full((1, 2 * bq), NEG_BIG, jnp.float32))
                write(sb, acc_ref[...])


def _attn_b_call(qb_t, kb, vb_t, frev, stats, bounded):
    s_len = kb.shape[0]
    bq = BQ_B
    pairs = N_HEADS_B // 2
    span_rows = bq + 2 * B_REACH
    assert bq == NK_B == MXU_TILE and frev.shape[-1] == span_rows + bq
    assert B_REACH == bq * QB_B and s_len % (bq * QB_B) == 0 and s_len // (bq * QB_B) >= 2
    operands = (qb_t, kb, vb_t, frev) + ((stats,) if bounded else ())
    logits_scratch = ([pltpu.VMEM((NK_B, 2 * bq), jnp.float32)] * 2 if bounded
                      else [pltpu.VMEM((span_rows, 2 * bq), jnp.float32)])
    return pl.pallas_call(
        functools.partial(_attn_b_kernel, bounded=bounded),
        grid=(pairs, s_len // (bq * QB_B)),
        in_specs=[
            pl.BlockSpec((2 * HEAD_DIM, bq * QB_B), lambda j, i: (j, i)),
            pl.BlockSpec((s_len, 2 * HEAD_DIM), lambda j, i: (0, j)),
            pl.BlockSpec((s_len // NK_B, 2 * V_ROWS, NK_B), lambda j, i: (0, j, 0)),
            pl.BlockSpec((None, 2, span_rows + bq), lambda j, i: (j, 0, 0)),
        ] + ([pl.BlockSpec((None, 8, 128), lambda j, i: (j, 0, 0))] if bounded else []),
        out_specs=pl.BlockSpec((2 * HEAD_DIM, bq * QB_B), lambda j, i: (j, i)),
        out_shape=jax.ShapeDtypeStruct((D_B, s_len), jnp.float32),
        scratch_shapes=[
            pltpu.VMEM((2 * HEAD_DIM, 2 * bq * QB_B), jnp.bfloat16),
            pltpu.VMEM((V_ROWS, 2 * bq), jnp.float32),
            pltpu.VMEM((span_rows, 2 * bq), jnp.float32),
        ] + logits_scratch,
        compiler_params=pltpu.CompilerParams(
            dimension_semantics=("arbitrary", "arbitrary"), vmem_limit_bytes=VMEM_LIMIT),
        name="attn_b_bounded" if bounded else "attn_b_online",
    )(*operands)


def _mixers(qa_t, ka, va_t, qb_t, kb, vb_t, bias, len2):
    frev, bias_max, bias_self = bias
    qa_len, ka_len, qb_len, kb_len = jnp.sqrt(jnp.max(len2[:4], axis=1))
    safe_a = 2.0 * qa_len * ka_len <= SHIFT_SPAN_LIMIT
    safe_b = 2.0 * qb_len * kb_len + jnp.max(bias_max - bias_self) <= SHIFT_SPAN_LIMIT
    pairs = N_HEADS_B // 2
    stats = jnp.concatenate([bias_max.reshape(pairs, 2), jnp.broadcast_to(kb_len, (pairs, 1)),
                             jnp.zeros((pairs, 5), jnp.float32)], axis=1)
    stats = jnp.broadcast_to(stats[:, :, None], (pairs, 8, 128))
    ops_a = (qa_t, ka, va_t, ka_len.reshape(1, 1))
    ops_b = (qb_t, kb, vb_t, frev, stats)
    call_a = lambda bounded: functools.partial(_attn_a_call, bounded=bounded)
    call_b = lambda bounded: functools.partial(_attn_b_call, bounded=bounded)

    def both_bounded(ops_a, ops_b):
        return call_a(True)(*ops_a), call_b(True)(*ops_b)

    def each_as_safe(ops_a, ops_b):
        return (lax.cond(safe_a, call_a(True), call_a(False), *ops_a),
                lax.cond(safe_b, call_b(True), call_b(False), *ops_b))

    return lax.cond(jnp.logical_and(safe_a, safe_b), both_bounded, each_as_safe, ops_a, ops_b)


def _t5_bucket_index(rel):
    nb = N_BUCKETS // 2
    max_exact = nb // 2
    side = jnp.where(rel > 0, nb, 0)
    n = jnp.abs(rel)
    large = max_exact + (jnp.log(jnp.maximum(n, max_exact).astype(jnp.float32) / max_exact)
                         / math.log(MAX_DISTANCE / max_exact) * (nb - max_exact)).astype(jnp.int32)
    large = jnp.minimum(large, nb - 1)
    return side + jnp.where(n < max_exact, n, large)


def _dilated_bias_table(rel_bias):
    bq = BQ_B
    reach_all = B_REACH + bq - 1
    delta = jnp.arange(reach_all, -reach_all - 1, -1)
    count = jnp.zeros(delta.shape, jnp.float32)
    for window, dilation in DILATED_PATTERNS:
        inside = (delta % dilation == 0) & (jnp.abs(delta) <= window // 2)
        count = count + inside.astype(jnp.float32)
    table = rel_bias[_t5_bucket_index(delta)].astype(jnp.float32)
    table = jnp.where((count > 0)[:, None],
                      (table + jnp.log(jnp.maximum(count, 1.0))[:, None]) * LOG2E, NEG_BIG)
    frev = jnp.pad(table.T, ((0, 0), (1, 0)))
    frev = frev.reshape(N_HEADS_B // 2, 2, frev.shape[1])
    bias_max = jnp.max(jnp.where((count > 0)[:, None], table, -jnp.inf), axis=0)
    return frev, bias_max, table[reach_all]


def _tail_kernel(ya_ref, yb_ref, x_ref, p_ref, ga_ref, gb_ref, wo_ref, gpost_ref,
                 g1_ref, w1_ref, w2_ref, g2_ref, g3_ref, wg_ref, wp_ref, o_ref):
    def norm_t(y_t, g_col):
        ms = jnp.mean(y_t * y_t, axis=0, keepdims=True)
        return (y_t * lax.rsqrt(ms + EPS) * g_col).astype(jnp.bfloat16)

    y_t = jnp.concatenate([norm_t(ya_ref[...], ga_ref[...]),
                           norm_t(yb_ref[...], gb_ref[...])], axis=0)
    y = lax.dot_general(y_t, wo_ref[...], (((0,), (0,)), ((), ())),
                        preferred_element_type=jnp.float32)
    h = x_ref[...] + _rms_rows(y, gpost_ref[...])

    xn = _rms_rows(h, g1_ref[...]).astype(jnp.bfloat16)
    f = None
    for c in range(D_FF // FF_CHUNK):
        cols = slice(c * FF_CHUNK, (c + 1) * FF_CHUNK)
        u = jnp.dot(xn, w1_ref[:, cols], preferred_element_type=jnp.float32)
        u = jnp.square(jnp.maximum(u, 0.0)).astype(jnp.bfloat16)
        part = jnp.dot(u, w2_ref[cols, :], preferred_element_type=jnp.float32)
        f = part if f is None else f + part
    h = h + _rms_rows(f, g2_ref[...])
    gate_in = _rms_rows(h, g3_ref[...]).astype(jnp.bfloat16)
    gate = jax.nn.sigmoid(jnp.dot(gate_in, wg_ref[...], preferred_element_type=jnp.float32))
    emb = jnp.dot(p_ref[...].astype(jnp.bfloat16), wp_ref[...], preferred_element_type=jnp.float32)
    o_ref[...] = h + gate * emb


def _tail_call(ya_t, yb_t, x2, p2, ga, gb, w_out, g_post, g1, w1, w2, g2, g3, wg, wp):
    s_len = x2.shape[0]
    tm = TM_TAIL
    const = lambda i: (0, 0)
    resident = functools.partial(pl.BlockSpec, index_map=const, pipeline_mode=pl.Buffered(1))
    row_vec = pl.BlockSpec((1, D_MODEL), const)
    return pl.pallas_call(
        _tail_kernel,
        grid=(s_len // tm,),
        in_specs=[
            pl.BlockSpec((D_A, tm), lambda i: (0, i)),
            pl.BlockSpec((D_B, tm), lambda i: (0, i)),
            pl.BlockSpec((tm, D_MODEL), lambda i: (i, 0)),
            pl.BlockSpec((tm, D_PLE), lambda i: (i, 0)),
            pl.BlockSpec((D_A, 1), const),
            pl.BlockSpec((D_B, 1), const),
            resident((D_A + D_B, D_MODEL)),
            row_vec,
            row_vec,
            resident((D_MODEL, D_FF)),
            resident((D_FF, D_MODEL)),
            row_vec,
            row_vec,
            resident((D_MODEL, D_MODEL)),
            resident((D_PLE, D_MODEL)),
        ],
        out_specs=pl.BlockSpec((tm, D_MODEL), lambda i: (i, 0)),
        out_shape=jax.ShapeDtypeStruct((s_len, D_MODEL), jnp.float32),
        compiler_params=pltpu.CompilerParams(
            dimension_semantics=("arbitrary",), vmem_limit_bytes=VMEM_LIMIT),
        name="tail",
    )(ya_t, yb_t, x2, p2, ga, gb, w_out, g_post, g1, w1, w2, g2, g3, wg, wp)


def _rope_tables_t(n_tokens):
    tok = np.arange(n_tokens)
    n_axis = ROPE_HALF // 2
    inv_freq = ROPE_THETA ** (-np.arange(n_axis, dtype=np.float64) / n_axis)
    ang = np.concatenate([inv_freq[:, None] * (tok // GRID_W)[None, :],
                          inv_freq[:, None] * (tok % GRID_W)[None, :]], axis=0)
    return jnp.asarray(np.cos(ang), jnp.float32), jnp.asarray(np.sin(ang), jnp.float32)


def _layer(h, p_i, w_in, g_attn_pre, g_q, g_k, g_out_a, g_out_b, w_out, g_attn_post, bias_t,
           g_mlp_pre, w_ff1, w_ff2, g_mlp_post, g_ple, w_ple_gate, w_ple_proj, cos_t, sin_t):
    bf = jnp.bfloat16
    row = lambda g: g.reshape(1, -1)
    col = lambda g: g.reshape(-1, 1)

    qa_t, ka, va_t, qb_t, kb, vb_t, len2 = _proj_call(
        h, row(g_attn_pre), w_in.T.astype(bf), col(g_q), col(g_k), cos_t, sin_t)
    ya_t, yb_t = _mixers(qa_t, ka, va_t, qb_t, kb, vb_t, bias_t, len2)
    return _tail_call(ya_t, yb_t, h, p_i, col(g_out_a), col(g_out_b), w_out.astype(bf),
                      row(g_attn_post), row(g_mlp_pre), w_ff1.astype(bf), w_ff2.astype(bf),
                      row(g_mlp_post), row(g_ple), w_ple_gate.astype(bf), w_ple_proj.astype(bf))


def kernel(x, p, w_in, g_attn_pre, g_q, g_k, g_out_a, g_out_b, w_out, g_attn_post, rel_bias,
           g_mlp_pre, w_ff1, w_ff2, g_mlp_post, g_ple, w_ple_gate, w_ple_proj):
    b, s_len, _ = x.shape
    cos_t, sin_t = _rope_tables_t(s_len)
    bias_t = _dilated_bias_table(rel_bias)
    outs = []
    for bi in range(b):
        h = x[bi]
        for i in range(w_in.shape[0]):
            h = _layer(h, p[i, bi], w_in[i], g_attn_pre[i], g_q[i], g_k[i], g_out_a[i], g_out_b[i],
                       w_out[i], g_attn_post[i], bias_t, g_mlp_pre[i], w_ff1[i], w_ff2[i],
                       g_mlp_post[i], g_ple[i], w_ple_gate[i], w_ple_proj[i], cos_t, sin_t)
        outs.append(h)
    return jnp.stack(outs, axis=0)
```

```python
import functools
import math

import jax
import jax.numpy as jnp
import numpy as np
from jax import lax
from jax.experimental import pallas as pl
from jax.experimental.pallas import tpu as pltpu

D_MODEL = 1024
HEAD_DIM = 64
N_HEADS_A = 8
N_KV_A = 2
N_HEADS_B = 8
D_A = N_HEADS_A * HEAD_DIM
D_KV_A = N_KV_A * HEAD_DIM
D_B = N_HEADS_B * HEAD_DIM
D_FF = 4 * D_MODEL
D_PLE = 256
GRID_W = 64
ROPE_THETA = 10000.0
ROPE_HALF = HEAD_DIM // 2
DILATED_PATTERNS = ((128, 1), (512, 4), (2048, 16))
N_BUCKETS = 32
MAX_DISTANCE = 1024
EPS = 1e-6
NEG_BIG = -1e30

LOG2E = math.log2(math.e)
Q_SCALE = HEAD_DIM ** -0.5 * LOG2E

MXU_TILE = 256
ONES_ROWS = 16
V_ROWS = HEAD_DIM + ONES_ROWS
VMEM_LIMIT = 56 * 1024 * 1024

TM_PROJ = 1024
BQ_A = 128
NK_A = 256
UNROLL_A = 8
ROUNDING_SLACK = (1.0 + 2.0 ** -8) ** 2
SHIFT_SPAN_LIMIT = 96.0
BQ_B = 256
NK_B = 256
QB_B = 4
B_REACH = max(w for w, _ in DILATED_PATTERNS) // 2
TM_TAIL = 512
FF_CHUNK = 1024

_T_QA, _T_KA, _T_VA = 0, D_A, D_A + D_KV_A
_T_QB, _T_KB, _T_VB = D_A + 2 * D_KV_A, D_A + 2 * D_KV_A + D_B, D_A + 2 * D_KV_A + 2 * D_B
_T_ROWS = D_A + 2 * D_KV_A + 3 * D_B


def _rms_rows(x, g_row):
    ms = jnp.mean(x * x, axis=-1, keepdims=True)
    return x * lax.rsqrt(ms + EPS) * g_row


def _proj_kernel(x_ref, g_ref, wt_ref, gq_ref, gk_ref, cos_ref, sin_ref,
                 qa_ref, ka_ref, va_ref, qb_ref, kb_ref, vb_ref, len_ref):
    xn = _rms_rows(x_ref[...], g_ref[...]).astype(jnp.bfloat16)
    cuts = (0, _T_KA, _T_QB, _T_KB, _T_VB, _T_ROWS)
    groups = {}

    def rows(lo, hi):
        for g_lo, g_hi in zip(cuts[:-1], cuts[1:]):
            if g_lo <= lo and hi <= g_hi:
                if g_lo not in groups:
                    groups[g_lo] = lax.dot_general(wt_ref[g_lo:g_hi, :], xn, (((1,), (1,)), ((), ())),
                                                   preferred_element_type=jnp.float32)
                return groups[g_lo][lo - g_lo:hi - g_lo]
        raise ValueError((lo, hi))

    cos = cos_ref[...]
    sin = sin_ref[...]

    def norm_rope(blk, g_col):
        ms = jnp.mean(blk * blk, axis=0, keepdims=True)
        y = blk * lax.rsqrt(ms + EPS) * g_col
        y1, y2 = y[:ROPE_HALF], y[ROPE_HALF:]
        return jnp.concatenate([y1 * cos - y2 * sin, y2 * cos + y1 * sin], axis=0)

    def longest2(heads):
        out = None
        for v in heads:
            n2 = jnp.sum(v * v, axis=0, keepdims=True)
            out = n2 if out is None else jnp.maximum(out, n2)
        return out

    gq = gq_ref[...]
    gk = gk_ref[...]
    q_heads = []
    for h in range(N_HEADS_A):
        r0 = _T_QA + h * HEAD_DIM
        q_heads.append(norm_rope(rows(r0, r0 + HEAD_DIM), gq) * Q_SCALE)
        qa_ref[h * HEAD_DIM:(h + 1) * HEAD_DIM, :] = q_heads[-1].astype(jnp.bfloat16)
    k_heads = [norm_rope(rows(_T_KA + g * HEAD_DIM, _T_KA + (g + 1) * HEAD_DIM), gk)
               for g in range(N_KV_A)]
    ka_ref[...] = jnp.concatenate(k_heads, axis=0).T.astype(jnp.bfloat16)

    qb = rows(_T_QB, _T_QB + D_B) * Q_SCALE
    kb = rows(_T_KB, _T_KB + D_B)
    qb_ref[...] = qb.astype(jnp.bfloat16)
    kb_ref[...] = kb.T.astype(jnp.bfloat16)
    split = lambda v: [v[h * HEAD_DIM:(h + 1) * HEAD_DIM] for h in range(N_HEADS_B)]
    len_ref[0:1, :] = longest2(q_heads) * ROUNDING_SLACK
    len_ref[1:2, :] = longest2(k_heads) * ROUNDING_SLACK
    len_ref[2:3, :] = longest2(split(qb)) * ROUNDING_SLACK
    len_ref[3:4, :] = longest2(split(kb)) * ROUNDING_SLACK
    len_ref[4:8, :] = jnp.zeros((4, len_ref.shape[1]), jnp.float32)

    tm = x_ref.shape[0]
    def store_slabs(v_ref, v_t, n_heads, nk):
        for c in range(tm // nk):
            for h in range(n_heads):
                r0 = h * V_ROWS
                v_ref[c, r0:r0 + HEAD_DIM, :] = v_t[h * HEAD_DIM:(h + 1) * HEAD_DIM,
                                                    c * nk:(c + 1) * nk]
                v_ref[c, r0 + HEAD_DIM:r0 + V_ROWS, :] = jnp.ones((ONES_ROWS, nk), jnp.bfloat16)

    store_slabs(va_ref, rows(_T_VA, _T_VA + D_KV_A).astype(jnp.bfloat16), N_KV_A, NK_A)
    store_slabs(vb_ref, rows(_T_VB, _T_VB + D_B).astype(jnp.bfloat16), N_HEADS_B, NK_B)


def _proj_call(x2, g_pre, wt, gq, gk, cos_t, sin_t):
    s_len = x2.shape[0]
    tm = TM_PROJ
    const = lambda i: (0, 0)
    return pl.pallas_call(
        _proj_kernel,
        grid=(s_len // tm,),
        in_specs=[
            pl.BlockSpec((tm, D_MODEL), lambda i: (i, 0)),
            pl.BlockSpec((1, D_MODEL), const),
            pl.BlockSpec((_T_ROWS, D_MODEL), const),
            pl.BlockSpec((HEAD_DIM, 1), const),
            pl.BlockSpec((HEAD_DIM, 1), const),
            pl.BlockSpec((ROPE_HALF, tm), lambda i: (0, i)),
            pl.BlockSpec((ROPE_HALF, tm), lambda i: (0, i)),
        ],
        out_specs=[
            pl.BlockSpec((D_A, tm), lambda i: (0, i)),
            pl.BlockSpec((tm, D_KV_A), lambda i: (i, 0)),
            pl.BlockSpec((tm // NK_A, V_ROWS * N_KV_A, NK_A), lambda i: (i, 0, 0)),
            pl.BlockSpec((D_B, tm), lambda i: (0, i)),
            pl.BlockSpec((tm, D_B), lambda i: (i, 0)),
            pl.BlockSpec((tm // NK_B, V_ROWS * N_HEADS_B, NK_B), lambda i: (i, 0, 0)),
            pl.BlockSpec((8, tm), lambda i: (0, i)),
        ],
        out_shape=[
            jax.ShapeDtypeStruct((D_A, s_len), jnp.bfloat16),
            jax.ShapeDtypeStruct((s_len, D_KV_A), jnp.bfloat16),
            jax.ShapeDtypeStruct((s_len // NK_A, V_ROWS * N_KV_A, NK_A),
                                 jnp.bfloat16),
            jax.ShapeDtypeStruct((D_B, s_len), jnp.bfloat16),
            jax.ShapeDtypeStruct((s_len, D_B), jnp.bfloat16),
            jax.ShapeDtypeStruct((s_len // NK_B, V_ROWS * N_HEADS_B, NK_B),
                                 jnp.bfloat16),
            jax.ShapeDtypeStruct((8, s_len), jnp.float32),
        ],
        compiler_params=pltpu.CompilerParams(
            dimension_semantics=("arbitrary",), vmem_limit_bytes=VMEM_LIMIT),
        name="proj",
    )(x2, g_pre, wt, gq, gk, cos_t, sin_t)


def _probs(s, m):
    return jnp.exp2(s - m).astype(jnp.bfloat16)


def _pv_product(p, v_t):
    groups = v_t.shape[0] // V_ROWS
    width = p.shape[1] // groups
    pv = [jnp.dot(v_t[g * V_ROWS:(g + 1) * V_ROWS], p[:, g * width:(g + 1) * width],
                  preferred_element_type=jnp.float32) for g in range(groups)]
    return jnp.concatenate(pv, axis=1)


def _softmax_step(s, v_t, m, acc_ref):
    m_new = jnp.maximum(m, jnp.max(s, axis=0, keepdims=True))
    acc_ref[...] = acc_ref[...] * jnp.exp2(m - m_new) + _pv_product(_probs(s, m_new), v_t)
    return m_new


def _logits_tiles(k_rows, qt_ref, bias_ref, s_ref, n_ct):
    maxima = []
    for ct in range(n_ct):
        cols = slice(ct * MXU_TILE, (ct + 1) * MXU_TILE)
        s = jnp.dot(k_rows, qt_ref[:, cols], preferred_element_type=jnp.float32)
        if bias_ref is not None:
            s = s + bias_ref[:, cols]
        s_ref[:, cols] = s
        maxima.append(jnp.max(s_ref[:, cols], axis=0, keepdims=True))
    return maxima


def _softmax_tiles(s_ref, mx, v_tiles, m, acc_ref):
    m_out = []
    for ct in range(len(mx)):
        cols = slice(ct * MXU_TILE, (ct + 1) * MXU_TILE)
        pv = None
        m_new = mx[ct] if m is None else jnp.maximum(m[ct], mx[ct])
        for r, v_t in enumerate(v_tiles):
            p = _probs(s_ref[r * MXU_TILE:(r + 1) * MXU_TILE, cols], m_new)
            part = jnp.dot(v_t[ct], p, preferred_element_type=jnp.float32)
            pv = part if pv is None else pv + part
        if m is None:
            acc_ref[:, cols] = pv
        else:
            acc_ref[:, cols] = acc_ref[:, cols] * jnp.exp2(m[ct] - m_new) + pv
        m_out.append(m_new)
    return m_out


def _attn_a_kernel(q_ref, k_ref, v_ref, *refs, bounded):
    if bounded:
        klen_ref, o_ref, qt_ref, acc_ref, *s_refs = refs
    else:
        o_ref, qt_ref, acc_ref, *s_refs = refs
    bq = q_ref.shape[1]
    m_cols = N_HEADS_A * bq
    grp = N_HEADS_A // N_KV_A
    zeros = jnp.zeros((HEAD_DIM, bq), jnp.bfloat16)
    q_len = []
    for h in range(N_HEADS_A):
        qh = q_ref[h * HEAD_DIM:(h + 1) * HEAD_DIM, :]
        parts = [qh if g == h // grp else zeros for g in range(N_KV_A)]
        qt_ref[:, h * bq:(h + 1) * bq] = jnp.concatenate(parts, axis=0)
        qf = qh.astype(jnp.float32)
        q_len.append(jnp.sqrt(jnp.sum(qf * qf, axis=0, keepdims=True)))
    n_chunks = k_ref.shape[0] // NK_A

    def logits(c):
        start = c * NK_A if isinstance(c, int) else pl.multiple_of(c * NK_A, NK_A)
        return jnp.dot(k_ref[pl.ds(start, NK_A), :], qt_ref[...],
                       preferred_element_type=jnp.float32)

    s_refs[0][...] = logits(0)

    if bounded:
        shift = jnp.concatenate(q_len, axis=1) * klen_ref[...]
        acc = None
        for c in range(n_chunks):
            if c + 1 < n_chunks:
                s_refs[(c + 1) % 2][...] = logits(c + 1)
            pv = _pv_product(_probs(s_refs[c % 2][...], shift), v_ref[c])
            acc = pv if acc is None else acc + pv
    else:
        acc_ref[...] = jnp.zeros_like(acc_ref)

        def body(t, m):
            c0 = t * UNROLL_A
            for u in range(UNROLL_A):
                nxt = jnp.minimum(c0 + u + 1, n_chunks - 1)
                s_refs[(u + 1) % 2][...] = logits(nxt)
                m = _softmax_step(s_refs[u % 2][...], v_ref[c0 + u], m, acc_ref)
            return m

        lax.fori_loop(0, n_chunks // UNROLL_A, body, jnp.full((1, m_cols), NEG_BIG, jnp.float32))
        acc = acc_ref[...]
    inv = 1.0 / acc[HEAD_DIM:HEAD_DIM + 1, :]
    for h in range(N_HEADS_A):
        cols = slice(h * bq, (h + 1) * bq)
        o_ref[h * HEAD_DIM:(h + 1) * HEAD_DIM, :] = acc[:HEAD_DIM, cols] * inv[:, cols]


def _attn_a_call(qa_t, ka, va_t, k_len, bounded):
    s_len = ka.shape[0]
    bq = BQ_A
    operands = (qa_t, ka, va_t) + ((k_len,) if bounded else ())
    return pl.pallas_call(
        functools.partial(_attn_a_kernel, bounded=bounded),
        grid=(s_len // bq,),
        in_specs=[
            pl.BlockSpec((D_A, bq), lambda i: (0, i)),
            pl.BlockSpec((s_len, D_KV_A), lambda i: (0, 0)),
            pl.BlockSpec((s_len // NK_A, V_ROWS * N_KV_A, NK_A), lambda i: (0, 0, 0)),
        ] + ([pl.BlockSpec((1, 1), lambda i: (0, 0))] if bounded else []),
        out_specs=pl.BlockSpec((D_A, bq), lambda i: (0, i)),
        out_shape=jax.ShapeDtypeStruct((D_A, s_len), jnp.float32),
        scratch_shapes=[
            pltpu.VMEM((D_KV_A, N_HEADS_A * bq), jnp.bfloat16),
            pltpu.VMEM((V_ROWS, N_HEADS_A * bq), jnp.float32),
        ] + [pltpu.VMEM((NK_A, N_HEADS_A * bq), jnp.float32)] * 2,
        compiler_params=pltpu.CompilerParams(
            dimension_semantics=("arbitrary",), vmem_limit_bytes=VMEM_LIMIT),
        name="attn_a_bounded" if bounded else "attn_a_online",
    )(*operands)


def _attn_b_kernel(q_ref, k_ref, v_ref, frev_ref, *refs, bounded):
    if bounded:
        stats_ref, o_ref, qt_ref, acc_ref, bias_ref, *s_refs = refs
    else:
        o_ref, qt_ref, acc_ref, bias_ref, s_ref = refs
    step = pl.program_id(1)
    bq = BQ_B
    n_chunks = k_ref.shape[0] // NK_B
    back = B_REACH // NK_B
    span = (bq + 2 * B_REACH) // NK_B

    def first_chunk(sb):
        return (step * QB_B + sb) * (bq // NK_B) - back

    def is_interior(sb):
        return jnp.logical_and(first_chunk(sb) >= 0, first_chunk(sb) + span <= n_chunks)

    @pl.when(step == 0)
    def _build_bias():
        for hh in range(2):
            for c in range(span):
                seg = (span - 1 - c) * NK_B
                row = frev_ref[hh:hh + 1, seg:seg + 2 * bq]
                rolled = pltpu.roll(jnp.broadcast_to(row, (NK_B, 2 * bq)), 0, 1,
                                    stride=1, stride_axis=0)
                bias_ref[c * NK_B:(c + 1) * NK_B, hh * bq:(hh + 1) * bq] = rolled[:, bq:]

    def stage_queries(sb):
        zeros = jnp.zeros((HEAD_DIM, bq), jnp.bfloat16)
        q0 = q_ref[:HEAD_DIM, sb * bq:(sb + 1) * bq]
        q1 = q_ref[HEAD_DIM:, sb * bq:(sb + 1) * bq]
        qt_ref[:, 2 * sb * bq:(2 * sb + 1) * bq] = jnp.concatenate([q0, zeros], axis=0)
        qt_ref[:, (2 * sb + 1) * bq:(2 * sb + 2) * bq] = jnp.concatenate([zeros, q1], axis=0)
        return q0, q1

    def write(sb, acc):
        inv = 1.0 / acc[HEAD_DIM:HEAD_DIM + 1, :]
        o_ref[:HEAD_DIM, sb * bq:(sb + 1) * bq] = acc[:HEAD_DIM, :bq] * inv[:, :bq]
        o_ref[HEAD_DIM:, sb * bq:(sb + 1) * bq] = acc[:HEAD_DIM, bq:] * inv[:, bq:]

    def logits(sb, c):
        k_start = pl.multiple_of((first_chunk(sb) + c) * NK_B, NK_B)
        b_start = c * NK_B if isinstance(c, int) else pl.multiple_of(c * NK_B, NK_B)
        s = jnp.dot(k_ref[pl.ds(k_start, NK_B), :], qt_ref[:, 2 * sb * bq:(2 * sb + 2) * bq],
                    preferred_element_type=jnp.float32)
        return s + bias_ref[pl.ds(b_start, NK_B), :]

    def clipped_bounds(sb):
        return jnp.maximum(0, -first_chunk(sb)), jnp.minimum(span, n_chunks - first_chunk(sb))

    if bounded:
        def stage_and_shift(sb):
            stats = stats_ref[...]
            widen = lambda r: jnp.concatenate([stats[r:r + 1, :]] * (bq // stats.shape[1]), axis=1)
            shift = []
            for hh, q in enumerate(stage_queries(sb)):
                qf = q.astype(jnp.float32)
                shift.append(jnp.sqrt(jnp.sum(qf * qf, axis=0, keepdims=True)) * widen(2) + widen(hh))
            return jnp.concatenate(shift, axis=1)

        def pipeline(chunks_of):
            shifts = [stage_and_shift(sb) for sb in range(QB_B)]
            order = [(sb, c) for sb in range(QB_B) for c in chunks_of(sb)]
            s_refs[0][...] = logits(*order[0])
            total = None
            for n, (sb, c) in enumerate(order):
                if n + 1 < len(order):
                    s_refs[(n + 1) % 2][...] = logits(*order[n + 1])
                pv = _pv_product(_probs(s_refs[n % 2][...], shifts[sb]), v_ref[first_chunk(sb) + c])
                total = pv if c == chunks_of(sb)[0] else total + pv
                if c == chunks_of(sb)[-1]:
                    write(sb, total)

        last = pl.num_programs(1) - 1
        blocks_back = back * NK_B // bq

        @pl.when(step == 0)
        def _sequence_start():
            pipeline(lambda sb: range((blocks_back - sb) * (bq // NK_B), span))

        @pl.when(step == last)
        def _sequence_end():
            pipeline(lambda sb: range(min(span, (QB_B - sb) * (bq // NK_B) + back)))

        @pl.when(jnp.logical_and(step > 0, step < last))
        def _whole_windows():
            pipeline(lambda sb: range(span))
    else:
        for sb in range(QB_B):
            @pl.when(is_interior(sb))
            def _whole_window(sb=sb):
                stage_queries(sb)
                n_ct = 2 * bq // MXU_TILE
                k_start = pl.multiple_of(first_chunk(sb) * NK_B, NK_B)
                mx = _logits_tiles(k_ref[pl.ds(k_start, span * NK_B), :],
                                   qt_ref.at[:, 2 * sb * bq:(2 * sb + 2) * bq], bias_ref, s_ref, n_ct)
                v_tiles = [[v_ref[first_chunk(sb) + c, hh * V_ROWS:(hh + 1) * V_ROWS, :]
                            for hh in range(n_ct)] for c in range(span)]
                _softmax_tiles(s_ref, mx, v_tiles, None, acc_ref)
                write(sb, acc_ref[...])

            @pl.when(jnp.logical_not(is_interior(sb)))
            def _clipped_window(sb=sb):
                acc_ref[...] = jnp.zeros_like(acc_ref)
                stage_queries(sb)

                def body(c, m):
                    return _softmax_step(logits(sb, c), v_ref[first_chunk(sb) + c], m, acc_ref)

                lax.fori_loop(*clipped_bounds(sb), body,
                              jnp.full((1, 2 * bq), NEG_BIG, jnp.float32))
                write(sb, acc_ref[...])


def _attn_b_call(qb_t, kb, vb_t, frev, stats, bounded):
    s_len = kb.shape[0]
    bq = BQ_B
    pairs = N_HEADS_B // 2
    span_rows = bq + 2 * B_REACH
    assert bq == NK_B == MXU_TILE and frev.shape[-1] == span_rows + bq
    assert B_REACH == bq * QB_B and s_len % (bq * QB_B) == 0 and s_len // (bq * QB_B) >= 2
    operands = (qb_t, kb, vb_t, frev) + ((stats,) if bounded else ())
    logits_scratch = ([pltpu.VMEM((NK_B, 2 * bq), jnp.float32)] * 2 if bounded
                      else [pltpu.VMEM((span_rows, 2 * bq), jnp.float32)])
    return pl.pallas_call(
        functools.partial(_attn_b_kernel, bounded=bounded),
        grid=(pairs, s_len // (bq * QB_B)),
        in_specs=[
            pl.BlockSpec((2 * HEAD_DIM, bq * QB_B), lambda j, i: (j, i)),
            pl.BlockSpec((s_len, 2 * HEAD_DIM), lambda j, i: (0, j)),
            pl.BlockSpec((s_len // NK_B, 2 * V_ROWS, NK_B), lambda j, i: (0, j, 0)),
            pl.BlockSpec((None, 2, span_rows + bq), lambda j, i: (j, 0, 0)),
        ] + ([pl.BlockSpec((None, 8, 128), lambda j, i: (j, 0, 0))] if bounded else []),
        out_specs=pl.BlockSpec((2 * HEAD_DIM, bq * QB_B), lambda j, i: (j, i)),
        out_shape=jax.ShapeDtypeStruct((D_B, s_len), jnp.float32),
        scratch_shapes=[
            pltpu.VMEM((2 * HEAD_DIM, 2 * bq * QB_B), jnp.bfloat16),
            pltpu.VMEM((V_ROWS, 2 * bq), jnp.float32),
            pltpu.VMEM((span_rows, 2 * bq), jnp.float32),
        ] + logits_scratch,
        compiler_params=pltpu.CompilerParams(
            dimension_semantics=("arbitrary", "arbitrary"), vmem_limit_bytes=VMEM_LIMIT),
        name="attn_b_bounded" if bounded else "attn_b_online",
    )(*operands)


def _mixers(qa_t, ka, va_t, qb_t, kb, vb_t, bias, len2):
    frev, bias_max, bias_self = bias
    qa_len, ka_len, qb_len, kb_len = jnp.sqrt(jnp.max(len2[:4], axis=1))
    safe_a = 2.0 * qa_len * ka_len <= SHIFT_SPAN_LIMIT
    safe_b = 2.0 * qb_len * kb_len + jnp.max(bias_max - bias_self) <= SHIFT_SPAN_LIMIT
    pairs = N_HEADS_B // 2
    stats = jnp.concatenate([bias_max.reshape(pairs, 2), jnp.broadcast_to(kb_len, (pairs, 1)),
                             jnp.zeros((pairs, 5), jnp.float32)], axis=1)
    stats = jnp.broadcast_to(stats[:, :, None], (pairs, 8, 128))
    ops_a = (qa_t, ka, va_t, ka_len.reshape(1, 1))
    ops_b = (qb_t, kb, vb_t, frev, stats)
    call_a = lambda bounded: functools.partial(_attn_a_call, bounded=bounded)
    call_b = lambda bounded: functools.partial(_attn_b_call, bounded=bounded)

    def both_bounded(ops_a, ops_b):
        return call_a(True)(*ops_a), call_b(True)(*ops_b)

    def each_as_safe(ops_a, ops_b):
        return (lax.cond(safe_a, call_a(True), call_a(False), *ops_a),
                lax.cond(safe_b, call_b(True), call_b(False), *ops_b))

    return lax.cond(jnp.logical_and(safe_a, safe_b), both_bounded, each_as_safe, ops_a, ops_b)


def _t5_bucket_index(rel):
    nb = N_BUCKETS // 2
    max_exact = nb // 2
    side = jnp.where(rel > 0, nb, 0)
    n = jnp.abs(rel)
    large = max_exact + (jnp.log(jnp.maximum(n, max_exact).astype(jnp.float32) / max_exact)
                         / math.log(MAX_DISTANCE / max_exact) * (nb - max_exact)).astype(jnp.int32)
    large = jnp.minimum(large, nb - 1)
    return side + jnp.where(n < max_exact, n, large)


def _dilated_bias_table(rel_bias):
    bq = BQ_B
    reach_all = B_REACH + bq - 1
    delta = jnp.arange(reach_all, -reach_all - 1, -1)
    count = jnp.zeros(delta.shape, jnp.float32)
    for window, dilation in DILATED_PATTERNS:
        inside = (delta % dilation == 0) & (jnp.abs(delta) <= window // 2)
        count = count + inside.astype(jnp.float32)
    table = rel_bias[_t5_bucket_index(delta)].astype(jnp.float32)
    table = jnp.where((count > 0)[:, None],
                      (table + jnp.log(jnp.maximum(count, 1.0))[:, None]) * LOG2E, NEG_BIG)
    frev = jnp.pad(table.T, ((0, 0), (1, 0)))
    frev = frev.reshape(N_HEADS_B // 2, 2, frev.shape[1])
    bias_max = jnp.max(jnp.where((count > 0)[:, None], table, -jnp.inf), axis=0)
    return frev, bias_max, table[reach_all]


def _tail_kernel(ya_ref, yb_ref, x_ref, p_ref, ga_ref, gb_ref, wo_ref, gpost_ref,
                 g1_ref, w1_ref, w2_ref, g2_ref, g3_ref, wg_ref, wp_ref, o_ref):
    def norm_t(y_t, g_col):
        ms = jnp.mean(y_t * y_t, axis=0, keepdims=True)
        return (y_t * lax.rsqrt(ms + EPS) * g_col).astype(jnp.bfloat16)

    y_t = jnp.concatenate([norm_t(ya_ref[...], ga_ref[...]),
                           norm_t(yb_ref[...], gb_ref[...])], axis=0)
    y = lax.dot_general(y_t, wo_ref[...], (((0,), (0,)), ((), ())),
                        preferred_element_type=jnp.float32)
    h = x_ref[...] + _rms_rows(y, gpost_ref[...])

    xn = _rms_rows(h, g1_ref[...]).astype(jnp.bfloat16)
    f = None
    for c in range(D_FF // FF_CHUNK):
        cols = slice(c * FF_CHUNK, (c + 1) * FF_CHUNK)
        u = jnp.dot(xn, w1_ref[:, cols], preferred_element_type=jnp.float32)
        u = jnp.square(jnp.maximum(u, 0.0)).astype(jnp.bfloat16)
        part = jnp.dot(u, w2_ref[cols, :], preferred_element_type=jnp.float32)
        f = part if f is None else f + part
    h = h + _rms_rows(f, g2_ref[...])
    gate_in = _rms_rows(h, g3_ref[...]).astype(jnp.bfloat16)
    gate = jax.nn.sigmoid(jnp.dot(gate_in, wg_ref[...], preferred_element_type=jnp.float32))
    emb = jnp.dot(p_ref[...].astype(jnp.bfloat16), wp_ref[...], preferred_element_type=jnp.float32)
    o_ref[...] = h + gate * emb


def _tail_call(ya_t, yb_t, x2, p2, ga, gb, w_out, g_post, g1, w1, w2, g2, g3, wg, wp):
    s_len = x2.shape[0]
    tm = TM_TAIL
    const = lambda i: (0, 0)
    resident = functools.partial(pl.BlockSpec, index_map=const, pipeline_mode=pl.Buffered(1))
    row_vec = pl.BlockSpec((1, D_MODEL), const)
    return pl.pallas_call(
        _tail_kernel,
        grid=(s_len // tm,),
        in_specs=[
            pl.BlockSpec((D_A, tm), lambda i: (0, i)),
            pl.BlockSpec((D_B, tm), lambda i: (0, i)),
            pl.BlockSpec((tm, D_MODEL), lambda i: (i, 0)),
            pl.BlockSpec((tm, D_PLE), lambda i: (i, 0)),
            pl.BlockSpec((D_A, 1), const),
            pl.BlockSpec((D_B, 1), const),
            resident((D_A + D_B, D_MODEL)),
            row_vec,
            row_vec,
            resident((D_MODEL, D_FF)),
            resident((D_FF, D_MODEL)),
            row_vec,
            row_vec,
            resident((D_MODEL, D_MODEL)),
            resident((D_PLE, D_MODEL)),
        ],
        out_specs=pl.BlockSpec((tm, D_MODEL), lambda i: (i, 0)),
        out_shape=jax.ShapeDtypeStruct((s_len, D_MODEL), jnp.float32),
        compiler_params=pltpu.CompilerParams(
            dimension_semantics=("arbitrary",), vmem_limit_bytes=VMEM_LIMIT),
        name="tail",
    )(ya_t, yb_t, x2, p2, ga, gb, w_out, g_post, g1, w1, w2, g2, g3, wg, wp)


def _rope_tables_t(n_tokens):
    tok = np.arange(n_tokens)
    n_axis = ROPE_HALF // 2
    inv_freq = ROPE_THETA ** (-np.arange(n_axis, dtype=np.float64) / n_axis)
    ang = np.concatenate([inv_freq[:, None] * (tok // GRID_W)[None, :],
                          inv_freq[:, None] * (tok % GRID_W)[None, :]], axis=0)
    return jnp.asarray(np.cos(ang), jnp.float32), jnp.asarray(np.sin(ang), jnp.float32)


def _layer(h, p_i, w_in, g_attn_pre, g_q, g_k, g_out_a, g_out_b, w_out, g_attn_post, bias_t,
           g_mlp_pre, w_ff1, w_ff2, g_mlp_post, g_ple, w_ple_gate, w_ple_proj, cos_t, sin_t):
    bf = jnp.bfloat16
    row = lambda g: g.reshape(1, -1)
    col = lambda g: g.reshape(-1, 1)

    qa_t, ka, va_t, qb_t, kb, vb_t, len2 = _proj_call(
        h, row(g_attn_pre), w_in.T.astype(bf), col(g_q), col(g_k), cos_t, sin_t)
    ya_t, yb_t = _mixers(qa_t, ka, va_t, qb_t, kb, vb_t, bias_t, len2)
    return _tail_call(ya_t, yb_t, h, p_i, col(g_out_a), col(g_out_b), w_out.astype(bf),
                      row(g_attn_post), row(g_mlp_pre), w_ff1.astype(bf), w_ff2.astype(bf),
                      row(g_mlp_post), row(g_ple), w_ple_gate.astype(bf), w_ple_proj.astype(bf))


def kernel(x, p, w_in, g_attn_pre, g_q, g_k, g_out_a, g_out_b, w_out, g_attn_post, rel_bias,
           g_mlp_pre, w_ff1, w_ff2, g_mlp_post, g_ple, w_ple_gate, w_ple_proj):
    b, s_len, _ = x.shape
    cos_t, sin_t = _rope_tables_t(s_len)
    bias_t = _dilated_bias_table(rel_bias)
    outs = []
    for bi in range(b):
        h = x[bi]
        for i in range(w_in.shape[0]):
            h = _layer(h, p[i, bi], w_in[i], g_attn_pre[i], g_q[i], g_k[i], g_out_a[i], g_out_b[i],
                       w_out[i], g_attn_post[i], bias_t, g_mlp_pre[i], w_ff1[i], w_ff2[i],
                       g_mlp_post[i], g_ple[i], w_ple_gate[i], w_ple_proj[i], cos_t, sin_t)
        outs.append(h)
    return jnp.stack(outs, axis=0)
```
